```python
import math
import jax, jax.numpy as jnp
from jax import lax
import numpy as np

D_MODEL = 1024
BATCH = 32
SEQ = 2048
DEPTH = 2

M_HEADS = 4
M_HEAD_DIM = D_MODEL // 8
M_WIDTH = M_HEADS * M_HEAD_DIM
M_CHUNK = 64
DA_HEADS = 4
DA_HEAD_DIM = D_MODEL // 16
DA_V_DIM = 2 * DA_HEAD_DIM
DA_QK_WIDTH = DA_HEADS * 2 * DA_HEAD_DIM
DA_WIDTH = DA_HEADS * DA_V_DIM
Q_BLOCK = 128
REL_BUCKETS = 32
REL_MAX_DIST = 128
CONV_WIDTH = 3
CONV_DIM = D_MODEL
MOE_GROUPS = 4
EXPERTS_PER_GROUP = 4
N_EXPERTS = MOE_GROUPS * EXPERTS_PER_GROUP
EXPERT_TOP_K = 2
EXPERT_FF = D_MODEL // 2

NORM_EPS = 1e-6
N_EVEN_LAYERS = (DEPTH + 1) // 2
N_ODD_LAYERS = DEPTH // 2
EVEN_SPLITS = [M_WIDTH, M_WIDTH, M_WIDTH, M_WIDTH, M_HEADS, M_HEADS, DA_QK_WIDTH, DA_QK_WIDTH, DA_WIDTH]
EVEN_IN = sum(EVEN_SPLITS)

kernel_name = "hybrid_mlstm_diffattn_shortconv_hmoe"


def rmsnorm(x, g):
    xf = x.astype(jnp.float32)
    y = xf * lax.rsqrt(jnp.mean(xf * xf, axis=-1, keepdims=True) + NORM_EPS)
    return (y * g.astype(jnp.float32)).astype(x.dtype)


def rel_bucket(rel):
    n = jnp.maximum(-rel, 0)
    max_exact = REL_BUCKETS // 2
    nf = jnp.maximum(n, 1).astype(jnp.float32)
    large = max_exact + (jnp.log(nf / max_exact) / math.log(REL_MAX_DIST / max_exact)
                         * (REL_BUCKETS - max_exact)).astype(jnp.int32)
    large = jnp.minimum(large, REL_BUCKETS - 1)
    return jnp.where(n < max_exact, n, large)


def mlstm(q, k, v, ig, fg):
    Bn, S, H, dh = q.shape
    L = M_CHUNK
    nc = S // L
    to_chunks = lambda a: a.reshape(Bn, nc, L, H, dh).transpose(1, 0, 3, 2, 4)
    g_chunks = lambda a: a.reshape(Bn, nc, L, H).transpose(1, 0, 3, 2)
    qc, kc, vc = to_chunks(q), to_chunks(k * (dh ** -0.5)), to_chunks(v)
    ic, lf = g_chunks(ig), g_chunks(jax.nn.log_sigmoid(fg))
    causal = jnp.tril(jnp.ones((L, L), dtype=bool))

    def step(carry, xs):
        C, n, m = carry
        qt, kt, vt, it, lft = xs
        b = jnp.cumsum(lft, axis=-1)
        g = b[..., -1]
        dmat = jnp.where(causal, b[..., :, None] - b[..., None, :] + it[..., None, :], -jnp.inf)
        inter = b + m[..., None]
        mt = jnp.maximum(inter, jnp.max(dmat, axis=-1))
        w = jnp.exp(dmat - mt[..., None])
        sc = jnp.exp(inter - mt)
        qk = jnp.einsum('bhtd,bhsd->bhts', qt, kt) * w
        num = sc[..., None] * jnp.einsum('bhtd,bhde->bhte', qt, C) + jnp.einsum('bhts,bhse->bhte', qk, vt)
        den = sc * jnp.einsum('bhtd,bhd->bht', qt, n) + jnp.sum(qk, axis=-1)
        h = num / jnp.maximum(jnp.abs(den), jnp.exp(-mt))[..., None]
        ws = g[..., None] - b + it
        m_new = jnp.maximum(g + m, jnp.max(ws, axis=-1))
        a = jnp.exp(g + m - m_new)
        wk = jnp.exp(ws - m_new[..., None])[..., None] * kt
        C_new = a[..., None, None] * C + jnp.einsum('bhsd,bhse->bhde', wk, vt)
        n_new = a[..., None] * n + jnp.sum(wk, axis=-2)
        return (C_new, n_new, m_new), h

    init = (jnp.zeros((Bn, H, dh, dh), jnp.float32), jnp.zeros((Bn, H, dh), jnp.float32),
            jnp.zeros((Bn, H), jnp.float32))
    _, hs = lax.scan(step, init, (qc, kc, vc, ic, lf))
    return hs.transpose(1, 0, 3, 2, 4).reshape(Bn, S, H, dh)


def diff_attention(dq, dk, dv, rel_bias, lq1, lk1, lq2, lk2, subln, layer):
    Bn, S, _ = dq.shape
    H, d = DA_HEADS, DA_HEAD_DIM
    nb = S // Q_BLOCK
    q = dq.reshape(Bn, S, H, 2, d)
    k = dk.reshape(Bn, S, H, 2, d)
    q1, q2 = q[..., 0, :].transpose(0, 2, 1, 3), q[..., 1, :].transpose(0, 2, 1, 3)
    k1, k2 = k[..., 0, :].transpose(0, 2, 1, 3), k[..., 1, :].transpose(0, 2, 1, 3)
    v = dv.reshape(Bn, S, H, DA_V_DIM).transpose(0, 2, 1, 3)
    lam_init = 0.8 - 0.6 * math.exp(-0.3 * layer)
    lam = (jnp.exp(jnp.sum(lq1.astype(jnp.float32) * lk1.astype(jnp.float32)))
           - jnp.exp(jnp.sum(lq2.astype(jnp.float32) * lk2.astype(jnp.float32))) + lam_init)
    scale = d ** -0.5
    kpos = jnp.arange(S)
    to_blocks = lambda a: a.reshape(Bn, H, nb, Q_BLOCK, d).transpose(2, 0, 1, 3, 4)

    def block(args):
        q1b, q2b, bi = args
        qpos = bi * Q_BLOCK + jnp.arange(Q_BLOCK)
        rel = kpos[None, :] - qpos[:, None]
        bias = rel_bias[rel_bucket(rel)].astype(jnp.float32).transpose(2, 0, 1)
        mask = rel <= 0
        s1 = jnp.einsum('bhqd,bhkd->bhqk', q1b, k1).astype(jnp.float32) * scale + bias
        s2 = jnp.einsum('bhqd,bhkd->bhqk', q2b, k2).astype(jnp.float32) * scale + bias
        p1 = jax.nn.softmax(jnp.where(mask, s1, -jnp.inf), axis=-1)
        p2 = jax.nn.softmax(jnp.where(mask, s2, -jnp.inf), axis=-1)
        a = p1 - lam * p2
        return jnp.einsum('bhqk,bhke->bhqe', a.astype(v.dtype), v)

    out = lax.map(block, (to_blocks(q1), to_blocks(q2), jnp.arange(nb)))
    out = out.transpose(1, 2, 0, 3, 4).reshape(Bn, H, S, DA_V_DIM)
    out = rmsnorm(out, subln) * (1.0 - lam_init)
    return out.transpose(0, 2, 1, 3).reshape(Bn, S, DA_WIDTH)


def even_mixer(h, w_in, w_out, ig_b, fg_b, head_norm, rel_bias, lq1, lk1, lq2, lk2, subln, layer):
    Bn, S, _ = h.shape
    p = h @ w_in
    idx = [int(c) for c in np.cumsum(EVEN_SPLITS)[:-1]]
    mq, mk, mv, mo, mi, mf, dq, dk, dv = jnp.split(p, idx, axis=-1)
    f32 = jnp.float32
    shp = (Bn, S, M_HEADS, M_HEAD_DIM)
    hm = mlstm(mq.reshape(shp).astype(f32), mk.reshape(shp).astype(f32), mv.reshape(shp).astype(f32),
               mi.astype(f32) + ig_b.astype(f32), mf.astype(f32) + fg_b.astype(f32))
    mu = jnp.mean(hm, axis=-1, keepdims=True)
    var = jnp.mean(jnp.square(hm - mu), axis=-1, keepdims=True)
    hm = (hm - mu) * lax.rsqrt(var + NORM_EPS) * head_norm.astype(f32).reshape(M_HEADS, M_HEAD_DIM)
    hm = hm.reshape(Bn, S, M_WIDTH) * jax.nn.sigmoid(mo.astype(f32))
    ha = diff_attention(dq, dk, dv, rel_bias, lq1, lk1, lq2, lk2, subln, layer)
    return jnp.concatenate([hm.astype(h.dtype), ha.astype(h.dtype)], axis=-1) @ w_out


def conv_mixer(h, w_in, conv_w, w_out):
    bg, cg, u = jnp.split(h @ w_in, 3, axis=-1)
    z = cg * u
    zc = lax.conv_general_dilated(z, conv_w[:, None, :].astype(z.dtype), window_strides=(1,),
                                  padding=[(CONV_WIDTH - 1, 0)],
                                  dimension_numbers=('NWC', 'WIO', 'NWC'),
                                  feature_group_count=CONV_DIM)
    return (bg * zc) @ w_out


def hier_moe(h, wg, bg, we, be, w_gate, w_up, w_down):
    Bn, S, D = h.shape
    t = h.reshape(-1, D)
    T = t.shape[0]
    glog = (t @ wg).astype(jnp.float32) + bg.astype(jnp.float32)
    gprob = jax.nn.softmax(glog, axis=-1)
    _, gsel = lax.top_k(glog, 1)
    pg = jnp.take_along_axis(gprob, gsel, axis=1)[:, 0]
    elog = ((t @ we).astype(jnp.float32) + be.astype(jnp.float32)).reshape(T, MOE_GROUPS, EXPERTS_PER_GROUP)
    elog_g = jnp.take_along_axis(elog, gsel[:, :, None], axis=1)[:, 0]
    top_v, top_i = lax.top_k(elog_g, EXPERT_TOP_K)
    top_w = jax.nn.softmax(top_v, axis=-1)
    eid = gsel * EXPERTS_PER_GROUP + top_i
    combine = jnp.einsum('tk,tke->te', top_w, jax.nn.one_hot(eid, N_EXPERTS, dtype=jnp.float32)) * pg[:, None]
    combine = combine.astype(t.dtype)
    out = jnp.zeros_like(t)
    for e in range(N_EXPERTS):
        y = (jax.nn.silu(t @ w_gate[e]) * (t @ w_up[e])) @ w_down[e]
        out = out + combine[:, e:e + 1] * y
    return out.reshape(Bn, S, D)


def setup_inputs(seed: int = 0) -> dict:
    key = jax.random.key(seed)
    ks = iter(jax.random.split(key, 32))
    nrm = lambda shape, s: jax.random.normal(next(ks), shape, jnp.float32) * s
    D = D_MODEL
    fg_base = jnp.linspace(3.0, 6.0, M_HEADS, dtype=jnp.float32)
    return {
        "x": nrm((BATCH, SEQ, D), 1.0),
        "rel_bias": nrm((REL_BUCKETS, DA_HEADS), 0.5),
        "norm_mix": 1.0 + nrm((DEPTH, D), 0.02),
        "norm_ffn": 1.0 + nrm((DEPTH, D), 0.02),
        "norm_final": 1.0 + nrm((D,), 0.02),
        "w_in_even": nrm((N_EVEN_LAYERS, D, EVEN_IN), D ** -0.5),
        "w_out_even": nrm((N_EVEN_LAYERS, M_WIDTH + DA_WIDTH, D), (M_WIDTH + DA_WIDTH) ** -0.5),
        "m_igate_b": nrm((N_EVEN_LAYERS, M_HEADS), 0.1),
        "m_fgate_b": fg_base[None, :] + nrm((N_EVEN_LAYERS, M_HEADS), 0.1),
        "m_head_norm": 1.0 + nrm((N_EVEN_LAYERS, M_WIDTH), 0.02),
        "lam_q1": nrm((N_EVEN_LAYERS, DA_HEAD_DIM), 0.1),
        "lam_k1": nrm((N_EVEN_LAYERS, DA_HEAD_DIM), 0.1),
        "lam_q2": nrm((N_EVEN_LAYERS, DA_HEAD_DIM), 0.1),
        "lam_k2": nrm((N_EVEN_LAYERS, DA_HEAD_DIM), 0.1),
        "da_subln": 1.0 + nrm((N_EVEN_LAYERS, DA_V_DIM), 0.02),
        "w_in_odd": nrm((N_ODD_LAYERS, D, 3 * CONV_DIM), D ** -0.5),
        "conv_w": nrm((N_ODD_LAYERS, CONV_WIDTH, CONV_DIM), CONV_WIDTH ** -0.5),
        "w_out_odd": nrm((N_ODD_LAYERS, CONV_DIM, D), CONV_DIM ** -0.5),
        "router_group_w": nrm((DEPTH, D, MOE_GROUPS), D ** -0.5),
        "router_group_b": nrm((DEPTH, MOE_GROUPS), 0.01),
        "router_expert_w": nrm((DEPTH, D, N_EXPERTS), D ** -0.5),
        "router_expert_b": nrm((DEPTH, N_EXPERTS), 0.01),
        "exp_w_gate": nrm((DEPTH, N_EXPERTS, D, EXPERT_FF), D ** -0.5),
        "exp_w_up": nrm((DEPTH, N_EXPERTS, D, EXPERT_FF), D ** -0.5),
        "exp_w_down": nrm((DEPTH, N_EXPERTS, EXPERT_FF, D), EXPERT_FF ** -0.5),
    }


def reference(x, rel_bias, norm_mix, norm_ffn, norm_final, w_in_even, w_out_even, m_igate_b, m_fgate_b,
              m_head_norm, lam_q1, lam_k1, lam_q2, lam_k2, da_subln, w_in_odd, conv_w, w_out_odd,
              router_group_w, router_group_b, router_expert_w, router_expert_b,
              exp_w_gate, exp_w_up, exp_w_down):
    for layer in range(DEPTH):
        h = rmsnorm(x, norm_mix[layer])
        if layer % 2 == 0:
            e = layer // 2
            y = even_mixer(h, w_in_even[e], w_out_even[e], m_igate_b[e], m_fgate_b[e], m_head_norm[e],
                           rel_bias, lam_q1[e], lam_k1[e], lam_q2[e], lam_k2[e], da_subln[e], layer)
        else:
            o = layer // 2
            y = conv_mixer(h, w_in_odd[o], conv_w[o], w_out_odd[o])
        x = x + y.astype(x.dtype)
        h = rmsnorm(x, norm_ffn[layer])
        x = x + hier_moe(h, router_group_w[layer], router_group_b[layer], router_expert_w[layer],
                         router_expert_b[layer], exp_w_gate[layer], exp_w_up[layer],
                         exp_w_down[layer]).astype(x.dtype)
    return rmsnorm(x, norm_final)
```

```python
import functools
import math

import jax
import jax.numpy as jnp
import numpy as np
from jax import lax
from jax.experimental import pallas as pl
from jax.experimental.pallas import tpu as pltpu

F32 = jnp.float32
BF16 = jnp.bfloat16

NORM_EPS = 1e-6
NEG_BIG = -1e30

M_HEADS = 4
M_HEAD_DIM = 128
DA_HEADS = 4
DA_HEAD_DIM = 64
DA_V_DIM = 128
REL_BUCKETS = 32
REL_MAX_DIST = 128
MOE_GROUPS = 4
EXPERTS_PER_GROUP = 4
N_EXPERTS = 16
N_PAIRS = 6
N_BUCKETS = MOE_GROUPS * N_PAIRS

LANES = 128
SUBLANES = 8
VMEM_LIMIT = 56 * 1024 * 1024

ROW_TILE = 512
MLSTM_CHUNK = 256
ATT_BLOCK = 256
MOE_TILE = 512

_NT = (((1,), (1,)), ((), ()))


def _cparams(*sem):
    return pltpu.CompilerParams(dimension_semantics=sem, vmem_limit_bytes=VMEM_LIMIT)


def _rms(x, g):
    ms = jnp.mean(x * x, axis=-1, keepdims=True)
    return x * lax.rsqrt(ms + NORM_EPS) * g


def _dot(a, b):
    return jnp.dot(a, b, preferred_element_type=F32)


def _split3(x):
    hi = x.astype(BF16)
    r1 = x - hi.astype(F32)
    mid = r1.astype(BF16)
    lo = (r1 - mid.astype(F32)).astype(BF16)
    return hi, mid, lo


def _log_sigmoid(x):
    return jnp.minimum(x, 0.0) - jnp.log(1.0 + jnp.exp(-jnp.abs(x)))


def _bias_tiles_kernel(rb_ref, out_ref):
    h = pl.program_id(0)
    blk = out_ref.shape[-1]
    r = lax.broadcasted_iota(jnp.int32, (blk, blk), 0)
    c = lax.broadcasted_iota(jnp.int32, (blk, blk), 1)
    far = rb_ref[REL_BUCKETS - 1, h]
    max_exact = REL_BUCKETS // 2
    for j in range(2):
        rel = c - r - j * blk
        n = jnp.maximum(-rel, 0)
        nf = jnp.maximum(n, 1).astype(F32)
        large = max_exact + (jnp.log(nf / max_exact) / math.log(REL_MAX_DIST / max_exact)
                             * (REL_BUCKETS - max_exact)).astype(jnp.int32)
        large = jnp.minimum(large, REL_BUCKETS - 1)
        bucket = jnp.where(n < max_exact, n, large)
        val = jnp.zeros((blk, blk), F32)
        for b in range(REL_BUCKETS):
            val = jnp.where(bucket == b, rb_ref[b, h], val)
        val = val - far
        if j == 0:
            val = jnp.where(rel <= 0, val, NEG_BIG)
        out_ref[0, j] = val


def _bias_tiles(rel_bias, blk):
    return pl.pallas_call(
        _bias_tiles_kernel,
        out_shape=jax.ShapeDtypeStruct((DA_HEADS, 2, blk, blk), F32),
        grid=(DA_HEADS,),
        in_specs=[pl.BlockSpec(memory_space=pltpu.SMEM)],
        out_specs=pl.BlockSpec((1, 2, blk, blk), lambda h: (h, 0, 0, 0)),
        compiler_params=_cparams("arbitrary"),
        name="bias_tiles",
    )(rel_bias)


def _inproj_even_kernel(x_ref, g_ref, w_ref, wgc_ref, wgr_ref, p_ref, gc_ref, gr_ref):
    h = _rms(x_ref[...], g_ref[...]).astype(BF16)
    n = w_ref.shape[1]
    for j in range(n // 512):
        sl = slice(j * 512, (j + 1) * 512)
        p_ref[:, sl] = _dot(h, w_ref[:, sl]).astype(BF16)
    gc_ref[...] = _dot(h, wgc_ref[...])
    gr_ref[...] = lax.dot_general(wgr_ref[...], h, _NT, preferred_element_type=F32)


def _inproj_even(x2d, g, w_main, wg_col, wg_row):
    t, d = x2d.shape
    n = w_main.shape[1]
    tm = ROW_TILE
    return pl.pallas_call(
        _inproj_even_kernel,
        out_shape=(jax.ShapeDtypeStruct((t, n), BF16),
                   jax.ShapeDtypeStruct((t, LANES), F32),
                   jax.ShapeDtypeStruct((SUBLANES, t), F32)),
        grid=(t // tm,),
        in_specs=[pl.BlockSpec((tm, d), lambda i: (i, 0)),
                  pl.BlockSpec((1, d), lambda i: (0, 0)),
                  pl.BlockSpec((d, n), lambda i: (0, 0)),
                  pl.BlockSpec((d, LANES), lambda i: (0, 0)),
                  pl.BlockSpec((SUBLANES, d), lambda i: (0, 0))],
        out_specs=(pl.BlockSpec((tm, n), lambda i: (i, 0)),
                   pl.BlockSpec((tm, LANES), lambda i: (i, 0)),
                   pl.BlockSpec((SUBLANES, tm), lambda i: (0, i))),
        compiler_params=_cparams("arbitrary"),
        name="inproj_even",
    )(x2d, g, w_main, wg_col, wg_row)


def _mlstm_kernel(q_ref, k_ref, v_ref, og_ref, gc_ref, gr_ref, bc_ref, br_ref, hn_ref, out_ref,
                  c_ref, m_ref):
    L = q_ref.shape[1]
    dh = M_HEAD_DIM
    scale = dh ** -0.5

    @pl.when(pl.program_id(1) == 0)
    def _():
        c_ref[...] = jnp.zeros_like(c_ref)
        m_ref[...] = jnp.zeros_like(m_ref)

    gc = gc_ref[0] + bc_ref[...]
    gr = gr_ref[...] + br_ref[...]
    lf_c = _log_sigmoid(gc)
    lf_r = _log_sigmoid(gr)

    row = lax.broadcasted_iota(jnp.int32, (L, L), 0)
    col = lax.broadcasted_iota(jnp.int32, (L, L), 1)
    causal = row >= col
    tril = jnp.where(causal, 1.0, 0.0).astype(BF16)
    triu = jnp.where(row <= col, 1.0, 0.0).astype(BF16)
    b_c = sum(_dot(tril, part) for part in _split3(lf_c))
    b_r = sum(_dot(part, triu) for part in _split3(lf_r))

    lane = lax.broadcasted_iota(jnp.int32, (L, LANES), 1)
    ones_col = jnp.where(lane == 0, 1.0, 0.0).astype(BF16)

    for h in range(M_HEADS):
        hs = slice(h * dh, (h + 1) * dh)
        q = q_ref[0, :, hs]
        k = k_ref[0, :, hs]
        v = v_ref[0, :, hs]
        i_r = gr[h:h + 1, :]
        b_rh = b_r[M_HEADS + h:M_HEADS + h + 1, :]
        i_c = gc[:, h:h + 1]
        b_ch = b_c[:, M_HEADS + h:M_HEADS + h + 1]
        m_prev = m_ref[h:h + 1, 0:1]

        dmat = jnp.where(causal, b_ch - b_rh + i_r, NEG_BIG)
        inter = b_ch + m_prev
        mt = jnp.maximum(inter, jnp.max(dmat, axis=-1, keepdims=True))
        w = jnp.exp(dmat - mt)
        sc = jnp.exp(inter - mt)
        s = lax.dot_general(q, k, _NT, preferred_element_type=F32)
        qk = (s * (w * scale)).astype(BF16)

        vaug = jnp.concatenate([v, ones_col], axis=-1)
        caug = c_ref[h]
        num_aug = sc * _dot(q, caug.astype(BF16)) + _dot(qk, vaug)
        num = num_aug[:, :dh]
        den = num_aug[:, dh:dh + 1]
        hh = num / jnp.maximum(jnp.abs(den), jnp.exp(-mt))

        mu = jnp.mean(hh, axis=-1, keepdims=True)
        cen = hh - mu
        var = jnp.mean(cen * cen, axis=-1, keepdims=True)
        hn = cen * lax.rsqrt(var + NORM_EPS) * hn_ref[:, hs]
        gate = jax.nn.sigmoid(og_ref[0, :, hs].astype(F32))
        out_ref[0, :, hs] = (hn * gate).astype(out_ref.dtype)

        g_end = b_ch[L - 1:L, :]
        ws = g_end - b_ch + i_c
        m_new = jnp.maximum(g_end + m_prev, jnp.max(ws, axis=0, keepdims=True))
        a = jnp.exp(g_end + m_prev - m_new)
        wk = (jnp.exp(ws - m_new) * scale) * k.astype(F32)
        upd = _dot(wk.T.astype(BF16), vaug)
        c_ref[h] = a * caug + upd
        m_ref[h:h + 1, :] = jnp.broadcast_to(m_new, (1, LANES))


def _mlstm(p3, gc3, gr, b_col, b_row, head_norm):
    bsz, s, _ = p3.shape
    L = MLSTM_CHUNK
    w = M_HEADS * M_HEAD_DIM
    nc = s // L
    blk = lambda j: pl.BlockSpec((1, L, w), lambda b, c, j=j: (b, c, j))
    return pl.pallas_call(
        _mlstm_kernel,
        out_shape=jax.ShapeDtypeStruct((bsz, s, w), BF16),
        grid=(bsz, nc),
        in_specs=[blk(0), blk(1), blk(2), blk(3),
                  pl.BlockSpec((1, L, LANES), lambda b, c: (b, c, 0)),
                  pl.BlockSpec((SUBLANES, L), lambda b, c: (0, b * nc + c)),
                  pl.BlockSpec((1, LANES), lambda b, c: (0, 0)),
                  pl.BlockSpec((SUBLANES, 1), lambda b, c: (0, 0)),
                  pl.BlockSpec((1, w), lambda b, c: (0, 0))],
        out_specs=pl.BlockSpec((1, L, w), lambda b, c: (b, c, 0)),
        scratch_shapes=[pltpu.VMEM((M_HEADS, M_HEAD_DIM, 2 * M_HEAD_DIM), F32),
                        pltpu.VMEM((SUBLANES, LANES), F32)],
        compiler_params=_cparams("arbitrary", "arbitrary"),
        name="mlstm",
    )(p3, p3, p3, p3, gc3, gr, b_col, b_row, head_norm)


def _attn_kernel(lam_ref, q_ref, k_ref, v_ref, bias_ref, sub_ref, out_ref, m_ref, l_ref, acc_ref,
                 *, out_scale):
    qi = pl.program_id(2)
    tq = q_ref.shape[1]
    tk = tq
    d = DA_HEAD_DIM

    q = q_ref[0]
    lane = lax.broadcasted_iota(jnp.int32, q.shape, 1)
    qs = q * (d ** -0.5)
    zero = jnp.zeros_like(qs)
    qm = (jnp.where(lane < d, qs, zero), jnp.where(lane >= d, qs, zero))

    m_ref[...] = jnp.full_like(m_ref, NEG_BIG)
    l_ref[...] = jnp.zeros_like(l_ref)
    acc_ref[...] = jnp.zeros_like(acc_ref)

    def block(ki, bias):
        start = pl.multiple_of(ki * tk, tk)
        kb = k_ref[0, pl.ds(start, tk), :]
        vb = v_ref[0, pl.ds(start, tk), :]
        for j in range(2):
            s = lax.dot_general(qm[j], kb, _NT, preferred_element_type=F32)
            if bias is not None:
                s = s + bias
            m_old = m_ref[j]
            m_new = jnp.maximum(m_old, jnp.max(s, axis=-1, keepdims=True))
            p = jnp.exp(s - m_new)
            alpha = jnp.exp(m_old - m_new)
            l_ref[j] = alpha * l_ref[j] + jnp.sum(p, axis=-1, keepdims=True)
            acc_ref[j] = alpha * acc_ref[j] + _dot(p.astype(BF16), vb)
            m_ref[j] = m_new

    def far_body(ki, carry):
        block(ki, None)
        return carry

    lax.fori_loop(0, jnp.maximum(qi - 1, 0), far_body, 0)

    @pl.when(qi >= 1)
    def _():
        block(qi - 1, bias_ref[0, 1])

    block(qi, bias_ref[0, 0])

    lam = lam_ref[0]
    o = acc_ref[0] / l_ref[0] - lam * (acc_ref[1] / l_ref[1])
    ms = jnp.mean(o * o, axis=-1, keepdims=True)
    o = o * lax.rsqrt(ms + NORM_EPS) * sub_ref[...] * out_scale
    out_ref[0] = o.astype(out_ref.dtype)


def _attention(lam, p3, bias_tiles, subln, out_scale):
    bsz, s, _ = p3.shape
    blk = ATT_BLOCK
    nq = s // blk
    w = DA_HEADS * DA_V_DIM
    q_blk = 4 * M_HEADS
    k_blk = q_blk + DA_HEADS
    v_blk = k_blk + DA_HEADS
    return pl.pallas_call(
        functools.partial(_attn_kernel, out_scale=out_scale),
        out_shape=jax.ShapeDtypeStruct((bsz, s, w), BF16),
        grid=(bsz, DA_HEADS, nq),
        in_specs=[pl.BlockSpec(memory_space=pltpu.SMEM),
                  pl.BlockSpec((1, blk, LANES), lambda b, h, i: (b, i, q_blk + h)),
                  pl.BlockSpec((1, s, LANES), lambda b, h, i: (b, 0, k_blk + h)),
                  pl.BlockSpec((1, s, LANES), lambda b, h, i: (b, 0, v_blk + h)),
                  pl.BlockSpec((1, 2, blk, blk), lambda b, h, i: (h, 0, 0, 0)),
                  pl.BlockSpec((1, DA_V_DIM), lambda b, h, i: (0, 0))],
        out_specs=pl.BlockSpec((1, blk, DA_V_DIM), lambda b, h, i: (b, i, h)),
        scratch_shapes=[pltpu.VMEM((2, blk, 1), F32),
                        pltpu.VMEM((2, blk, 1), F32),
                        pltpu.VMEM((2, blk, DA_V_DIM), F32)],
        compiler_params=_cparams("arbitrary", "arbitrary", "arbitrary"),
        name="diff_attention",
    )(lam, p3, p3, p3, bias_tiles, subln)


def _router_epilogue(xn, g_ref, wr_ref, br_ref, triu_ref, h2_ref, meta_ref, cnt_ref):
    tm = xn.shape[0]
    h2 = _rms(xn, g_ref[...]).astype(BF16)
    h2_ref[...] = h2

    lg = lax.dot_general(wr_ref[...], h2, _NT, preferred_element_type=F32) + br_ref[...]
    gl = [lg[g:g + 1, :] for g in range(MOE_GROUPS)]
    gmax = functools.reduce(jnp.maximum, gl)
    gsel = jnp.where(gl[0] == gmax, 0.0, jnp.where(gl[1] == gmax, 1.0, jnp.where(gl[2] == gmax, 2.0, 3.0)))
    pg = 1.0 / functools.reduce(lambda a, b: a + b, [jnp.exp(x - gmax) for x in gl])

    ev = []
    for j in range(EXPERTS_PER_GROUP):
        acc = jnp.zeros_like(gmax)
        for g in range(MOE_GROUPS):
            r0 = SUBLANES + g * EXPERTS_PER_GROUP + j
            acc = jnp.where(gsel == float(g), lg[r0:r0 + 1, :], acc)
        ev.append(acc)
    v1 = functools.reduce(jnp.maximum, ev)
    i1 = jnp.where(ev[0] == v1, 0.0, jnp.where(ev[1] == v1, 1.0, jnp.where(ev[2] == v1, 2.0, 3.0)))
    ev2 = [jnp.where(i1 == float(j), -jnp.inf, ev[j]) for j in range(EXPERTS_PER_GROUP)]
    v2 = functools.reduce(jnp.maximum, ev2)
    hit = [(ev2[j] == v2) & (i1 != float(j)) for j in range(EXPERTS_PER_GROUP)]
    i2 = jnp.where(hit[0], 0.0, jnp.where(hit[1], 1.0, jnp.where(hit[2], 2.0, 3.0)))
    e21 = jnp.exp(v2 - v1)
    w1 = 1.0 / (1.0 + e21)
    w2 = e21 * w1
    first_low = i1 < i2
    lo = jnp.minimum(i1, i2)
    hi = jnp.maximum(i1, i2)
    c_lo = pg * jnp.where(first_low, w1, w2)
    c_hi = pg * jnp.where(first_low, w2, w1)
    pair = lo * (7.0 - lo) * 0.5 + hi - lo - 1.0
    bucket = gsel * float(N_PAIRS) + pair

    @pl.when(pl.program_id(0) == 0)
    def _():
        cnt_ref[...] = jnp.zeros_like(cnt_ref)

    rows = lax.broadcasted_iota(jnp.int32, (32, tm), 0).astype(F32)
    onehot = rows == bucket
    cum = _dot(jnp.where(onehot, 1.0, 0.0).astype(BF16), triu_ref[...])
    carry = cnt_ref[:, 0:1]
    rank = jnp.sum(jnp.where(onehot, cum + carry, 0.0), axis=0, keepdims=True) - 1.0
    cnt_ref[...] = jnp.broadcast_to(carry + cum[:, tm - 1:tm], cnt_ref.shape)

    rid = lax.broadcasted_iota(jnp.int32, (LANES, tm), 0)
    rec = jnp.where(rid == 0, bucket, jnp.where(rid == 1, c_lo, jnp.where(rid == 2, c_hi,
                    jnp.where(rid == 3, rank, 0.0))))
    meta_ref[...] = rec.T


_EPI_OUT_SHAPES = lambda t, d: (jax.ShapeDtypeStruct((t, d), F32),
                                jax.ShapeDtypeStruct((t, d), BF16),
                                jax.ShapeDtypeStruct((t, LANES), F32),
                                jax.ShapeDtypeStruct((32, LANES), F32))


def _epi_out_specs(tm, d):
    return (pl.BlockSpec((tm, d), lambda i: (i, 0)),
            pl.BlockSpec((tm, d), lambda i: (i, 0)),
            pl.BlockSpec((tm, LANES), lambda i: (i, 0)),
            pl.BlockSpec((32, LANES), lambda i: (0, 0)))


def _epi_in_specs(tm, d):
    return [pl.BlockSpec((1, d), lambda i: (0, 0)),
            pl.BlockSpec((32, d), lambda i: (0, 0)),
            pl.BlockSpec((32, 1), lambda i: (0, 0)),
            pl.BlockSpec((tm, tm), lambda i: (0, 0))]


def _outproj_even_kernel(hm_ref, ha_ref, x_ref, wo_ref, g_ref, wr_ref, br_ref, triu_ref,
                         xn_ref, h2_ref, meta_ref, cnt_ref):
    half = hm_ref.shape[1]
    y = _dot(hm_ref[...], wo_ref[:half, :]) + _dot(ha_ref[...], wo_ref[half:, :])
    xn = x_ref[...] + y
    xn_ref[...] = xn
    _router_epilogue(xn, g_ref, wr_ref, br_ref, triu_ref, h2_ref, meta_ref, cnt_ref)


def _outproj_even(hm, ha, x2d, w_out, g, wr, br, triu):
    t, d = x2d.shape
    tm = ROW_TILE
    half = hm.shape[1]
    return pl.pallas_call(
        _outproj_even_kernel,
        out_shape=_EPI_OUT_SHAPES(t, d),
        grid=(t // tm,),
        in_specs=[pl.BlockSpec((tm, half), lambda i: (i, 0)),
                  pl.BlockSpec((tm, half), lambda i: (i, 0)),
                  pl.BlockSpec((tm, d), lambda i: (i, 0)),
                  pl.BlockSpec((2 * half, d), lambda i: (0, 0))] + _epi_in_specs(tm, d),
        out_specs=_epi_out_specs(tm, d),
        compiler_params=_cparams("arbitrary"),
        name="outproj_even",
    )(hm, ha, x2d, w_out, g, wr, br, triu)


def _conv_mixer_kernel(x_ref, y_ref, gm_ref, wi_ref, cw_ref, wo_ref, g_ref, wr_ref, br_ref, triu_ref,
                       xn_ref, h2_ref, meta_ref, cnt_ref, tail_ref, *, tiles_per_seq):
    tm, d = x_ref.shape
    x = x_ref[...] + y_ref[...].astype(F32)
    h = _rms(x, gm_ref[...]).astype(BF16)

    @pl.when(pl.program_id(0) % tiles_per_seq == 0)
    def _():
        tail_ref[...] = jnp.zeros_like(tail_ref)

    row = lax.broadcasted_iota(jnp.int32, (tm, 512), 0)
    acc = jnp.zeros((tm, d), F32)
    for j in range(d // 512):
        sl = slice(j * 512, (j + 1) * 512)
        bg = _dot(h, wi_ref[:, sl])
        cg = _dot(h, wi_ref[:, d + j * 512:d + (j + 1) * 512])
        u = _dot(h, wi_ref[:, 2 * d + j * 512:2 * d + (j + 1) * 512])
        z = cg * u
        prev1 = tail_ref[SUBLANES - 1:SUBLANES, sl]
        prev2 = tail_ref[SUBLANES - 2:SUBLANES - 1, sl]
        z1 = jnp.where(row == 0, prev1, pltpu.roll(z, 1, axis=0))
        z2 = jnp.where(row == 0, prev2, jnp.where(row == 1, prev1, pltpu.roll(z, 2, axis=0)))
        zc = cw_ref[0:1, sl] * z2 + cw_ref[1:2, sl] * z1 + cw_ref[2:3, sl] * z
        tail_ref[:, sl] = z[tm - SUBLANES:, :]
        acc = acc + _dot((bg * zc).astype(BF16), wo_ref[sl, :])
    xn = x + acc
    xn_ref[...] = xn
    _router_epilogue(xn, g_ref, wr_ref, br_ref, triu_ref, h2_ref, meta_ref, cnt_ref)


def _conv_mixer(x2d, ymoe, gm, w_in, conv_w, w_out, g, wr, br, triu, seq):
    t, d = x2d.shape
    tm = ROW_TILE
    return pl.pallas_call(
        functools.partial(_conv_mixer_kernel, tiles_per_seq=seq // tm),
        out_shape=_EPI_OUT_SHAPES(t, d),
        grid=(t // tm,),
        in_specs=[pl.BlockSpec((tm, d), lambda i: (i, 0)),
                  pl.BlockSpec((tm, d), lambda i: (i, 0)),
                  pl.BlockSpec((1, d), lambda i: (0, 0)),
                  pl.BlockSpec((d, 3 * d), lambda i: (0, 0)),
                  pl.BlockSpec((SUBLANES, d), lambda i: (0, 0)),
                  pl.BlockSpec((d, d), lambda i: (0, 0))] + _epi_in_specs(tm, d),
        out_specs=_epi_out_specs(tm, d),
        scratch_shapes=[pltpu.VMEM((SUBLANES, d), F32)],
        compiler_params=_cparams("arbitrary"),
        name="conv_mixer",
    )(x2d, ymoe, gm, w_in, conv_w, w_out, g, wr, br, triu)


def _moe_kernel(elo_ref, ehi_ref, valid_ref, x_ref, meta_ref,
                wg_lo, wu_lo, wd_lo, wg_hi, wu_hi, wd_hi, out_ref):
    @pl.when(valid_ref[pl.program_id(0)] != 0)
    def _():
        x = x_ref[...]

        def ffn(wg, wu, wd):
            a = _dot(x, wg[0])
            a = (a * jax.nn.sigmoid(a)) * _dot(x, wu[0])
            return _dot(a.astype(BF16), wd[0])

        c_lo = meta_ref[:, 1:2]
        c_hi = meta_ref[:, 2:3]
        out = c_lo * ffn(wg_lo, wu_lo, wd_lo) + c_hi * ffn(wg_hi, wu_hi, wd_hi)
        out_ref[...] = out.astype(out_ref.dtype)


def _moe(tile_elo, tile_ehi, tile_valid, xs, ms, w_gate, w_up, w_down):
    tp, d = xs.shape
    ff = w_gate.shape[-1]
    tm = MOE_TILE
    lo = lambda i, elo, ehi, valid: (elo[i], 0, 0)
    hi = lambda i, elo, ehi, valid: (ehi[i], 0, 0)
    rows = lambda i, elo, ehi, valid: (i, 0)
    return pl.pallas_call(
        _moe_kernel,
        out_shape=jax.ShapeDtypeStruct((tp, d), BF16),
        grid_spec=pltpu.PrefetchScalarGridSpec(
            num_scalar_prefetch=3,
            grid=(tp // tm,),
            in_specs=[pl.BlockSpec((tm, d), rows),
                      pl.BlockSpec((tm, LANES), rows),
                      pl.BlockSpec((1, d, ff), lo), pl.BlockSpec((1, d, ff), lo), pl.BlockSpec((1, ff, d), lo),
                      pl.BlockSpec((1, d, ff), hi), pl.BlockSpec((1, d, ff), hi), pl.BlockSpec((1, ff, d), hi)],
            out_specs=pl.BlockSpec((tm, d), rows)),
        compiler_params=_cparams("arbitrary"),
        name="grouped_moe",
    )(tile_elo, tile_ehi, tile_valid, xs, ms, w_gate, w_up, w_down, w_gate, w_up, w_down)


def _final_kernel(x_ref, y_ref, g_ref, out_ref):
    out_ref[...] = _rms(x_ref[...] + y_ref[...].astype(F32), g_ref[...])


def _final(x2d, ymoe, g):
    t, d = x2d.shape
    tm = ROW_TILE
    return pl.pallas_call(
        _final_kernel,
        out_shape=jax.ShapeDtypeStruct((t, d), F32),
        grid=(t // tm,),
        in_specs=[pl.BlockSpec((tm, d), lambda i: (i, 0)),
                  pl.BlockSpec((tm, d), lambda i: (i, 0)),
                  pl.BlockSpec((1, d), lambda i: (0, 0))],
        out_specs=pl.BlockSpec((tm, d), lambda i: (i, 0)),
        compiler_params=_cparams("arbitrary"),
        name="final_norm",
    )(x2d, ymoe, g)


_PAIRS = [(a, b) for a in range(EXPERTS_PER_GROUP) for b in range(a + 1, EXPERTS_PER_GROUP)]
_BUCKET_ELO = np.array([g * EXPERTS_PER_GROUP + _PAIRS[p][0] for g in range(MOE_GROUPS) for p in range(N_PAIRS)], np.int32)
_BUCKET_EHI = np.array([g * EXPERTS_PER_GROUP + _PAIRS[p][1] for g in range(MOE_GROUPS) for p in range(N_PAIRS)], np.int32)


def _sorted_layout(meta, cnt, t):
    tm = MOE_TILE
    n_tiles = t // tm + N_BUCKETS
    bucket = meta[:, 0].astype(jnp.int32)
    rank = meta[:, 3].astype(jnp.int32)
    counts = cnt[:N_BUCKETS, 0].astype(jnp.int32)
    padded = ((counts + tm - 1) // tm) * tm
    ends = jnp.cumsum(padded)
    starts = ends - padded
    dest = starts[bucket] + rank
    tile_start = jnp.arange(n_tiles, dtype=jnp.int32) * tm
    tile_valid = (tile_start < ends[-1]).astype(jnp.int32)
    last_valid = jnp.maximum(ends[-1] // tm - 1, 0)
    tile_bucket = jnp.searchsorted(ends, jnp.minimum(tile_start, last_valid * tm), side="right").astype(jnp.int32)
    tile_bucket = jnp.minimum(tile_bucket, N_BUCKETS - 1)
    return dest, jnp.asarray(_BUCKET_ELO)[tile_bucket], jnp.asarray(_BUCKET_EHI)[tile_bucket], tile_valid, n_tiles * tm


def _moe_layer(h2, meta, cnt, w_gate, w_up, w_down):
    t = h2.shape[0]
    dest, tile_elo, tile_ehi, tile_valid, tp = _sorted_layout(meta, cnt, t)
    src = jnp.zeros((tp,), jnp.int32).at[dest].set(jnp.arange(t, dtype=jnp.int32))
    xs = jnp.take(h2, src, axis=0)
    ms = jnp.take(meta, src, axis=0)
    ys = _moe(tile_elo, tile_ehi, tile_valid, xs, ms, w_gate, w_up, w_down)
    return jnp.take(ys, dest, axis=0)


def _router_weights(wg, bg, we, be):
    d = wg.shape[0]
    wr = jnp.zeros((32, d), F32).at[:MOE_GROUPS].set(wg.T).at[SUBLANES:SUBLANES + N_EXPERTS].set(we.T)
    br = jnp.zeros((32, 1), F32).at[:MOE_GROUPS, 0].set(bg).at[SUBLANES:SUBLANES + N_EXPERTS, 0].set(be)
    return wr.astype(BF16), br


def kernel(x, rel_bias, norm_mix, norm_ffn, norm_final, w_in_even, w_out_even, m_igate_b, m_fgate_b,
           m_head_norm, lam_q1, lam_k1, lam_q2, lam_k2, da_subln, w_in_odd, conv_w, w_out_odd,
           router_group_w, router_group_b, router_expert_w, router_expert_b,
           exp_w_gate, exp_w_up, exp_w_down):
    bsz, seq, d = x.shape
    t = bsz * seq
    x2d = x.reshape(t, d)
    mw = M_HEADS * M_HEAD_DIM

    w_in = w_in_even[0]
    gate0 = 4 * mw
    w_main = jnp.concatenate([w_in[:, :gate0], w_in[:, gate0 + 2 * M_HEADS:]], axis=1).astype(BF16)
    w_gates = w_in[:, gate0:gate0 + 2 * M_HEADS]
    wg_col = jnp.zeros((d, LANES), F32).at[:, :2 * M_HEADS].set(w_gates).astype(BF16)
    wg_row = w_gates.T.astype(BF16)
    gate_b = jnp.concatenate([m_igate_b[0], m_fgate_b[0]]).astype(F32)
    b_col = jnp.zeros((1, LANES), F32).at[0, :2 * M_HEADS].set(gate_b)
    b_row = gate_b.reshape(2 * M_HEADS, 1)
    triu = jnp.triu(jnp.ones((ROW_TILE, ROW_TILE), F32)).astype(BF16)

    p, gc, gr = _inproj_even(x2d, norm_mix[0].reshape(1, d), w_main, wg_col, wg_row)
    p3 = p.reshape(bsz, seq, -1)
    hm = _mlstm(p3, gc.reshape(bsz, seq, LANES), gr, b_col, b_row, m_head_norm[0].reshape(1, mw))

    lam_init = 0.8 - 0.6 * math.exp(-0.3 * 0)
    lam = (jnp.exp(jnp.sum(lam_q1[0].astype(F32) * lam_k1[0].astype(F32)))
           - jnp.exp(jnp.sum(lam_q2[0].astype(F32) * lam_k2[0].astype(F32))) + lam_init).reshape(1)
    ha = _attention(lam, p3, _bias_tiles(rel_bias.astype(F32), ATT_BLOCK),
                    da_subln[0].reshape(1, DA_V_DIM).astype(F32), 1.0 - lam_init)

    wr0, br0 = _router_weights(router_group_w[0], router_group_b[0], router_expert_w[0], router_expert_b[0])
    x1, h2, meta, cnt = _outproj_even(hm.reshape(t, mw), ha.reshape(t, -1), x2d, w_out_even[0].astype(BF16),
                                      norm_ffn[0].reshape(1, d), wr0, br0, triu)
    y = _moe_layer(h2, meta, cnt, exp_w_gate[0].astype(BF16), exp_w_up[0].astype(BF16),
                   exp_w_down[0].astype(BF16))

    wr1, br1 = _router_weights(router_group_w[1], router_group_b[1], router_expert_w[1], router_expert_b[1])
    cw = jnp.zeros((SUBLANES, d), F32).at[:conv_w.shape[1]].set(conv_w[0])
    x3, h2, meta, cnt = _conv_mixer(x1, y, norm_mix[1].reshape(1, d), w_in_odd[0].astype(BF16), cw,
                                    w_out_odd[0].astype(BF16), norm_ffn[1].reshape(1, d), wr1, br1, triu, seq)
    y = _moe_layer(h2, meta, cnt, exp_w_gate[1].astype(BF16), exp_w_up[1].astype(BF16),
                   exp_w_down[1].astype(BF16))

    return _final(x3, y, norm_final.reshape(1, d)).reshape(bsz, seq, d)
```

```python
import functools
import math

import jax
import jax.numpy as jnp
import numpy as np
from jax import lax
from jax.experimental import pallas as pl
from jax.experimental.pallas import tpu as pltpu

F32 = jnp.float32
BF16 = jnp.bfloat16

NORM_EPS = 1e-6
NEG_BIG = -1e30

M_HEADS = 4
M_HEAD_DIM = 128
DA_HEADS = 4
DA_HEAD_DIM = 64
DA_V_DIM = 128
REL_BUCKETS = 32
REL_MAX_DIST = 128
MOE_GROUPS = 4
EXPERTS_PER_GROUP = 4
N_EXPERTS = 16
N_PAIRS = 6
N_BUCKETS = MOE_GROUPS * N_PAIRS

LANES = 128
SUBLANES = 8
VMEM_LIMIT = 56 * 1024 * 1024

ROW_TILE = 512
MLSTM_CHUNK = 256
ATT_BLOCK = 256
MOE_TILE = 512

_NT = (((1,), (1,)), ((), ()))


def _cparams(*sem):
    return pltpu.CompilerParams(dimension_semantics=sem, vmem_limit_bytes=VMEM_LIMIT)


def _rms(x, g):
    ms = jnp.mean(x * x, axis=-1, keepdims=True)
    return x * lax.rsqrt(ms + NORM_EPS) * g


def _dot(a, b):
    return jnp.dot(a, b, preferred_element_type=F32)


def _split3(x):
    hi = x.astype(BF16)
    r1 = x - hi.astype(F32)
    mid = r1.astype(BF16)
    lo = (r1 - mid.astype(F32)).astype(BF16)
    return hi, mid, lo


def _log_sigmoid(x):
    return jnp.minimum(x, 0.0) - jnp.log(1.0 + jnp.exp(-jnp.abs(x)))


def _bias_tiles_kernel(rb_ref, out_ref):
    h = pl.program_id(0)
    blk = out_ref.shape[-1]
    r = lax.broadcasted_iota(jnp.int32, (blk, blk), 0)
    c = lax.broadcasted_iota(jnp.int32, (blk, blk), 1)
    far = rb_ref[REL_BUCKETS - 1, h]
    max_exact = REL_BUCKETS // 2
    for j in range(2):
        rel = c - r - j * blk
        n = jnp.maximum(-rel, 0)
        nf = jnp.maximum(n, 1).astype(F32)
        large = max_exact + (jnp.log(nf / max_exact) / math.log(REL_MAX_DIST / max_exact)
                             * (REL_BUCKETS - max_exact)).astype(jnp.int32)
        large = jnp.minimum(large, REL_BUCKETS - 1)
        bucket = jnp.where(n < max_exact, n, large)
        val = jnp.zeros((blk, blk), F32)
        for b in range(REL_BUCKETS):
            val = jnp.where(bucket == b, rb_ref[b, h], val)
        val = val - far
        if j == 0:
            val = jnp.where(rel <= 0, val, NEG_BIG)
        out_ref[0, j] = val


def _bias_tiles(rel_bias, blk):
    return pl.pallas_call(
        _bias_tiles_kernel,
        out_shape=jax.ShapeDtypeStruct((DA_HEADS, 2, blk, blk), F32),
        grid=(DA_HEADS,),
        in_specs=[pl.BlockSpec(memory_space=pltpu.SMEM)],
        out_specs=pl.BlockSpec((1, 2, blk, blk), lambda h: (h, 0, 0, 0)),
        compiler_params=_cparams("arbitrary"),
        name="bias_tiles",
    )(rel_bias)


def _inproj_even_kernel(x_ref, g_ref, w_ref, wgc_ref, wgr_ref, p_ref, gc_ref, gr_ref):
    h = _rms(x_ref[...], g_ref[...]).astype(BF16)
    n = w_ref.shape[1]
    for j in range(n // 512):
        sl = slice(j * 512, (j + 1) * 512)
        p_ref[:, sl] = _dot(h, w_ref[:, sl]).astype(BF16)
    gc_ref[...] = _dot(h, wgc_ref[...])
    gr_ref[...] = lax.dot_general(wgr_ref[...], h, _NT, preferred_element_type=F32)


def _inproj_even(x2d, g, w_main, wg_col, wg_row):
    t, d = x2d.shape
    n = w_main.shape[1]
    tm = ROW_TILE
    return pl.pallas_call(
        _inproj_even_kernel,
        out_shape=(jax.ShapeDtypeStruct((t, n), BF16),
                   jax.ShapeDtypeStruct((t, LANES), F32),
                   jax.ShapeDtypeStruct((SUBLANES, t), F32)),
        grid=(t // tm,),
        in_specs=[pl.BlockSpec((tm, d), lambda i: (i, 0)),
                  pl.BlockSpec((1, d), lambda i: (0, 0)),
                  pl.BlockSpec((d, n), lambda i: (0, 0)),
                  pl.BlockSpec((d, LANES), lambda i: (0, 0)),
                  pl.BlockSpec((SUBLANES, d), lambda i: (0, 0))],
        out_specs=(pl.BlockSpec((tm, n), lambda i: (i, 0)),
                   pl.BlockSpec((tm, LANES), lambda i: (i, 0)),
                   pl.BlockSpec((SUBLANES, tm), lambda i: (0, i))),
        compiler_params=_cparams("arbitrary"),
        name="inproj_even",
    )(x2d, g, w_main, wg_col, wg_row)


def _mlstm_kernel(q_ref, k_ref, v_ref, og_ref, gc_ref, gr_ref, bc_ref, br_ref, hn_ref, out_ref,
                  c_ref, m_ref):
    L = q_ref.shape[1]
    dh = M_HEAD_DIM
    scale = dh ** -0.5

    @pl.when(pl.program_id(1) == 0)
    def _():
        c_ref[...] = jnp.zeros_like(c_ref)
        m_ref[...] = jnp.zeros_like(m_ref)

    gc = gc_ref[0] + bc_ref[...]
    gr = gr_ref[...] + br_ref[...]
    lf_c = _log_sigmoid(gc)
    lf_r = _log_sigmoid(gr)

    row = lax.broadcasted_iota(jnp.int32, (L, L), 0)
    col = lax.broadcasted_iota(jnp.int32, (L, L), 1)
    causal = row >= col
    tril = jnp.where(causal, 1.0, 0.0).astype(BF16)
    triu = jnp.where(row <= col, 1.0, 0.0).astype(BF16)
    b_c = sum(_dot(tril, part) for part in _split3(lf_c))
    b_r = sum(_dot(part, triu) for part in _split3(lf_r))

    lane = lax.broadcasted_iota(jnp.int32, (L, LANES), 1)
    ones_col = jnp.where(lane == 0, 1.0, 0.0).astype(BF16)

    for h in range(M_HEADS):
        hs = slice(h * dh, (h + 1) * dh)
        q = q_ref[0, :, hs]
        k = k_ref[0, :, hs]
        v = v_ref[0, :, hs]
        i_r = gr[h:h + 1, :]
        b_rh = b_r[M_HEADS + h:M_HEADS + h + 1, :]
        i_c = gc[:, h:h + 1]
        b_ch = b_c[:, M_HEADS + h:M_HEADS + h + 1]
        m_prev = m_ref[h:h + 1, 0:1]

        dmat = jnp.where(causal, b_ch - b_rh + i_r, NEG_BIG)
        inter = b_ch + m_prev
        mt = jnp.maximum(inter, jnp.max(dmat, axis=-1, keepdims=True))
        w = jnp.exp(dmat - mt)
        sc = jnp.exp(inter - mt)
        s = lax.dot_general(q, k, _NT, preferred_element_type=F32)
        qk = (s * (w * scale)).astype(BF16)

        vaug = jnp.concatenate([v, ones_col], axis=-1)
        caug = c_ref[h]
        num_aug = sc * _dot(q, caug.astype(BF16)) + _dot(qk, vaug)
        num = num_aug[:, :dh]
        den = num_aug[:, dh:dh + 1]
        hh = num / jnp.maximum(jnp.abs(den), jnp.exp(-mt))

        mu = jnp.mean(hh, axis=-1, keepdims=True)
        cen = hh - mu
        var = jnp.mean(cen * cen, axis=-1, keepdims=True)
        hn = cen * lax.rsqrt(var + NORM_EPS) * hn_ref[:, hs]
        gate = jax.nn.sigmoid(og_ref[0, :, hs].astype(F32))
        out_ref[0, :, hs] = (hn * gate).astype(out_ref.dtype)

        g_end = b_ch[L - 1:L, :]
        ws = g_end - b_ch + i_c
        m_new = jnp.maximum(g_end + m_prev, jnp.max(ws, axis=0, keepdims=True))
        a = jnp.exp(g_end + m_prev - m_new)
        wk = (jnp.exp(ws - m_new) * scale) * k.astype(F32)
        upd = _dot(wk.T.astype(BF16), vaug)
        c_ref[h] = a * caug + upd
        m_ref[h:h + 1, :] = jnp.broadcast_to(m_new, (1, LANES))


def _mlstm(p3, gc3, gr, b_col, b_row, head_norm):
    bsz, s, _ = p3.shape
    L = MLSTM_CHUNK
    w = M_HEADS * M_HEAD_DIM
    nc = s // L
    blk = lambda j: pl.BlockSpec((1, L, w), lambda b, c, j=j: (b, c, j))
    return pl.pallas_call(
        _mlstm_kernel,
        out_shape=jax.ShapeDtypeStruct((bsz, s, w), BF16),
        grid=(bsz, nc),
        in_specs=[blk(0), blk(1), blk(2), blk(3),
                  pl.BlockSpec((1, L, LANES), lambda b, c: (b, c, 0)),
                  pl.BlockSpec((SUBLANES, L), lambda b, c: (0, b * nc + c)),
                  pl.BlockSpec((1, LANES), lambda b, c: (0, 0)),
                  pl.BlockSpec((SUBLANES, 1), lambda b, c: (0, 0)),
                  pl.BlockSpec((1, w), lambda b, c: (0, 0))],
        out_specs=pl.BlockSpec((1, L, w), lambda b, c: (b, c, 0)),
        scratch_shapes=[pltpu.VMEM((M_HEADS, M_HEAD_DIM, 2 * M_HEAD_DIM), F32),
                        pltpu.VMEM((SUBLANES, LANES), F32)],
        compiler_params=_cparams("arbitrary", "arbitrary"),
        name="mlstm",
    )(p3, p3, p3, p3, gc3, gr, b_col, b_row, head_norm)


def _attn_kernel(lam_ref, q_ref, k_ref, v_ref, bias_ref, sub_ref, out_ref, vaug_ref, *, out_scale, blk):
    s_len = q_ref.shape[1]
    d = DA_HEAD_DIM
    lane_s = lax.broadcasted_iota(jnp.int32, (s_len, DA_V_DIM), 1)
    vaug_ref[:, :DA_V_DIM] = v_ref[0]
    vaug_ref[:, DA_V_DIM:] = jnp.where(lane_s == 0, 1.0, 0.0).astype(BF16)
    lam = lam_ref[0]
    lane = lax.broadcasted_iota(jnp.int32, (blk, LANES), 1)
    for qi in range(s_len // blk):
        q0, q1 = qi * blk, (qi + 1) * blk
        qs = q_ref[0, q0:q1, :] * (d ** -0.5)
        zero = jnp.zeros_like(qs)
        heads = []
        for qm in (jnp.where(lane < d, qs, zero), jnp.where(lane >= d, qs, zero)):
            score = lambda a, b: lax.dot_general(qm, k_ref[0, a:b, :], _NT, preferred_element_type=F32)
            parts = [score(q0, q1) + bias_ref[0, 0]]
            if qi >= 1:
                parts.append(score(q0 - blk, q0) + bias_ref[0, 1])
            if qi >= 2:
                parts.append(score(0, q0 - blk))
            m = functools.reduce(jnp.maximum, [jnp.max(p, axis=-1, keepdims=True) for p in parts])
            pexp = [jnp.exp(p - m).astype(BF16) for p in parts]
            o = _dot(pexp[0], vaug_ref[q0:q1, :])
            if qi >= 1:
                o = o + _dot(pexp[1], vaug_ref[q0 - blk:q0, :])
            if qi >= 2:
                o = o + _dot(pexp[2], vaug_ref[0:q0 - blk, :])
            heads.append(o[:, :DA_V_DIM] / o[:, DA_V_DIM:DA_V_DIM + 1])
        o = heads[0] - lam * heads[1]
        ms = jnp.mean(o * o, axis=-1, keepdims=True)
        o = o * lax.rsqrt(ms + NORM_EPS) * sub_ref[...] * out_scale
        out_ref[0, q0:q1, :] = o.astype(out_ref.dtype)


def _attention(lam, p3, bias_tiles, subln, out_scale):
    bsz, s, _ = p3.shape
    blk = ATT_BLOCK
    w = DA_HEADS * DA_V_DIM
    q_blk = 4 * M_HEADS
    k_blk = q_blk + DA_HEADS
    v_blk = k_blk + DA_HEADS
    col = lambda base: pl.BlockSpec((1, s, LANES), lambda b, h: (b, 0, base + h))
    return pl.pallas_call(
        functools.partial(_attn_kernel, out_scale=out_scale, blk=blk),
        out_shape=jax.ShapeDtypeStruct((bsz, s, w), BF16),
        grid=(bsz, DA_HEADS),
        in_specs=[pl.BlockSpec(memory_space=pltpu.SMEM),
                  col(q_blk), col(k_blk), col(v_blk),
                  pl.BlockSpec((1, 2, blk, blk), lambda b, h: (h, 0, 0, 0)),
                  pl.BlockSpec((1, DA_V_DIM), lambda b, h: (0, 0))],
        out_specs=pl.BlockSpec((1, s, DA_V_DIM), lambda b, h: (b, 0, h)),
        scratch_shapes=[pltpu.VMEM((s, 2 * DA_V_DIM), BF16)],
        compiler_params=_cparams("arbitrary", "arbitrary"),
        name="diff_attention",
    )(lam, p3, p3, p3, bias_tiles, subln)


def _router_epilogue(xn, g_ref, wr_ref, br_ref, triu_ref, h2_ref, meta_ref, cnt_ref):
    tm = xn.shape[0]
    h2 = _rms(xn, g_ref[...]).astype(BF16)
    h2_ref[...] = h2

    lg = lax.dot_general(wr_ref[...], h2, _NT, preferred_element_type=F32) + br_ref[...]
    gl = [lg[g:g + 1, :] for g in range(MOE_GROUPS)]
    gmax = functools.reduce(jnp.maximum, gl)
    gsel = jnp.where(gl[0] == gmax, 0.0, jnp.where(gl[1] == gmax, 1.0, jnp.where(gl[2] == gmax, 2.0, 3.0)))
    pg = 1.0 / functools.reduce(lambda a, b: a + b, [jnp.exp(x - gmax) for x in gl])

    ev = []
    for j in range(EXPERTS_PER_GROUP):
        acc = jnp.zeros_like(gmax)
        for g in range(MOE_GROUPS):
            r0 = SUBLANES + g * EXPERTS_PER_GROUP + j
            acc = jnp.where(gsel == float(g), lg[r0:r0 + 1, :], acc)
        ev.append(acc)
    v1 = functools.reduce(jnp.maximum, ev)
    i1 = jnp.where(ev[0] == v1, 0.0, jnp.where(ev[1] == v1, 1.0, jnp.where(ev[2] == v1, 2.0, 3.0)))
    ev2 = [jnp.where(i1 == float(j), -jnp.inf, ev[j]) for j in range(EXPERTS_PER_GROUP)]
    v2 = functools.reduce(jnp.maximum, ev2)
    hit = [(ev2[j] == v2) & (i1 != float(j)) for j in range(EXPERTS_PER_GROUP)]
    i2 = jnp.where(hit[0], 0.0, jnp.where(hit[1], 1.0, jnp.where(hit[2], 2.0, 3.0)))
    e21 = jnp.exp(v2 - v1)
    w1 = 1.0 / (1.0 + e21)
    w2 = e21 * w1
    first_low = i1 < i2
    lo = jnp.minimum(i1, i2)
    hi = jnp.maximum(i1, i2)
    c_lo = pg * jnp.where(first_low, w1, w2)
    c_hi = pg * jnp.where(first_low, w2, w1)
    pair = lo * (7.0 - lo) * 0.5 + hi - lo - 1.0
    bucket = gsel * float(N_PAIRS) + pair

    @pl.when(pl.program_id(0) == 0)
    def _():
        cnt_ref[...] = jnp.zeros_like(cnt_ref)

    rows = lax.broadcasted_iota(jnp.int32, (32, tm), 0).astype(F32)
    onehot = rows == bucket
    cum = _dot(jnp.where(onehot, 1.0, 0.0).astype(BF16), triu_ref[...])
    carry = cnt_ref[:, 0:1]
    rank = jnp.sum(jnp.where(onehot, cum + carry, 0.0), axis=0, keepdims=True) - 1.0
    cnt_ref[...] = jnp.broadcast_to(carry + cum[:, tm - 1:tm], cnt_ref.shape)

    rid = lax.broadcasted_iota(jnp.int32, (LANES, tm), 0)
    rec = jnp.where(rid == 0, bucket, jnp.where(rid == 1, c_lo, jnp.where(rid == 2, c_hi,
                    jnp.where(rid == 3, rank, 0.0))))
    meta_ref[...] = rec.T


_EPI_OUT_SHAPES = lambda t, d: (jax.ShapeDtypeStruct((t, d), F32),
                                jax.ShapeDtypeStruct((t, d), BF16),
                                jax.ShapeDtypeStruct((t, LANES), F32),
                                jax.ShapeDtypeStruct((32, LANES), F32))


def _epi_out_specs(tm, d):
    return (pl.BlockSpec((tm, d), lambda i: (i, 0)),
            pl.BlockSpec((tm, d), lambda i: (i, 0)),
            pl.BlockSpec((tm, LANES), lambda i: (i, 0)),
            pl.BlockSpec((32, LANES), lambda i: (0, 0)))


def _epi_in_specs(tm, d):
    return [pl.BlockSpec((1, d), lambda i: (0, 0)),
            pl.BlockSpec((32, d), lambda i: (0, 0)),
            pl.BlockSpec((32, 1), lambda i: (0, 0)),
            pl.BlockSpec((tm, tm), lambda i: (0, 0))]


def _outproj_even_kernel(hm_ref, ha_ref, x_ref, wo_ref, g_ref, wr_ref, br_ref, triu_ref,
                         xn_ref, h2_ref, meta_ref, cnt_ref):
    half = hm_ref.shape[1]
    y = _dot(hm_ref[...], wo_ref[:half, :]) + _dot(ha_ref[...], wo_ref[half:, :])
    xn = x_ref[...] + y
    xn_ref[...] = xn
    _router_epilogue(xn, g_ref, wr_ref, br_ref, triu_ref, h2_ref, meta_ref, cnt_ref)


def _outproj_even(hm, ha, x2d, w_out, g, wr, br, triu):
    t, d = x2d.shape
    tm = ROW_TILE
    half = hm.shape[1]
    return pl.pallas_call(
        _outproj_even_kernel,
        out_shape=_EPI_OUT_SHAPES(t, d),
        grid=(t // tm,),
        in_specs=[pl.BlockSpec((tm, half), lambda i: (i, 0)),
                  pl.BlockSpec((tm, half), lambda i: (i, 0)),
                  pl.BlockSpec((tm, d), lambda i: (i, 0)),
                  pl.BlockSpec((2 * half, d), lambda i: (0, 0))] + _epi_in_specs(tm, d),
        out_specs=_epi_out_specs(tm, d),
        compiler_params=_cparams("arbitrary"),
        name="outproj_even",
    )(hm, ha, x2d, w_out, g, wr, br, triu)


def _conv_mixer_kernel(x_ref, y_ref, gm_ref, wi_ref, cw_ref, wo_ref, g_ref, wr_ref, br_ref, triu_ref,
                       xn_ref, h2_ref, meta_ref, cnt_ref, tail_ref, *, tiles_per_seq):
    tm, d = x_ref.shape
    x = x_ref[...] + y_ref[...].astype(F32)
    h = _rms(x, gm_ref[...]).astype(BF16)

    @pl.when(pl.program_id(0) % tiles_per_seq == 0)
    def _():
        tail_ref[...] = jnp.zeros_like(tail_ref)

    row = lax.broadcasted_iota(jnp.int32, (tm, 512), 0)
    acc = jnp.zeros((tm, d), F32)
    for j in range(d // 512):
        sl = slice(j * 512, (j + 1) * 512)
        bg = _dot(h, wi_ref[:, sl])
        cg = _dot(h, wi_ref[:, d + j * 512:d + (j + 1) * 512])
        u = _dot(h, wi_ref[:, 2 * d + j * 512:2 * d + (j + 1) * 512])
        z = cg * u
        prev1 = tail_ref[SUBLANES - 1:SUBLANES, sl]
        prev2 = tail_ref[SUBLANES - 2:SUBLANES - 1, sl]
        z1 = jnp.where(row == 0, prev1, pltpu.roll(z, 1, axis=0))
        z2 = jnp.where(row == 0, prev2, jnp.where(row == 1, prev1, pltpu.roll(z, 2, axis=0)))
        zc = cw_ref[0:1, sl] * z2 + cw_ref[1:2, sl] * z1 + cw_ref[2:3, sl] * z
        tail_ref[:, sl] = z[tm - SUBLANES:, :]
        acc = acc + _dot((bg * zc).astype(BF16), wo_ref[sl, :])
    xn = x + acc
    xn_ref[...] = xn
    _router_epilogue(xn, g_ref, wr_ref, br_ref, triu_ref, h2_ref, meta_ref, cnt_ref)


def _conv_mixer(x2d, ymoe, gm, w_in, conv_w, w_out, g, wr, br, triu, seq):
    t, d = x2d.shape
    tm = ROW_TILE
    return pl.pallas_call(
        functools.partial(_conv_mixer_kernel, tiles_per_seq=seq // tm),
        out_shape=_EPI_OUT_SHAPES(t, d),
        grid=(t // tm,),
        in_specs=[pl.BlockSpec((tm, d), lambda i: (i, 0)),
                  pl.BlockSpec((tm, d), lambda i: (i, 0)),
                  pl.BlockSpec((1, d), lambda i: (0, 0)),
                  pl.BlockSpec((d, 3 * d), lambda i: (0, 0)),
                  pl.BlockSpec((SUBLANES, d), lambda i: (0, 0)),
                  pl.BlockSpec((d, d), lambda i: (0, 0))] + _epi_in_specs(tm, d),
        out_specs=_epi_out_specs(tm, d),
        scratch_shapes=[pltpu.VMEM((SUBLANES, d), F32)],
        compiler_params=_cparams("arbitrary"),
        name="conv_mixer",
    )(x2d, ymoe, gm, w_in, conv_w, w_out, g, wr, br, triu)


def _moe_kernel(elo_ref, ehi_ref, valid_ref, x_ref, meta_ref,
                wg_lo, wu_lo, wd_lo, wg_hi, wu_hi, wd_hi, out_ref):
    @pl.when(valid_ref[pl.program_id(0)] != 0)
    def _():
        x = x_ref[...]

        def ffn(wg, wu, wd):
            a = _dot(x, wg[0])
            a = (a * jax.nn.sigmoid(a)) * _dot(x, wu[0])
            return _dot(a.astype(BF16), wd[0])

        c_lo = meta_ref[:, 1:2]
        c_hi = meta_ref[:, 2:3]
        out = c_lo * ffn(wg_lo, wu_lo, wd_lo) + c_hi * ffn(wg_hi, wu_hi, wd_hi)
        out_ref[...] = out.astype(out_ref.dtype)


def _moe(tile_elo, tile_ehi, tile_valid, xs, ms, w_gate, w_up, w_down):
    tp, d = xs.shape
    ff = w_gate.shape[-1]
    tm = MOE_TILE
    lo = lambda i, elo, ehi, valid: (elo[i], 0, 0)
    hi = lambda i, elo, ehi, valid: (ehi[i], 0, 0)
    rows = lambda i, elo, ehi, valid: (i, 0)
    return pl.pallas_call(
        _moe_kernel,
        out_shape=jax.ShapeDtypeStruct((tp, d), BF16),
        grid_spec=pltpu.PrefetchScalarGridSpec(
            num_scalar_prefetch=3,
            grid=(tp // tm,),
            in_specs=[pl.BlockSpec((tm, d), rows),
                      pl.BlockSpec((tm, LANES), rows),
                      pl.BlockSpec((1, d, ff), lo), pl.BlockSpec((1, d, ff), lo), pl.BlockSpec((1, ff, d), lo),
                      pl.BlockSpec((1, d, ff), hi), pl.BlockSpec((1, d, ff), hi), pl.BlockSpec((1, ff, d), hi)],
            out_specs=pl.BlockSpec((tm, d), rows)),
        compiler_params=_cparams("arbitrary"),
        name="grouped_moe",
    )(tile_elo, tile_ehi, tile_valid, xs, ms, w_gate, w_up, w_down, w_gate, w_up, w_down)


def _final_kernel(x_ref, y_ref, g_ref, out_ref):
    out_ref[...] = _rms(x_ref[...] + y_ref[...].astype(F32), g_ref[...])


def _final(x2d, ymoe, g):
    t, d = x2d.shape
    tm = ROW_TILE
    return pl.pallas_call(
        _final_kernel,
        out_shape=jax.ShapeDtypeStruct((t, d), F32),
        grid=(t // tm,),
        in_specs=[pl.BlockSpec((tm, d), lambda i: (i, 0)),
                  pl.BlockSpec((tm, d), lambda i: (i, 0)),
                  pl.BlockSpec((1, d), lambda i: (0, 0))],
        out_specs=pl.BlockSpec((tm, d), lambda i: (i, 0)),
        compiler_params=_cparams("arbitrary"),
        name="final_norm",
    )(x2d, ymoe, g)


_PAIRS = [(a, b) for a in range(EXPERTS_PER_GROUP) for b in range(a + 1, EXPERTS_PER_GROUP)]
_BUCKET_ELO = np.array([g * EXPERTS_PER_GROUP + _PAIRS[p][0] for g in range(MOE_GROUPS) for p in range(N_PAIRS)], np.int32)
_BUCKET_EHI = np.array([g * EXPERTS_PER_GROUP + _PAIRS[p][1] for g in range(MOE_GROUPS) for p in range(N_PAIRS)], np.int32)


def _sorted_layout(meta, cnt, t):
    tm = MOE_TILE
    n_tiles = t // tm + N_BUCKETS
    bucket = meta[:, 0].astype(jnp.int32)
    rank = meta[:, 3].astype(jnp.int32)
    counts = cnt[:N_BUCKETS, 0].astype(jnp.int32)
    padded = ((counts + tm - 1) // tm) * tm
    ends = jnp.cumsum(padded)
    starts = ends - padded
    dest = starts[bucket] + rank
    tile_start = jnp.arange(n_tiles, dtype=jnp.int32) * tm
    tile_valid = (tile_start < ends[-1]).astype(jnp.int32)
    last_valid = jnp.maximum(ends[-1] // tm - 1, 0)
    tile_bucket = jnp.searchsorted(ends, jnp.minimum(tile_start, last_valid * tm), side="right").astype(jnp.int32)
    tile_bucket = jnp.minimum(tile_bucket, N_BUCKETS - 1)
    return dest, jnp.asarray(_BUCKET_ELO)[tile_bucket], jnp.asarray(_BUCKET_EHI)[tile_bucket], tile_valid, n_tiles * tm


def _moe_layer(h2, meta, cnt, w_gate, w_up, w_down):
    t = h2.shape[0]
    dest, tile_elo, tile_ehi, tile_valid, tp = _sorted_layout(meta, cnt, t)
    src = jnp.zeros((tp,), jnp.int32).at[dest].set(jnp.arange(t, dtype=jnp.int32))
    xs = jnp.take(h2, src, axis=0)
    ms = jnp.take(meta, src, axis=0)
    ys = _moe(tile_elo, tile_ehi, tile_valid, xs, ms, w_gate, w_up, w_down)
    return jnp.take(ys, dest, axis=0)


def _router_weights(wg, bg, we, be):
    d = wg.shape[0]
    wr = jnp.zeros((32, d), F32).at[:MOE_GROUPS].set(wg.T).at[SUBLANES:SUBLANES + N_EXPERTS].set(we.T)
    br = jnp.zeros((32, 1), F32).at[:MOE_GROUPS, 0].set(bg).at[SUBLANES:SUBLANES + N_EXPERTS, 0].set(be)
    return wr.astype(BF16), br


def kernel(x, rel_bias, norm_mix, norm_ffn, norm_final, w_in_even, w_out_even, m_igate_b, m_fgate_b,
           m_head_norm, lam_q1, lam_k1, lam_q2, lam_k2, da_subln, w_in_odd, conv_w, w_out_odd,
           router_group_w, router_group_b, router_expert_w, router_expert_b,
           exp_w_gate, exp_w_up, exp_w_down):
    bsz, seq, d = x.shape
    t = bsz * seq
    x2d = x.reshape(t, d)
    mw = M_HEADS * M_HEAD_DIM

    w_in = w_in_even[0]
    gate0 = 4 * mw
    w_main = jnp.concatenate([w_in[:, :gate0], w_in[:, gate0 + 2 * M_HEADS:]], axis=1).astype(BF16)
    w_gates = w_in[:, gate0:gate0 + 2 * M_HEADS]
    wg_col = jnp.zeros((d, LANES), F32).at[:, :2 * M_HEADS].set(w_gates).astype(BF16)
    wg_row = w_gates.T.astype(BF16)
    gate_b = jnp.concatenate([m_igate_b[0], m_fgate_b[0]]).astype(F32)
    b_col = jnp.zeros((1, LANES), F32).at[0, :2 * M_HEADS].set(gate_b)
    b_row = gate_b.reshape(2 * M_HEADS, 1)
    triu = jnp.triu(jnp.ones((ROW_TILE, ROW_TILE), F32)).astype(BF16)

    p, gc, gr = _inproj_even(x2d, norm_mix[0].reshape(1, d), w_main, wg_col, wg_row)
    p3 = p.reshape(bsz, seq, -1)
    hm = _mlstm(p3, gc.reshape(bsz, seq, LANES), gr, b_col, b_row, m_head_norm[0].reshape(1, mw))

    lam_init = 0.8 - 0.6 * math.exp(-0.3 * 0)
    lam = (jnp.exp(jnp.sum(lam_q1[0].astype(F32) * lam_k1[0].astype(F32)))
           - jnp.exp(jnp.sum(lam_q2[0].astype(F32) * lam_k2[0].astype(F32))) + lam_init).reshape(1)
    ha = _attention(lam, p3, _bias_tiles(rel_bias.astype(F32), ATT_BLOCK),
                    da_subln[0].reshape(1, DA_V_DIM).astype(F32), 1.0 - lam_init)

    wr0, br0 = _router_weights(router_group_w[0], router_group_b[0], router_expert_w[0], router_expert_b[0])
    x1, h2, meta, cnt = _outproj_even(hm.reshape(t, mw), ha.reshape(t, -1), x2d, w_out_even[0].astype(BF16),
                                      norm_ffn[0].reshape(1, d), wr0, br0, triu)
    y = _moe_layer(h2, meta, cnt, exp_w_gate[0].astype(BF16), exp_w_up[0].astype(BF16),
                   exp_w_down[0].astype(BF16))

    wr1, br1 = _router_weights(router_group_w[1], router_group_b[1], router_expert_w[1], router_expert_b[1])
    cw = jnp.zeros((SUBLANES, d), F32).at[:conv_w.shape[1]].set(conv_w[0])
    x3, h2, meta, cnt = _conv_mixer(x1, y, norm_mix[1].reshape(1, d), w_in_odd[0].astype(BF16), cw,
                                    w_out_odd[0].astype(BF16), norm_ffn[1].reshape(1, d), wr1, br1, triu, seq)
    y = _moe_layer(h2, meta, cnt, exp_w_gate[1].astype(BF16), exp_w_up[1].astype(BF16),
                   exp_w_down[1].astype(BF16))

    return _final(x3, y, norm_final.reshape(1, d)).reshape(bsz, seq, d)
```

```python
import functools
import math

import jax
import jax.numpy as jnp
import numpy as np
from jax import lax
from jax.experimental import pallas as pl
from jax.experimental.pallas import tpu as pltpu

F32 = jnp.float32
BF16 = jnp.bfloat16

NORM_EPS = 1e-6
NEG_BIG = -1e30

M_HEADS = 4
M_HEAD_DIM = 128
DA_HEADS = 4
DA_HEAD_DIM = 64
DA_V_DIM = 128
REL_BUCKETS = 32
REL_MAX_DIST = 128
MOE_GROUPS = 4
EXPERTS_PER_GROUP = 4
N_EXPERTS = 16
N_PAIRS = 6
N_BUCKETS = MOE_GROUPS * N_PAIRS

LANES = 128
SUBLANES = 8
VMEM_LIMIT = 56 * 1024 * 1024

ROW_TILE = 512
MLSTM_CHUNK = 256
ATT_BLOCK = 256
MOE_TILE = 512

_NT = (((1,), (1,)), ((), ()))


def _cparams(*sem):
    return pltpu.CompilerParams(dimension_semantics=sem, vmem_limit_bytes=VMEM_LIMIT)


def _rms(x, g):
    ms = jnp.mean(x * x, axis=-1, keepdims=True)
    return x * lax.rsqrt(ms + NORM_EPS) * g


def _dot(a, b):
    return jnp.dot(a, b, preferred_element_type=F32)


def _split3(x):
    hi = x.astype(BF16)
    r1 = x - hi.astype(F32)
    mid = r1.astype(BF16)
    lo = (r1 - mid.astype(F32)).astype(BF16)
    return hi, mid, lo


def _log_sigmoid(x):
    return jnp.minimum(x, 0.0) - jnp.log(1.0 + jnp.exp(-jnp.abs(x)))


def _bias_tiles_kernel(rb_ref, out_ref):
    h = pl.program_id(0)
    blk = out_ref.shape[-1]
    r = lax.broadcasted_iota(jnp.int32, (blk, blk), 0)
    c = lax.broadcasted_iota(jnp.int32, (blk, blk), 1)
    far = rb_ref[REL_BUCKETS - 1, h]
    max_exact = REL_BUCKETS // 2
    for j in range(2):
        rel = c - r - j * blk
        n = jnp.maximum(-rel, 0)
        nf = jnp.maximum(n, 1).astype(F32)
        large = max_exact + (jnp.log(nf / max_exact) / math.log(REL_MAX_DIST / max_exact)
                             * (REL_BUCKETS - max_exact)).astype(jnp.int32)
        large = jnp.minimum(large, REL_BUCKETS - 1)
        bucket = jnp.where(n < max_exact, n, large)
        val = jnp.zeros((blk, blk), F32)
        for b in range(REL_BUCKETS):
            val = jnp.where(bucket == b, rb_ref[b, h], val)
        val = val - far
        if j == 0:
            val = jnp.where(rel <= 0, val, NEG_BIG)
        out_ref[0, j] = val


def _bias_tiles(rel_bias, blk):
    return pl.pallas_call(
        _bias_tiles_kernel,
        out_shape=jax.ShapeDtypeStruct((DA_HEADS, 2, blk, blk), F32),
        grid=(DA_HEADS,),
        in_specs=[pl.BlockSpec(memory_space=pltpu.SMEM)],
        out_specs=pl.BlockSpec((1, 2, blk, blk), lambda h: (h, 0, 0, 0)),
        compiler_params=_cparams("arbitrary"),
        name="bias_tiles",
    )(rel_bias)


def _inproj_even_kernel(x_ref, g_ref, w_ref, wgc_ref, wgr_ref, p_ref, gc_ref, gr_ref):
    h = _rms(x_ref[...], g_ref[...]).astype(BF16)
    n = w_ref.shape[1]
    for j in range(n // 512):
        sl = slice(j * 512, (j + 1) * 512)
        p_ref[:, sl] = _dot(h, w_ref[:, sl]).astype(BF16)
    gc_ref[...] = _dot(h, wgc_ref[...])
    gr_ref[...] = lax.dot_general(wgr_ref[...], h, _NT, preferred_element_type=F32)


def _inproj_even(x2d, g, w_main, wg_col, wg_row):
    t, d = x2d.shape
    n = w_main.shape[1]
    tm = ROW_TILE
    return pl.pallas_call(
        _inproj_even_kernel,
        out_shape=(jax.ShapeDtypeStruct((t, n), BF16),
                   jax.ShapeDtypeStruct((t, LANES), F32),
                   jax.ShapeDtypeStruct((SUBLANES, t), F32)),
        grid=(t // tm,),
        in_specs=[pl.BlockSpec((tm, d), lambda i: (i, 0)),
                  pl.BlockSpec((1, d), lambda i: (0, 0)),
                  pl.BlockSpec((d, n), lambda i: (0, 0)),
                  pl.BlockSpec((d, LANES), lambda i: (0, 0)),
                  pl.BlockSpec((SUBLANES, d), lambda i: (0, 0))],
        out_specs=(pl.BlockSpec((tm, n), lambda i: (i, 0)),
                   pl.BlockSpec((tm, LANES), lambda i: (i, 0)),
                   pl.BlockSpec((SUBLANES, tm), lambda i: (0, i))),
        compiler_params=_cparams("arbitrary"),
        name="inproj_even",
    )(x2d, g, w_main, wg_col, wg_row)


def _mlstm_kernel(q_ref, k_ref, v_ref, og_ref, gc_ref, gr_ref, bc_ref, br_ref, hn_ref, out_ref,
                  c_ref, m_ref):
    L = q_ref.shape[1]
    dh = M_HEAD_DIM
    scale = dh ** -0.5

    @pl.when(pl.program_id(1) == 0)
    def _():
        c_ref[...] = jnp.zeros_like(c_ref)
        m_ref[...] = jnp.zeros_like(m_ref)

    gc = gc_ref[0] + bc_ref[...]
    gr = gr_ref[...] + br_ref[...]
    lf_c = _log_sigmoid(gc)
    lf_r = _log_sigmoid(gr)

    row = lax.broadcasted_iota(jnp.int32, (L, L), 0)
    col = lax.broadcasted_iota(jnp.int32, (L, L), 1)
    causal = row >= col
    tril = jnp.where(causal, 1.0, 0.0).astype(BF16)
    triu = jnp.where(row <= col, 1.0, 0.0).astype(BF16)
    b_c = sum(_dot(tril, part) for part in _split3(lf_c))
    b_r = sum(_dot(part, triu) for part in _split3(lf_r))

    lane = lax.broadcasted_iota(jnp.int32, (L, LANES), 1)
    ones_col = jnp.where(lane == 0, 1.0, 0.0).astype(BF16)

    for h in range(M_HEADS):
        hs = slice(h * dh, (h + 1) * dh)
        q = q_ref[0, :, hs]
        k = k_ref[0, :, hs]
        v = v_ref[0, :, hs]
        i_r = gr[h:h + 1, :]
        b_rh = b_r[M_HEADS + h:M_HEADS + h + 1, :]
        i_c = gc[:, h:h + 1]
        b_ch = b_c[:, M_HEADS + h:M_HEADS + h + 1]
        m_prev = m_ref[h:h + 1, 0:1]

        dmat = jnp.where(causal, b_ch - b_rh + i_r, NEG_BIG)
        inter = b_ch + m_prev
        mt = jnp.maximum(inter, jnp.max(dmat, axis=-1, keepdims=True))
        w = jnp.exp(dmat - mt)
        sc = jnp.exp(inter - mt)
        s = lax.dot_general(q, k, _NT, preferred_element_type=F32)
        qk = (s * (w * scale)).astype(BF16)

        vaug = jnp.concatenate([v, ones_col], axis=-1)
        caug = c_ref[h]
        num_aug = sc * _dot(q, caug.astype(BF16)) + _dot(qk, vaug)
        num = num_aug[:, :dh]
        den = num_aug[:, dh:dh + 1]
        hh = num / jnp.maximum(jnp.abs(den), jnp.exp(-mt))

        mu = jnp.mean(hh, axis=-1, keepdims=True)
        cen = hh - mu
        var = jnp.mean(cen * cen, axis=-1, keepdims=True)
        hn = cen * lax.rsqrt(var + NORM_EPS) * hn_ref[:, hs]
        gate = jax.nn.sigmoid(og_ref[0, :, hs].astype(F32))
        out_ref[0, :, hs] = (hn * gate).astype(out_ref.dtype)

        g_end = b_ch[L - 1:L, :]
        ws = g_end - b_ch + i_c
        m_new = jnp.maximum(g_end + m_prev, jnp.max(ws, axis=0, keepdims=True))
        a = jnp.exp(g_end + m_prev - m_new)
        wk = (jnp.exp(ws - m_new) * scale) * k.astype(F32)
        upd = _dot(wk.T.astype(BF16), vaug)
        c_ref[h] = a * caug + upd
        m_ref[h:h + 1, :] = jnp.broadcast_to(m_new, (1, LANES))


def _mlstm(p3, gc3, gr, b_col, b_row, head_norm):
    bsz, s, _ = p3.shape
    L = MLSTM_CHUNK
    w = M_HEADS * M_HEAD_DIM
    nc = s // L
    blk = lambda j: pl.BlockSpec((1, L, w), lambda b, c, j=j: (b, c, j))
    return pl.pallas_call(
        _mlstm_kernel,
        out_shape=jax.ShapeDtypeStruct((bsz, s, w), BF16),
        grid=(bsz, nc),
        in_specs=[blk(0), blk(1), blk(2), blk(3),
                  pl.BlockSpec((1, L, LANES), lambda b, c: (b, c, 0)),
                  pl.BlockSpec((SUBLANES, L), lambda b, c: (0, b * nc + c)),
                  pl.BlockSpec((1, LANES), lambda b, c: (0, 0)),
                  pl.BlockSpec((SUBLANES, 1), lambda b, c: (0, 0)),
                  pl.BlockSpec((1, w), lambda b, c: (0, 0))],
        out_specs=pl.BlockSpec((1, L, w), lambda b, c: (b, c, 0)),
        scratch_shapes=[pltpu.VMEM((M_HEADS, M_HEAD_DIM, 2 * M_HEAD_DIM), F32),
                        pltpu.VMEM((SUBLANES, LANES), F32)],
        compiler_params=_cparams("arbitrary", "arbitrary"),
        name="mlstm",
    )(p3, p3, p3, p3, gc3, gr, b_col, b_row, head_norm)


def _attn_kernel(lam_ref, q_ref, k_ref, v_ref, bias_ref, sub_ref, out_ref, vaug_ref, *, out_scale, blk):
    s_len = q_ref.shape[1]
    d = DA_HEAD_DIM
    lane_s = lax.broadcasted_iota(jnp.int32, (s_len, DA_V_DIM), 1)
    vaug_ref[:, :DA_V_DIM] = v_ref[0]
    vaug_ref[:, DA_V_DIM:] = jnp.where(lane_s == 0, 1.0, 0.0).astype(BF16)
    lam = lam_ref[0]
    lane = lax.broadcasted_iota(jnp.int32, (blk, LANES), 1)
    for qi in range(s_len // blk):
        q0, q1 = qi * blk, (qi + 1) * blk
        qs = q_ref[0, q0:q1, :] * (d ** -0.5)
        zero = jnp.zeros_like(qs)
        heads = []
        for qm in (jnp.where(lane < d, qs, zero), jnp.where(lane >= d, qs, zero)):
            score = lambda a, b: lax.dot_general(qm, k_ref[0, a:b, :], _NT, preferred_element_type=F32)
            parts = [score(q0, q1) + bias_ref[0, 0]]
            if qi >= 1:
                parts.append(score(q0 - blk, q0) + bias_ref[0, 1])
            if qi >= 2:
                parts.append(score(0, q0 - blk))
            m = functools.reduce(jnp.maximum, [jnp.max(p, axis=-1, keepdims=True) for p in parts])
            pexp = [jnp.exp(p - m).astype(BF16) for p in parts]
            o = _dot(pexp[0], vaug_ref[q0:q1, :])
            if qi >= 1:
                o = o + _dot(pexp[1], vaug_ref[q0 - blk:q0, :])
            if qi >= 2:
                o = o + _dot(pexp[2], vaug_ref[0:q0 - blk, :])
            heads.append(o[:, :DA_V_DIM] / o[:, DA_V_DIM:DA_V_DIM + 1])
        o = heads[0] - lam * heads[1]
        ms = jnp.mean(o * o, axis=-1, keepdims=True)
        o = o * lax.rsqrt(ms + NORM_EPS) * sub_ref[...] * out_scale
        out_ref[0, q0:q1, :] = o.astype(out_ref.dtype)


def _attention(lam, p3, bias_tiles, subln, out_scale):
    bsz, s, _ = p3.shape
    blk = ATT_BLOCK
    w = DA_HEADS * DA_V_DIM
    q_blk = 4 * M_HEADS
    k_blk = q_blk + DA_HEADS
    v_blk = k_blk + DA_HEADS
    col = lambda base: pl.BlockSpec((1, s, LANES), lambda b, h: (b, 0, base + h))
    return pl.pallas_call(
        functools.partial(_attn_kernel, out_scale=out_scale, blk=blk),
        out_shape=jax.ShapeDtypeStruct((bsz, s, w), BF16),
        grid=(bsz, DA_HEADS),
        in_specs=[pl.BlockSpec(memory_space=pltpu.SMEM),
                  col(q_blk), col(k_blk), col(v_blk),
                  pl.BlockSpec((1, 2, blk, blk), lambda b, h: (h, 0, 0, 0)),
                  pl.BlockSpec((1, DA_V_DIM), lambda b, h: (0, 0))],
        out_specs=pl.BlockSpec((1, s, DA_V_DIM), lambda b, h: (b, 0, h)),
        scratch_shapes=[pltpu.VMEM((s, 2 * DA_V_DIM), BF16)],
        compiler_params=_cparams("arbitrary", "arbitrary"),
        name="diff_attention",
    )(lam, p3, p3, p3, bias_tiles, subln)


def _pack_bf16_pairs(v):
    half = v.shape[1] // 2
    hi = lax.bitcast_convert_type(v[:, :half].astype(F32), jnp.uint32)
    lo = lax.bitcast_convert_type(v[:, half:].astype(F32), jnp.uint32)
    return hi | (lo >> 16)


def _unpack_bf16_pairs(w):
    hi = lax.bitcast_convert_type(w & jnp.uint32(0xFFFF0000), F32)
    lo = lax.bitcast_convert_type(w << 16, F32)
    return hi, lo


def _router_epilogue(xn, g_ref, wr_ref, br_ref, triu_ref, hrow_ref, route_ref, cnt_ref):
    tm, d = xn.shape
    h2 = _rms(xn, g_ref[...]).astype(BF16)
    hrow_ref[:, :d // 2] = _pack_bf16_pairs(h2)

    lg = lax.dot_general(wr_ref[...], h2, _NT, preferred_element_type=F32) + br_ref[...]
    gl = [lg[g:g + 1, :] for g in range(MOE_GROUPS)]
    gmax = functools.reduce(jnp.maximum, gl)
    gsel = jnp.where(gl[0] == gmax, 0.0, jnp.where(gl[1] == gmax, 1.0, jnp.where(gl[2] == gmax, 2.0, 3.0)))
    pg = 1.0 / functools.reduce(lambda a, b: a + b, [jnp.exp(x - gmax) for x in gl])

    ev = []
    for j in range(EXPERTS_PER_GROUP):
        acc = jnp.zeros_like(gmax)
        for g in range(MOE_GROUPS):
            r0 = SUBLANES + g * EXPERTS_PER_GROUP + j
            acc = jnp.where(gsel == float(g), lg[r0:r0 + 1, :], acc)
        ev.append(acc)
    v1 = functools.reduce(jnp.maximum, ev)
    i1 = jnp.where(ev[0] == v1, 0.0, jnp.where(ev[1] == v1, 1.0, jnp.where(ev[2] == v1, 2.0, 3.0)))
    ev2 = [jnp.where(i1 == float(j), -jnp.inf, ev[j]) for j in range(EXPERTS_PER_GROUP)]
    v2 = functools.reduce(jnp.maximum, ev2)
    hit = [(ev2[j] == v2) & (i1 != float(j)) for j in range(EXPERTS_PER_GROUP)]
    i2 = jnp.where(hit[0], 0.0, jnp.where(hit[1], 1.0, jnp.where(hit[2], 2.0, 3.0)))
    e21 = jnp.exp(v2 - v1)
    w1 = 1.0 / (1.0 + e21)
    w2 = e21 * w1
    first_low = i1 < i2
    lo = jnp.minimum(i1, i2)
    hi = jnp.maximum(i1, i2)
    c_lo = pg * jnp.where(first_low, w1, w2)
    c_hi = pg * jnp.where(first_low, w2, w1)
    pair = lo * (7.0 - lo) * 0.5 + hi - lo - 1.0
    bucket = gsel * float(N_PAIRS) + pair

    @pl.when(pl.program_id(0) == 0)
    def _():
        cnt_ref[...] = jnp.zeros_like(cnt_ref)

    rows = lax.broadcasted_iota(jnp.int32, (32, tm), 0).astype(F32)
    onehot = rows == bucket
    cum = _dot(jnp.where(onehot, 1.0, 0.0).astype(BF16), triu_ref[...])
    carry = cnt_ref[:, 0:1]
    rank = jnp.sum(jnp.where(onehot, cum + carry, 0.0), axis=0, keepdims=True) - 1.0
    cnt_ref[...] = jnp.broadcast_to(carry + cum[:, tm - 1:tm], cnt_ref.shape)

    rid = lax.broadcasted_iota(jnp.int32, (LANES, tm), 0)
    rec = jnp.where(rid == 0, bucket, jnp.where(rid == 1, c_lo, jnp.where(rid == 2, c_hi,
                    jnp.where(rid == 3, rank, 0.0))))
    hrow_ref[:, d // 2:] = lax.bitcast_convert_type(rec.T, jnp.uint32)
    route_ref[...] = rec[:SUBLANES, :]


_EPI_OUT_SHAPES = lambda t, d: (jax.ShapeDtypeStruct((t, d), F32),
                                jax.ShapeDtypeStruct((t, d // 2 + LANES), jnp.uint32),
                                jax.ShapeDtypeStruct((SUBLANES, t), F32),
                                jax.ShapeDtypeStruct((32, LANES), F32))


def _epi_out_specs(tm, d):
    return (pl.BlockSpec((tm, d), lambda i: (i, 0)),
            pl.BlockSpec((tm, d // 2 + LANES), lambda i: (i, 0)),
            pl.BlockSpec((SUBLANES, tm), lambda i: (0, i)),
            pl.BlockSpec((32, LANES), lambda i: (0, 0)))


def _epi_in_specs(tm, d):
    return [pl.BlockSpec((1, d), lambda i: (0, 0)),
            pl.BlockSpec((32, d), lambda i: (0, 0)),
            pl.BlockSpec((32, 1), lambda i: (0, 0)),
            pl.BlockSpec((tm, tm), lambda i: (0, 0))]


def _outproj_even_kernel(hm_ref, ha_ref, x_ref, wo_ref, g_ref, wr_ref, br_ref, triu_ref,
                         xn_ref, h2_ref, meta_ref, cnt_ref):
    half = hm_ref.shape[1]
    y = _dot(hm_ref[...], wo_ref[:half, :]) + _dot(ha_ref[...], wo_ref[half:, :])
    xn = x_ref[...] + y
    xn_ref[...] = xn
    _router_epilogue(xn, g_ref, wr_ref, br_ref, triu_ref, h2_ref, meta_ref, cnt_ref)


def _outproj_even(hm, ha, x2d, w_out, g, wr, br, triu):
    t, d = x2d.shape
    tm = ROW_TILE
    half = hm.shape[1]
    return pl.pallas_call(
        _outproj_even_kernel,
        out_shape=_EPI_OUT_SHAPES(t, d),
        grid=(t // tm,),
        in_specs=[pl.BlockSpec((tm, half), lambda i: (i, 0)),
                  pl.BlockSpec((tm, half), lambda i: (i, 0)),
                  pl.BlockSpec((tm, d), lambda i: (i, 0)),
                  pl.BlockSpec((2 * half, d), lambda i: (0, 0))] + _epi_in_specs(tm, d),
        out_specs=_epi_out_specs(tm, d),
        compiler_params=_cparams("arbitrary"),
        name="outproj_even",
    )(hm, ha, x2d, w_out, g, wr, br, triu)


def _conv_mixer_kernel(x_ref, y_ref, gm_ref, wi_ref, cw_ref, wo_ref, g_ref, wr_ref, br_ref, triu_ref,
                       xn_ref, h2_ref, meta_ref, cnt_ref, tail_ref, *, tiles_per_seq):
    tm, d = x_ref.shape
    x = x_ref[...] + jnp.concatenate(_unpack_bf16_pairs(y_ref[...]), axis=1)
    h = _rms(x, gm_ref[...]).astype(BF16)

    @pl.when(pl.program_id(0) % tiles_per_seq == 0)
    def _():
        tail_ref[...] = jnp.zeros_like(tail_ref)

    row = lax.broadcasted_iota(jnp.int32, (tm, 512), 0)
    acc = jnp.zeros((tm, d), F32)
    for j in range(d // 512):
        sl = slice(j * 512, (j + 1) * 512)
        bg = _dot(h, wi_ref[:, sl])
        cg = _dot(h, wi_ref[:, d + j * 512:d + (j + 1) * 512])
        u = _dot(h, wi_ref[:, 2 * d + j * 512:2 * d + (j + 1) * 512])
        z = cg * u
        prev1 = tail_ref[SUBLANES - 1:SUBLANES, sl]
        prev2 = tail_ref[SUBLANES - 2:SUBLANES - 1, sl]
        z1 = jnp.where(row == 0, prev1, pltpu.roll(z, 1, axis=0))
        z2 = jnp.where(row == 0, prev2, jnp.where(row == 1, prev1, pltpu.roll(z, 2, axis=0)))
        zc = cw_ref[0:1, sl] * z2 + cw_ref[1:2, sl] * z1 + cw_ref[2:3, sl] * z
        tail_ref[:, sl] = z[tm - SUBLANES:, :]
        acc = acc + _dot((bg * zc).astype(BF16), wo_ref[sl, :])
    xn = x + acc
    xn_ref[...] = xn
    _router_epilogue(xn, g_ref, wr_ref, br_ref, triu_ref, h2_ref, meta_ref, cnt_ref)


def _conv_mixer(x2d, ymoe, gm, w_in, conv_w, w_out, g, wr, br, triu, seq):
    t, d = x2d.shape
    tm = ROW_TILE
    return pl.pallas_call(
        functools.partial(_conv_mixer_kernel, tiles_per_seq=seq // tm),
        out_shape=_EPI_OUT_SHAPES(t, d),
        grid=(t // tm,),
        in_specs=[pl.BlockSpec((tm, d), lambda i: (i, 0)),
                  pl.BlockSpec((tm, d // 2), lambda i: (i, 0)),
                  pl.BlockSpec((1, d), lambda i: (0, 0)),
                  pl.BlockSpec((d, 3 * d), lambda i: (0, 0)),
                  pl.BlockSpec((SUBLANES, d), lambda i: (0, 0)),
                  pl.BlockSpec((d, d), lambda i: (0, 0))] + _epi_in_specs(tm, d),
        out_specs=_epi_out_specs(tm, d),
        scratch_shapes=[pltpu.VMEM((SUBLANES, d), F32)],
        compiler_params=_cparams("arbitrary"),
        name="conv_mixer",
    )(x2d, ymoe, gm, w_in, conv_w, w_out, g, wr, br, triu)


def _moe_kernel(elo_ref, ehi_ref, valid_ref, fresh_ref,
                src_ref, src_next_ref, dst_prev_ref,
                h_hbm,
                wg_lo, wu_lo, wd_lo, wg_hi, wu_hi, wd_hi,
                y_hbm,
                xbuf, obuf, zbuf, wbuf_in, wbuf_out, gsem, ssem, zsem):
    i = pl.program_id(0)
    n = pl.num_programs(0)
    tm = xbuf.shape[1]
    half = obuf.shape[2]
    slot = i % 2
    valid = valid_ref[i] != 0

    def start_gather(idx_ref, s):
        for r in range(tm):
            pltpu.make_async_copy(h_hbm.at[pl.ds(idx_ref[0, 0, r], 1)], xbuf.at[s, pl.ds(r, 1)], gsem.at[s]).start()

    def start_scatter_prev():
        for r in range(tm):
            pltpu.make_async_copy(obuf.at[1 - slot, pl.ds(r, 1)], y_hbm.at[pl.ds(dst_prev_ref[0, 0, r], 1)],
                                  ssem.at[1 - slot]).start()

    wait_gather = lambda: pltpu.make_async_copy(h_hbm.at[pl.ds(0, tm)], xbuf.at[slot], gsem.at[slot]).wait()
    wait_scatter_prev = lambda: pltpu.make_async_copy(obuf.at[1 - slot], y_hbm.at[pl.ds(0, tm)],
                                                      ssem.at[1 - slot]).wait()

    @pl.when(i == 0)
    def _():
        obuf[1] = jnp.zeros(obuf.shape[1:], obuf.dtype)
        zbuf[...] = jnp.zeros_like(zbuf)
        start_gather(src_ref, 0)

    @pl.when(fresh_ref[i] != 0)
    def _():
        for k, w in enumerate((wg_lo, wu_lo, wg_hi, wu_hi)):
            wbuf_in[k] = w[0].astype(BF16)
        for k, w in enumerate((wd_lo, wd_hi)):
            wbuf_out[k] = w[0].astype(BF16)

    @pl.when(valid)
    def _():
        wait_gather()
        start_gather(src_next_ref, 1 - slot)
        start_scatter_prev()
        rows = xbuf[slot]
        xa, xb = _unpack_bf16_pairs(rows[:, :half])
        x = jnp.concatenate([xa.astype(BF16), xb.astype(BF16)], axis=1)
        rec = lax.bitcast_convert_type(rows[:, half:], F32)

        def ffn(k):
            a = _dot(x, wbuf_in[2 * k])
            a = (a * jax.nn.sigmoid(a)) * _dot(x, wbuf_in[2 * k + 1])
            return _dot(a.astype(BF16), wbuf_out[k])

        out = rec[:, 1:2] * ffn(0) + rec[:, 2:3] * ffn(1)
        obuf[slot] = _pack_bf16_pairs(out.astype(BF16))
        wait_scatter_prev()

    @pl.when(jnp.logical_not(valid) & (valid_ref[jnp.maximum(i - 1, 0)] != 0))
    def _():
        wait_gather()
        start_scatter_prev()
        wait_scatter_prev()

    @pl.when(jnp.logical_not(valid) & (i < n - 1))
    def _():
        fill = pltpu.make_async_copy(zbuf, y_hbm.at[pl.ds(pl.multiple_of(i * tm, tm), tm)], zsem)
        fill.start()
        fill.wait()


def _moe(tiles, src_rows, dst_rows, hrows, w_gate, w_up, w_down):
    tile_elo, tile_ehi, tile_valid, tile_fresh = tiles
    n_tiles = tile_elo.shape[0]
    t, wrow = hrows.shape
    d = w_gate.shape[1]
    ff = w_gate.shape[2]
    tm = MOE_TILE
    lo = lambda i, elo, *_: (elo[i], 0, 0)
    hi = lambda i, elo, ehi, *_: (ehi[i], 0, 0)
    cur = lambda i, *_: (i, 0, 0)
    nxt = lambda i, *_: ((i + 1) % n_tiles, 0, 0)
    prv = lambda i, *_: ((i + n_tiles - 1) % n_tiles, 0, 0)
    idx = lambda m: pl.BlockSpec((1, 1, tm), m, memory_space=pltpu.SMEM)
    return pl.pallas_call(
        _moe_kernel,
        out_shape=jax.ShapeDtypeStruct((n_tiles * tm, d // 2), jnp.uint32),
        grid_spec=pltpu.PrefetchScalarGridSpec(
            num_scalar_prefetch=4,
            grid=(n_tiles,),
            in_specs=[idx(cur), idx(nxt), idx(prv),
                      pl.BlockSpec(memory_space=pl.ANY),
                      pl.BlockSpec((1, d, ff), lo), pl.BlockSpec((1, d, ff), lo), pl.BlockSpec((1, ff, d), lo),
                      pl.BlockSpec((1, d, ff), hi), pl.BlockSpec((1, d, ff), hi), pl.BlockSpec((1, ff, d), hi)],
            out_specs=pl.BlockSpec(memory_space=pl.ANY),
            scratch_shapes=[pltpu.VMEM((2, tm, wrow), jnp.uint32),
                            pltpu.VMEM((2, tm, d // 2), jnp.uint32),
                            pltpu.VMEM((tm, d // 2), jnp.uint32),
                            pltpu.VMEM((4, d, ff), BF16),
                            pltpu.VMEM((2, ff, d), BF16),
                            pltpu.SemaphoreType.DMA((2,)),
                            pltpu.SemaphoreType.DMA((2,)),
                            pltpu.SemaphoreType.DMA]),
        compiler_params=_cparams("arbitrary"),
        name="grouped_moe",
    )(tile_elo, tile_ehi, tile_valid, tile_fresh, src_rows, src_rows, dst_rows, hrows,
      w_gate, w_up, w_down, w_gate, w_up, w_down)


def _final_kernel(x_ref, y_ref, g_ref, out_ref):
    out_ref[...] = _rms(x_ref[...] + jnp.concatenate(_unpack_bf16_pairs(y_ref[...]), axis=1), g_ref[...])


def _final(x2d, ymoe, g):
    t, d = x2d.shape
    tm = ROW_TILE
    return pl.pallas_call(
        _final_kernel,
        out_shape=jax.ShapeDtypeStruct((t, d), F32),
        grid=(t // tm,),
        in_specs=[pl.BlockSpec((tm, d), lambda i: (i, 0)),
                  pl.BlockSpec((tm, d // 2), lambda i: (i, 0)),
                  pl.BlockSpec((1, d), lambda i: (0, 0))],
        out_specs=pl.BlockSpec((tm, d), lambda i: (i, 0)),
        compiler_params=_cparams("arbitrary"),
        name="final_norm",
    )(x2d, ymoe, g)


_PAIRS = [(a, b) for a in range(EXPERTS_PER_GROUP) for b in range(a + 1, EXPERTS_PER_GROUP)]
_BUCKET_ELO = np.array([g * EXPERTS_PER_GROUP + _PAIRS[p][0] for g in range(MOE_GROUPS) for p in range(N_PAIRS)], np.int32)
_BUCKET_EHI = np.array([g * EXPERTS_PER_GROUP + _PAIRS[p][1] for g in range(MOE_GROUPS) for p in range(N_PAIRS)], np.int32)


def _moe_layer(hrows, route, cnt, layer, w_gate, w_up, w_down):
    t = hrows.shape[0]
    tm = MOE_TILE
    n_tiles = t // tm + N_BUCKETS
    tp = n_tiles * tm
    bucket = route[0].astype(jnp.int32)
    rank = route[3].astype(jnp.int32)
    counts = cnt[:N_BUCKETS, 0].astype(jnp.int32)
    padded = ((counts + tm - 1) // tm) * tm
    ends = jnp.cumsum(padded)
    dest = (ends - padded)[bucket] + rank
    tile_start = jnp.arange(n_tiles, dtype=jnp.int32) * tm
    tile_valid = tile_start < ends[-1]
    last_start = jnp.maximum(ends[-1] - tm, 0)
    tile_bucket = jnp.sum((ends[None, :] <= jnp.minimum(tile_start, last_start)[:, None]).astype(jnp.int32), axis=1)
    prev_bucket = jnp.concatenate([jnp.full((1,), -1, jnp.int32), tile_bucket[:-1]])
    tile_fresh = tile_valid & (tile_bucket != prev_bucket)
    base = layer * N_EXPERTS
    token_at = jnp.full((tp,), -1, jnp.int32).at[dest].set(jnp.arange(t, dtype=jnp.int32))
    is_pad = token_at < 0
    src_rows = jnp.where(is_pad, 0, token_at)
    dst_rows = jnp.where(is_pad, t - 1 + jnp.cumsum(is_pad.astype(jnp.int32)), token_at)
    tiles = (jnp.asarray(_BUCKET_ELO)[tile_bucket] + base, jnp.asarray(_BUCKET_EHI)[tile_bucket] + base,
             tile_valid.astype(jnp.int32), tile_fresh.astype(jnp.int32))
    shape = (n_tiles, 1, tm)
    return _moe(tiles, src_rows.reshape(shape), dst_rows.reshape(shape), hrows, w_gate, w_up, w_down)


def _router_weights(wg, bg, we, be):
    d = wg.shape[0]
    wr = jnp.zeros((32, d), F32).at[:MOE_GROUPS].set(wg.T).at[SUBLANES:SUBLANES + N_EXPERTS].set(we.T)
    br = jnp.zeros((32, 1), F32).at[:MOE_GROUPS, 0].set(bg).at[SUBLANES:SUBLANES + N_EXPERTS, 0].set(be)
    return wr.astype(BF16), br


def kernel(x, rel_bias, norm_mix, norm_ffn, norm_final, w_in_even, w_out_even, m_igate_b, m_fgate_b,
           m_head_norm, lam_q1, lam_k1, lam_q2, lam_k2, da_subln, w_in_odd, conv_w, w_out_odd,
           router_group_w, router_group_b, router_expert_w, router_expert_b,
           exp_w_gate, exp_w_up, exp_w_down):
    bsz, seq, d = x.shape
    t = bsz * seq
    x2d = x.reshape(t, d)
    mw = M_HEADS * M_HEAD_DIM

    w_in = w_in_even[0]
    gate0 = 4 * mw
    w_main = jnp.concatenate([w_in[:, :gate0], w_in[:, gate0 + 2 * M_HEADS:]], axis=1).astype(BF16)
    w_gates = w_in[:, gate0:gate0 + 2 * M_HEADS]
    wg_col = jnp.zeros((d, LANES), F32).at[:, :2 * M_HEADS].set(w_gates).astype(BF16)
    wg_row = w_gates.T.astype(BF16)
    gate_b = jnp.concatenate([m_igate_b[0], m_fgate_b[0]]).astype(F32)
    b_col = jnp.zeros((1, LANES), F32).at[0, :2 * M_HEADS].set(gate_b)
    b_row = gate_b.reshape(2 * M_HEADS, 1)
    triu = jnp.triu(jnp.ones((ROW_TILE, ROW_TILE), F32)).astype(BF16)

    p, gc, gr = _inproj_even(x2d, norm_mix[0].reshape(1, d), w_main, wg_col, wg_row)
    p3 = p.reshape(bsz, seq, -1)
    hm = _mlstm(p3, gc.reshape(bsz, seq, LANES), gr, b_col, b_row, m_head_norm[0].reshape(1, mw))

    lam_init = 0.8 - 0.6 * math.exp(-0.3 * 0)
    lam = (jnp.exp(jnp.sum(lam_q1[0].astype(F32) * lam_k1[0].astype(F32)))
           - jnp.exp(jnp.sum(lam_q2[0].astype(F32) * lam_k2[0].astype(F32))) + lam_init).reshape(1)
    ha = _attention(lam, p3, _bias_tiles(rel_bias.astype(F32), ATT_BLOCK),
                    da_subln[0].reshape(1, DA_V_DIM).astype(F32), 1.0 - lam_init)

    wr0, br0 = _router_weights(router_group_w[0], router_group_b[0], router_expert_w[0], router_expert_b[0])
    x1, h2, meta, cnt = _outproj_even(hm.reshape(t, mw), ha.reshape(t, -1), x2d, w_out_even[0].astype(BF16),
                                      norm_ffn[0].reshape(1, d), wr0, br0, triu)
    ff = exp_w_gate.shape[-1]
    experts = (exp_w_gate.reshape(-1, d, ff), exp_w_up.reshape(-1, d, ff), exp_w_down.reshape(-1, ff, d))
    y = _moe_layer(h2, meta, cnt, 0, *experts)

    wr1, br1 = _router_weights(router_group_w[1], router_group_b[1], router_expert_w[1], router_expert_b[1])
    cw = jnp.zeros((SUBLANES, d), F32).at[:conv_w.shape[1]].set(conv_w[0])
    x3, h2, meta, cnt = _conv_mixer(x1, y, norm_mix[1].reshape(1, d), w_in_odd[0].astype(BF16), cw,
                                    w_out_odd[0].astype(BF16), norm_ffn[1].reshape(1, d), wr1, br1, triu, seq)
    y = _moe_layer(h2, meta, cnt, 1, *experts)

    return _final(x3, y, norm_final.reshape(1, d)).reshape(bsz, seq, d)
```

```python
import functools
import math

import jax
import jax.numpy as jnp
import numpy as np
from jax import lax
from jax.experimental import pallas as pl
from jax.experimental.pallas import tpu as pltpu

F32 = jnp.float32
BF16 = jnp.bfloat16

NORM_EPS = 1e-6
NEG_BIG = -1e30

M_HEADS = 4
M_HEAD_DIM = 128
DA_HEADS = 4
DA_HEAD_DIM = 64
DA_V_DIM = 128
REL_BUCKETS = 32
REL_MAX_DIST = 128
MOE_GROUPS = 4
EXPERTS_PER_GROUP = 4
N_EXPERTS = 16
N_PAIRS = 6
N_BUCKETS = MOE_GROUPS * N_PAIRS

LANES = 128
SUBLANES = 8
VMEM_LIMIT = 56 * 1024 * 1024

ROW_TILE = 512
MLSTM_CHUNK = 256
ATT_BLOCK = 256
MOE_TILE = 512

_NT = (((1,), (1,)), ((), ()))


def _cparams(*sem):
    return pltpu.CompilerParams(dimension_semantics=sem, vmem_limit_bytes=VMEM_LIMIT)


def _rms(x, g):
    ms = jnp.mean(x * x, axis=-1, keepdims=True)
    return x * lax.rsqrt(ms + NORM_EPS) * g


def _dot(a, b):
    return jnp.dot(a, b, preferred_element_type=F32)


def _split3(x):
    hi = x.astype(BF16)
    r1 = x - hi.astype(F32)
    mid = r1.astype(BF16)
    lo = (r1 - mid.astype(F32)).astype(BF16)
    return hi, mid, lo


def _log_sigmoid(x):
    return jnp.minimum(x, 0.0) - jnp.log(1.0 + jnp.exp(-jnp.abs(x)))


def _bias_tiles_kernel(rb_ref, out_ref):
    h = pl.program_id(0)
    blk = out_ref.shape[-1]
    r = lax.broadcasted_iota(jnp.int32, (blk, blk), 0)
    c = lax.broadcasted_iota(jnp.int32, (blk, blk), 1)
    far = rb_ref[REL_BUCKETS - 1, h]
    max_exact = REL_BUCKETS // 2
    for j in range(2):
        rel = c - r - j * blk
        n = jnp.maximum(-rel, 0)
        nf = jnp.maximum(n, 1).astype(F32)
        large = max_exact + (jnp.log(nf / max_exact) / math.log(REL_MAX_DIST / max_exact)
                             * (REL_BUCKETS - max_exact)).astype(jnp.int32)
        large = jnp.minimum(large, REL_BUCKETS - 1)
        bucket = jnp.where(n < max_exact, n, large)
        val = jnp.zeros((blk, blk), F32)
        for b in range(REL_BUCKETS):
            val = jnp.where(bucket == b, rb_ref[b, h], val)
        val = val - far
        if j == 0:
            val = jnp.where(rel <= 0, val, NEG_BIG)
        out_ref[0, j] = val


def _bias_tiles(rel_bias, blk):
    return pl.pallas_call(
        _bias_tiles_kernel,
        out_shape=jax.ShapeDtypeStruct((DA_HEADS, 2, blk, blk), F32),
        grid=(DA_HEADS,),
        in_specs=[pl.BlockSpec(memory_space=pltpu.SMEM)],
        out_specs=pl.BlockSpec((1, 2, blk, blk), lambda h: (h, 0, 0, 0)),
        compiler_params=_cparams("arbitrary"),
        name="bias_tiles",
    )(rel_bias)


def _inproj_even_kernel(x_ref, g_ref, w_ref, wgc_ref, wgr_ref, p_ref, gc_ref, gr_ref):
    h = _rms(x_ref[...], g_ref[...]).astype(BF16)
    n = w_ref.shape[1]
    for j in range(n // 512):
        sl = slice(j * 512, (j + 1) * 512)
        p_ref[:, sl] = _dot(h, w_ref[:, sl]).astype(BF16)
    gc_ref[...] = _dot(h, wgc_ref[...])
    gr_ref[...] = lax.dot_general(wgr_ref[...], h, _NT, preferred_element_type=F32)


def _inproj_even(x2d, g, w_main, wg_col, wg_row):
    t, d = x2d.shape
    n = w_main.shape[1]
    tm = ROW_TILE
    return pl.pallas_call(
        _inproj_even_kernel,
        out_shape=(jax.ShapeDtypeStruct((t, n), BF16),
                   jax.ShapeDtypeStruct((t, LANES), F32),
                   jax.ShapeDtypeStruct((SUBLANES, t), F32)),
        grid=(t // tm,),
        in_specs=[pl.BlockSpec((tm, d), lambda i: (i, 0)),
                  pl.BlockSpec((1, d), lambda i: (0, 0)),
                  pl.BlockSpec((d, n), lambda i: (0, 0)),
                  pl.BlockSpec((d, LANES), lambda i: (0, 0)),
                  pl.BlockSpec((SUBLANES, d), lambda i: (0, 0))],
        out_specs=(pl.BlockSpec((tm, n), lambda i: (i, 0)),
                   pl.BlockSpec((tm, LANES), lambda i: (i, 0)),
                   pl.BlockSpec((SUBLANES, tm), lambda i: (0, i))),
        compiler_params=_cparams("arbitrary"),
        name="inproj_even",
    )(x2d, g, w_main, wg_col, wg_row)


def _mlstm_kernel(q_ref, k_ref, v_ref, og_ref, gc_ref, gr_ref, bc_ref, br_ref, hn_ref, out_ref,
                  c_ref, m_ref):
    L = q_ref.shape[1]
    dh = M_HEAD_DIM
    scale = dh ** -0.5

    @pl.when(pl.program_id(1) == 0)
    def _():
        c_ref[...] = jnp.zeros_like(c_ref)
        m_ref[...] = jnp.zeros_like(m_ref)

    gc = gc_ref[0] + bc_ref[...]
    gr = gr_ref[...] + br_ref[...]
    lf_c = _log_sigmoid(gc)
    lf_r = _log_sigmoid(gr)

    row = lax.broadcasted_iota(jnp.int32, (L, L), 0)
    col = lax.broadcasted_iota(jnp.int32, (L, L), 1)
    causal = row >= col
    tril = jnp.where(causal, 1.0, 0.0).astype(BF16)
    triu = jnp.where(row <= col, 1.0, 0.0).astype(BF16)
    b_c = sum(_dot(tril, part) for part in _split3(lf_c))
    b_r = sum(_dot(part, triu) for part in _split3(lf_r))

    lane = lax.broadcasted_iota(jnp.int32, (L, LANES), 1)
    ones_col = jnp.where(lane == 0, 1.0, 0.0).astype(BF16)

    for h in range(M_HEADS):
        hs = slice(h * dh, (h + 1) * dh)
        q = q_ref[0, :, hs]
        k = k_ref[0, :, hs]
        v = v_ref[0, :, hs]
        i_r = gr[h:h + 1, :]
        b_rh = b_r[M_HEADS + h:M_HEADS + h + 1, :]
        i_c = gc[:, h:h + 1]
        b_ch = b_c[:, M_HEADS + h:M_HEADS + h + 1]
        m_prev = m_ref[h:h + 1, 0:1]

        dmat = jnp.where(causal, b_ch - b_rh + i_r, NEG_BIG)
        inter = b_ch + m_prev
        mt = jnp.maximum(inter, jnp.max(dmat, axis=-1, keepdims=True))
        w = jnp.exp(dmat - mt)
        sc = jnp.exp(inter - mt)
        s = lax.dot_general(q, k, _NT, preferred_element_type=F32)
        qk = (s * (w * scale)).astype(BF16)

        vaug = jnp.concatenate([v, ones_col], axis=-1)
        caug = c_ref[h]
        num_aug = sc * _dot(q, caug.astype(BF16)) + _dot(qk, vaug)
        num = num_aug[:, :dh]
        den = num_aug[:, dh:dh + 1]
        hh = num / jnp.maximum(jnp.abs(den), jnp.exp(-mt))

        mu = jnp.mean(hh, axis=-1, keepdims=True)
        cen = hh - mu
        var = jnp.mean(cen * cen, axis=-1, keepdims=True)
        hn = cen * lax.rsqrt(var + NORM_EPS) * hn_ref[:, hs]
        gate = jax.nn.sigmoid(og_ref[0, :, hs].astype(F32))
        out_ref[0, :, hs] = (hn * gate).astype(out_ref.dtype)

        g_end = b_ch[L - 1:L, :]
        ws = g_end - b_ch + i_c
        m_new = jnp.maximum(g_end + m_prev, jnp.max(ws, axis=0, keepdims=True))
        a = jnp.exp(g_end + m_prev - m_new)
        wk = (jnp.exp(ws - m_new) * scale) * k.astype(F32)
        upd = _dot(wk.T.astype(BF16), vaug)
        c_ref[h] = a * caug + upd
        m_ref[h:h + 1, :] = jnp.broadcast_to(m_new, (1, LANES))


def _mlstm(p3, gc3, gr, b_col, b_row, head_norm):
    bsz, s, _ = p3.shape
    L = MLSTM_CHUNK
    w = M_HEADS * M_HEAD_DIM
    nc = s // L
    blk = lambda j: pl.BlockSpec((1, L, w), lambda b, c, j=j: (b, c, j))
    return pl.pallas_call(
        _mlstm_kernel,
        out_shape=jax.ShapeDtypeStruct((bsz, s, w), BF16),
        grid=(bsz, nc),
        in_specs=[blk(0), blk(1), blk(2), blk(3),
                  pl.BlockSpec((1, L, LANES), lambda b, c: (b, c, 0)),
                  pl.BlockSpec((SUBLANES, L), lambda b, c: (0, b * nc + c)),
                  pl.BlockSpec((1, LANES), lambda b, c: (0, 0)),
                  pl.BlockSpec((SUBLANES, 1), lambda b, c: (0, 0)),
                  pl.BlockSpec((1, w), lambda b, c: (0, 0))],
        out_specs=pl.BlockSpec((1, L, w), lambda b, c: (b, c, 0)),
        scratch_shapes=[pltpu.VMEM((M_HEADS, M_HEAD_DIM, 2 * M_HEAD_DIM), F32),
                        pltpu.VMEM((SUBLANES, LANES), F32)],
        compiler_params=_cparams("arbitrary", "arbitrary"),
        name="mlstm",
    )(p3, p3, p3, p3, gc3, gr, b_col, b_row, head_norm)


def _attn_kernel(lam_ref, q_ref, k_ref, v_ref, bias_ref, sub_ref, out_ref, vaug_ref, *, out_scale, blk):
    s_len = q_ref.shape[1]
    d = DA_HEAD_DIM
    lane_s = lax.broadcasted_iota(jnp.int32, (s_len, DA_V_DIM), 1)
    vaug_ref[:, :DA_V_DIM] = v_ref[0]
    vaug_ref[:, DA_V_DIM:] = jnp.where(lane_s == 0, 1.0, 0.0).astype(BF16)
    lam = lam_ref[0]
    lane = lax.broadcasted_iota(jnp.int32, (blk, LANES), 1)
    for qi in range(s_len // blk):
        q0, q1 = qi * blk, (qi + 1) * blk
        qs = q_ref[0, q0:q1, :] * (d ** -0.5)
        zero = jnp.zeros_like(qs)
        heads = []
        for qm in (jnp.where(lane < d, qs, zero), jnp.where(lane >= d, qs, zero)):
            score = lambda a, b: lax.dot_general(qm, k_ref[0, a:b, :], _NT, preferred_element_type=F32)
            parts = [score(q0, q1) + bias_ref[0, 0]]
            if qi >= 1:
                parts.append(score(q0 - blk, q0) + bias_ref[0, 1])
            if qi >= 2:
                parts.append(score(0, q0 - blk))
            m = functools.reduce(jnp.maximum, [jnp.max(p, axis=-1, keepdims=True) for p in parts])
            pexp = [jnp.exp(p - m).astype(BF16) for p in parts]
            o = _dot(pexp[0], vaug_ref[q0:q1, :])
            if qi >= 1:
                o = o + _dot(pexp[1], vaug_ref[q0 - blk:q0, :])
            if qi >= 2:
                o = o + _dot(pexp[2], vaug_ref[0:q0 - blk, :])
            heads.append(o[:, :DA_V_DIM] / o[:, DA_V_DIM:DA_V_DIM + 1])
        o = heads[0] - lam * heads[1]
        ms = jnp.mean(o * o, axis=-1, keepdims=True)
        o = o * lax.rsqrt(ms + NORM_EPS) * sub_ref[...] * out_scale
        out_ref[0, q0:q1, :] = o.astype(out_ref.dtype)


def _attention(lam, p3, bias_tiles, subln, out_scale):
    bsz, s, _ = p3.shape
    blk = ATT_BLOCK
    w = DA_HEADS * DA_V_DIM
    q_blk = 4 * M_HEADS
    k_blk = q_blk + DA_HEADS
    v_blk = k_blk + DA_HEADS
    col = lambda base: pl.BlockSpec((1, s, LANES), lambda b, h: (b, 0, base + h))
    return pl.pallas_call(
        functools.partial(_attn_kernel, out_scale=out_scale, blk=blk),
        out_shape=jax.ShapeDtypeStruct((bsz, s, w), BF16),
        grid=(bsz, DA_HEADS),
        in_specs=[pl.BlockSpec(memory_space=pltpu.SMEM),
                  col(q_blk), col(k_blk), col(v_blk),
                  pl.BlockSpec((1, 2, blk, blk), lambda b, h: (h, 0, 0, 0)),
                  pl.BlockSpec((1, DA_V_DIM), lambda b, h: (0, 0))],
        out_specs=pl.BlockSpec((1, s, DA_V_DIM), lambda b, h: (b, 0, h)),
        scratch_shapes=[pltpu.VMEM((s, 2 * DA_V_DIM), BF16)],
        compiler_params=_cparams("arbitrary", "arbitrary"),
        name="diff_attention",
    )(lam, p3, p3, p3, bias_tiles, subln)


def _pack_bf16_pairs(v):
    half = v.shape[1] // 2
    hi = lax.bitcast_convert_type(v[:, :half].astype(F32), jnp.uint32)
    lo = lax.bitcast_convert_type(v[:, half:].astype(F32), jnp.uint32)
    return hi | (lo >> 16)


def _unpack_bf16_pairs(w):
    hi = lax.bitcast_convert_type(w & jnp.uint32(0xFFFF0000), F32)
    lo = lax.bitcast_convert_type(w << 16, F32)
    return hi, lo


def _router_epilogue(xn, g_ref, wr_ref, br_ref, triu_ref, hrow_ref, route_ref, cnt_ref):
    tm, d = xn.shape
    h2 = _rms(xn, g_ref[...]).astype(BF16)
    hrow_ref[:, :d // 2] = _pack_bf16_pairs(h2)

    lg = lax.dot_general(wr_ref[...], h2, _NT, preferred_element_type=F32) + br_ref[...]
    gl = [lg[g:g + 1, :] for g in range(MOE_GROUPS)]
    gmax = functools.reduce(jnp.maximum, gl)
    gsel = jnp.where(gl[0] == gmax, 0.0, jnp.where(gl[1] == gmax, 1.0, jnp.where(gl[2] == gmax, 2.0, 3.0)))
    pg = 1.0 / functools.reduce(lambda a, b: a + b, [jnp.exp(x - gmax) for x in gl])

    ev = []
    for j in range(EXPERTS_PER_GROUP):
        acc = jnp.zeros_like(gmax)
        for g in range(MOE_GROUPS):
            r0 = SUBLANES + g * EXPERTS_PER_GROUP + j
            acc = jnp.where(gsel == float(g), lg[r0:r0 + 1, :], acc)
        ev.append(acc)
    v1 = functools.reduce(jnp.maximum, ev)
    i1 = jnp.where(ev[0] == v1, 0.0, jnp.where(ev[1] == v1, 1.0, jnp.where(ev[2] == v1, 2.0, 3.0)))
    ev2 = [jnp.where(i1 == float(j), -jnp.inf, ev[j]) for j in range(EXPERTS_PER_GROUP)]
    v2 = functools.reduce(jnp.maximum, ev2)
    hit = [(ev2[j] == v2) & (i1 != float(j)) for j in range(EXPERTS_PER_GROUP)]
    i2 = jnp.where(hit[0], 0.0, jnp.where(hit[1], 1.0, jnp.where(hit[2], 2.0, 3.0)))
    e21 = jnp.exp(v2 - v1)
    w1 = 1.0 / (1.0 + e21)
    w2 = e21 * w1
    first_low = i1 < i2
    lo = jnp.minimum(i1, i2)
    hi = jnp.maximum(i1, i2)
    c_lo = pg * jnp.where(first_low, w1, w2)
    c_hi = pg * jnp.where(first_low, w2, w1)
    pair = lo * (7.0 - lo) * 0.5 + hi - lo - 1.0
    bucket = gsel * float(N_PAIRS) + pair

    @pl.when(pl.program_id(0) == 0)
    def _():
        cnt_ref[...] = jnp.zeros_like(cnt_ref)

    rows = lax.broadcasted_iota(jnp.int32, (32, tm), 0).astype(F32)
    onehot = rows == bucket
    cum = _dot(jnp.where(onehot, 1.0, 0.0).astype(BF16), triu_ref[...])
    carry = cnt_ref[:, 0:1]
    rank = jnp.sum(jnp.where(onehot, cum + carry, 0.0), axis=0, keepdims=True) - 1.0
    cnt_ref[...] = jnp.broadcast_to(carry + cum[:, tm - 1:tm], cnt_ref.shape)

    rid = lax.broadcasted_iota(jnp.int32, (LANES, tm), 0)
    rec = jnp.where(rid == 0, bucket, jnp.where(rid == 1, c_lo, jnp.where(rid == 2, c_hi,
                    jnp.where(rid == 3, rank, 0.0))))
    hrow_ref[:, d // 2:] = lax.bitcast_convert_type(rec.T, jnp.uint32)
    route_ref[...] = rec[:SUBLANES, :]


_EPI_OUT_SHAPES = lambda t, d: (jax.ShapeDtypeStruct((t, d), F32),
                                jax.ShapeDtypeStruct((t, d // 2 + LANES), jnp.uint32),
                                jax.ShapeDtypeStruct((SUBLANES, t), F32),
                                jax.ShapeDtypeStruct((32, LANES), F32))


def _epi_out_specs(tm, d):
    return (pl.BlockSpec((tm, d), lambda i: (i, 0)),
            pl.BlockSpec((tm, d // 2 + LANES), lambda i: (i, 0)),
            pl.BlockSpec((SUBLANES, tm), lambda i: (0, i)),
            pl.BlockSpec((32, LANES), lambda i: (0, 0)))


def _epi_in_specs(tm, d):
    return [pl.BlockSpec((1, d), lambda i: (0, 0)),
            pl.BlockSpec((32, d), lambda i: (0, 0)),
            pl.BlockSpec((32, 1), lambda i: (0, 0)),
            pl.BlockSpec((tm, tm), lambda i: (0, 0))]


def _outproj_even_kernel(hm_ref, ha_ref, x_ref, wo_ref, g_ref, wr_ref, br_ref, triu_ref,
                         xn_ref, h2_ref, meta_ref, cnt_ref):
    half = hm_ref.shape[1]
    y = _dot(hm_ref[...], wo_ref[:half, :]) + _dot(ha_ref[...], wo_ref[half:, :])
    xn = x_ref[...] + y
    xn_ref[...] = xn
    _router_epilogue(xn, g_ref, wr_ref, br_ref, triu_ref, h2_ref, meta_ref, cnt_ref)


def _outproj_even(hm, ha, x2d, w_out, g, wr, br, triu):
    t, d = x2d.shape
    tm = ROW_TILE
    half = hm.shape[1]
    return pl.pallas_call(
        _outproj_even_kernel,
        out_shape=_EPI_OUT_SHAPES(t, d),
        grid=(t // tm,),
        in_specs=[pl.BlockSpec((tm, half), lambda i: (i, 0)),
                  pl.BlockSpec((tm, half), lambda i: (i, 0)),
                  pl.BlockSpec((tm, d), lambda i: (i, 0)),
                  pl.BlockSpec((2 * half, d), lambda i: (0, 0))] + _epi_in_specs(tm, d),
        out_specs=_epi_out_specs(tm, d),
        compiler_params=_cparams("arbitrary"),
        name="outproj_even",
    )(hm, ha, x2d, w_out, g, wr, br, triu)


def _conv_mixer_kernel(x_ref, y_ref, gm_ref, wi_ref, cw_ref, wo_ref, g_ref, wr_ref, br_ref, triu_ref,
                       xn_ref, h2_ref, meta_ref, cnt_ref, tail_ref, *, tiles_per_seq):
    tm, d = x_ref.shape
    x = x_ref[...] + jnp.concatenate(_unpack_bf16_pairs(y_ref[...]), axis=1)
    h = _rms(x, gm_ref[...]).astype(BF16)

    @pl.when(pl.program_id(0) % tiles_per_seq == 0)
    def _():
        tail_ref[...] = jnp.zeros_like(tail_ref)

    row = lax.broadcasted_iota(jnp.int32, (tm, 512), 0)
    acc = jnp.zeros((tm, d), F32)
    for j in range(d // 512):
        sl = slice(j * 512, (j + 1) * 512)
        bg = _dot(h, wi_ref[:, sl])
        cg = _dot(h, wi_ref[:, d + j * 512:d + (j + 1) * 512])
        u = _dot(h, wi_ref[:, 2 * d + j * 512:2 * d + (j + 1) * 512])
        z = cg * u
        prev1 = tail_ref[SUBLANES - 1:SUBLANES, sl]
        prev2 = tail_ref[SUBLANES - 2:SUBLANES - 1, sl]
        z1 = jnp.where(row == 0, prev1, pltpu.roll(z, 1, axis=0))
        z2 = jnp.where(row == 0, prev2, jnp.where(row == 1, prev1, pltpu.roll(z, 2, axis=0)))
        zc = cw_ref[0:1, sl] * z2 + cw_ref[1:2, sl] * z1 + cw_ref[2:3, sl] * z
        tail_ref[:, sl] = z[tm - SUBLANES:, :]
        acc = acc + _dot((bg * zc).astype(BF16), wo_ref[sl, :])
    xn = x + acc
    xn_ref[...] = xn
    _router_epilogue(xn, g_ref, wr_ref, br_ref, triu_ref, h2_ref, meta_ref, cnt_ref)


def _conv_mixer(x2d, ymoe, gm, w_in, conv_w, w_out, g, wr, br, triu, seq):
    t, d = x2d.shape
    tm = ROW_TILE
    return pl.pallas_call(
        functools.partial(_conv_mixer_kernel, tiles_per_seq=seq // tm),
        out_shape=_EPI_OUT_SHAPES(t, d),
        grid=(t // tm,),
        in_specs=[pl.BlockSpec((tm, d), lambda i: (i, 0)),
                  pl.BlockSpec((tm, d // 2), lambda i: (i, 0)),
                  pl.BlockSpec((1, d), lambda i: (0, 0)),
                  pl.BlockSpec((d, 3 * d), lambda i: (0, 0)),
                  pl.BlockSpec((SUBLANES, d), lambda i: (0, 0)),
                  pl.BlockSpec((d, d), lambda i: (0, 0))] + _epi_in_specs(tm, d),
        out_specs=_epi_out_specs(tm, d),
        scratch_shapes=[pltpu.VMEM((SUBLANES, d), F32)],
        compiler_params=_cparams("arbitrary"),
        name="conv_mixer",
    )(x2d, ymoe, gm, w_in, conv_w, w_out, g, wr, br, triu)


def _moe_kernel(elo_ref, ehi_ref, valid_ref, fresh_ref,
                src_ref, src_next_ref, dst_prev_ref,
                h_hbm,
                wg_lo, wu_lo, wd_lo, wg_hi, wu_hi, wd_hi,
                y_hbm,
                xbuf0, xbuf1, obuf0, obuf1, zbuf, wbuf_in, wbuf_out, gsem, ssem, zsem):
    i = pl.program_id(0)
    n = pl.num_programs(0)
    tm, half = obuf0.shape
    xbufs = (xbuf0, xbuf1)
    obufs = (obuf0, obuf1)
    valid = valid_ref[i] != 0
    prev_valid = valid_ref[jnp.maximum(i - 1, 0)] != 0

    def row_gather(idx_ref, r, s):
        return pltpu.make_async_copy(h_hbm.at[pl.ds(idx_ref[0, 0, r], 1)], xbufs[s].at[pl.ds(r, 1)], gsem.at[s])

    def row_scatter(r, s):
        return pltpu.make_async_copy(obufs[s].at[pl.ds(r, 1)], y_hbm.at[pl.ds(dst_prev_ref[0, 0, r], 1)], ssem.at[s])

    wait_gather = lambda s: pltpu.make_async_copy(h_hbm.at[pl.ds(0, tm)], xbufs[s], gsem.at[s]).wait()
    wait_scatter = lambda s: pltpu.make_async_copy(obufs[s], y_hbm.at[pl.ds(0, tm)], ssem.at[s]).wait()

    @pl.when(i == 0)
    def _():
        obuf1[...] = jnp.zeros_like(obuf1)
        zbuf[...] = jnp.zeros_like(zbuf)

        @pl.loop(0, tm)
        def _(r):
            row_gather(src_ref, r, 0).start()

    @pl.when(fresh_ref[i] != 0)
    def _():
        for k, w in enumerate((wg_lo, wu_lo, wg_hi, wu_hi)):
            wbuf_in[k] = w[0].astype(BF16)
        for k, w in enumerate((wd_lo, wd_hi)):
            wbuf_out[k] = w[0].astype(BF16)

    for s in range(2):
        @pl.when(valid & (i % 2 == s))
        def _(s=s):
            wait_gather(s)
            for r in range(tm):
                row_gather(src_next_ref, r, 1 - s).start()
            for r in range(tm):
                row_scatter(r, 1 - s).start()
            rows = xbufs[s][...]
            xa, xb = _unpack_bf16_pairs(rows[:, :half])
            x = jnp.concatenate([xa.astype(BF16), xb.astype(BF16)], axis=1)
            rec = lax.bitcast_convert_type(rows[:, half:], F32)

            def ffn(k):
                a = _dot(x, wbuf_in[2 * k])
                a = (a * jax.nn.sigmoid(a)) * _dot(x, wbuf_in[2 * k + 1])
                return _dot(a.astype(BF16), wbuf_out[k])

            out = rec[:, 1:2] * ffn(0) + rec[:, 2:3] * ffn(1)
            obufs[s][...] = _pack_bf16_pairs(out.astype(BF16))
            wait_scatter(1 - s)

        @pl.when(jnp.logical_not(valid) & prev_valid & (i % 2 == s))
        def _(s=s):
            wait_gather(s)

            @pl.loop(0, tm)
            def _(r):
                row_scatter(r, 1 - s).start()

            wait_scatter(1 - s)

    @pl.when(jnp.logical_not(valid) & (i < n - 1))
    def _():
        fill = pltpu.make_async_copy(zbuf, y_hbm.at[pl.ds(pl.multiple_of(i * tm, tm), tm)], zsem)
        fill.start()
        fill.wait()


def _moe(tiles, src_rows, dst_rows, hrows, w_gate, w_up, w_down):
    tile_elo, tile_ehi, tile_valid, tile_fresh = tiles
    n_tiles = tile_elo.shape[0]
    t, wrow = hrows.shape
    d = w_gate.shape[1]
    ff = w_gate.shape[2]
    tm = MOE_TILE
    lo = lambda i, elo, *_: (elo[i], 0, 0)
    hi = lambda i, elo, ehi, *_: (ehi[i], 0, 0)
    cur = lambda i, *_: (i, 0, 0)
    nxt = lambda i, *_: ((i + 1) % n_tiles, 0, 0)
    prv = lambda i, *_: ((i + n_tiles - 1) % n_tiles, 0, 0)
    idx = lambda m: pl.BlockSpec((1, 1, tm), m, memory_space=pltpu.SMEM)
    return pl.pallas_call(
        _moe_kernel,
        out_shape=jax.ShapeDtypeStruct((n_tiles * tm, d // 2), jnp.uint32),
        grid_spec=pltpu.PrefetchScalarGridSpec(
            num_scalar_prefetch=4,
            grid=(n_tiles,),
            in_specs=[idx(cur), idx(nxt), idx(prv),
                      pl.BlockSpec(memory_space=pl.ANY),
                      pl.BlockSpec((1, d, ff), lo), pl.BlockSpec((1, d, ff), lo), pl.BlockSpec((1, ff, d), lo),
                      pl.BlockSpec((1, d, ff), hi), pl.BlockSpec((1, d, ff), hi), pl.BlockSpec((1, ff, d), hi)],
            out_specs=pl.BlockSpec(memory_space=pl.ANY),
            scratch_shapes=[pltpu.VMEM((tm, wrow), jnp.uint32),
                            pltpu.VMEM((tm, wrow), jnp.uint32),
                            pltpu.VMEM((tm, d // 2), jnp.uint32),
                            pltpu.VMEM((tm, d // 2), jnp.uint32),
                            pltpu.VMEM((tm, d // 2), jnp.uint32),
                            pltpu.VMEM((4, d, ff), BF16),
                            pltpu.VMEM((2, ff, d), BF16),
                            pltpu.SemaphoreType.DMA((2,)),
                            pltpu.SemaphoreType.DMA((2,)),
                            pltpu.SemaphoreType.DMA]),
        compiler_params=_cparams("arbitrary"),
        name="grouped_moe",
    )(tile_elo, tile_ehi, tile_valid, tile_fresh, src_rows, src_rows, dst_rows, hrows,
      w_gate, w_up, w_down, w_gate, w_up, w_down)


def _final_kernel(x_ref, y_ref, g_ref, out_ref):
    out_ref[...] = _rms(x_ref[...] + jnp.concatenate(_unpack_bf16_pairs(y_ref[...]), axis=1), g_ref[...])


def _final(x2d, ymoe, g):
    t, d = x2d.shape
    tm = ROW_TILE
    return pl.pallas_call(
        _final_kernel,
        out_shape=jax.ShapeDtypeStruct((t, d), F32),
        grid=(t // tm,),
        in_specs=[pl.BlockSpec((tm, d), lambda i: (i, 0)),
                  pl.BlockSpec((tm, d // 2), lambda i: (i, 0)),
                  pl.BlockSpec((1, d), lambda i: (0, 0))],
        out_specs=pl.BlockSpec((tm, d), lambda i: (i, 0)),
        compiler_params=_cparams("arbitrary"),
        name="final_norm",
    )(x2d, ymoe, g)


_PAIRS = [(a, b) for a in range(EXPERTS_PER_GROUP) for b in range(a + 1, EXPERTS_PER_GROUP)]
_BUCKET_ELO = np.array([g * EXPERTS_PER_GROUP + _PAIRS[p][0] for g in range(MOE_GROUPS) for p in range(N_PAIRS)], np.int32)
_BUCKET_EHI = np.array([g * EXPERTS_PER_GROUP + _PAIRS[p][1] for g in range(MOE_GROUPS) for p in range(N_PAIRS)], np.int32)


def _moe_layer(hrows, route, cnt, layer, w_gate, w_up, w_down):
    t = hrows.shape[0]
    tm = MOE_TILE
    n_tiles = t // tm + N_BUCKETS
    tp = n_tiles * tm
    bucket = route[0].astype(jnp.int32)
    rank = route[3].astype(jnp.int32)
    counts = cnt[:N_BUCKETS, 0].astype(jnp.int32)
    padded = ((counts + tm - 1) // tm) * tm
    ends = jnp.cumsum(padded)
    dest = (ends - padded)[bucket] + rank
    tile_start = jnp.arange(n_tiles, dtype=jnp.int32) * tm
    tile_valid = tile_start < ends[-1]
    last_start = jnp.maximum(ends[-1] - tm, 0)
    tile_bucket = jnp.sum((ends[None, :] <= jnp.minimum(tile_start, last_start)[:, None]).astype(jnp.int32), axis=1)
    prev_bucket = jnp.concatenate([jnp.full((1,), -1, jnp.int32), tile_bucket[:-1]])
    tile_fresh = tile_valid & (tile_bucket != prev_bucket)
    base = layer * N_EXPERTS
    token_at = jnp.full((tp,), -1, jnp.int32).at[dest].set(jnp.arange(t, dtype=jnp.int32))
    is_pad = token_at < 0
    src_rows = jnp.where(is_pad, 0, token_at)
    dst_rows = jnp.where(is_pad, t - 1 + jnp.cumsum(is_pad.astype(jnp.int32)), token_at)
    tiles = (jnp.asarray(_BUCKET_ELO)[tile_bucket] + base, jnp.asarray(_BUCKET_EHI)[tile_bucket] + base,
             tile_valid.astype(jnp.int32), tile_fresh.astype(jnp.int32))
    shape = (n_tiles, 1, tm)
    return _moe(tiles, src_rows.reshape(shape), dst_rows.reshape(shape), hrows, w_gate, w_up, w_down)


def _router_weights(wg, bg, we, be):
    d = wg.shape[0]
    wr = jnp.zeros((32, d), F32).at[:MOE_GROUPS].set(wg.T).at[SUBLANES:SUBLANES + N_EXPERTS].set(we.T)
    br = jnp.zeros((32, 1), F32).at[:MOE_GROUPS, 0].set(bg).at[SUBLANES:SUBLANES + N_EXPERTS, 0].set(be)
    return wr.astype(BF16), br


def kernel(x, rel_bias, norm_mix, norm_ffn, norm_final, w_in_even, w_out_even, m_igate_b, m_fgate_b,
           m_head_norm, lam_q1, lam_k1, lam_q2, lam_k2, da_subln, w_in_odd, conv_w, w_out_odd,
           router_group_w, router_group_b, router_expert_w, router_expert_b,
           exp_w_gate, exp_w_up, exp_w_down):
    bsz, seq, d = x.shape
    t = bsz * seq
    x2d = x.reshape(t, d)
    mw = M_HEADS * M_HEAD_DIM

    w_in = w_in_even[0]
    gate0 = 4 * mw
    w_main = jnp.concatenate([w_in[:, :gate0], w_in[:, gate0 + 2 * M_HEADS:]], axis=1).astype(BF16)
    w_gates = w_in[:, gate0:gate0 + 2 * M_HEADS]
    wg_col = jnp.zeros((d, LANES), F32).at[:, :2 * M_HEADS].set(w_gates).astype(BF16)
    wg_row = w_gates.T.astype(BF16)
    gate_b = jnp.concatenate([m_igate_b[0], m_fgate_b[0]]).astype(F32)
    b_col = jnp.zeros((1, LANES), F32).at[0, :2 * M_HEADS].set(gate_b)
    b_row = gate_b.reshape(2 * M_HEADS, 1)
    triu = jnp.triu(jnp.ones((ROW_TILE, ROW_TILE), F32)).astype(BF16)

    p, gc, gr = _inproj_even(x2d, norm_mix[0].reshape(1, d), w_main, wg_col, wg_row)
    p3 = p.reshape(bsz, seq, -1)
    hm = _mlstm(p3, gc.reshape(bsz, seq, LANES), gr, b_col, b_row, m_head_norm[0].reshape(1, mw))

    lam_init = 0.8 - 0.6 * math.exp(-0.3 * 0)
    lam = (jnp.exp(jnp.sum(lam_q1[0].astype(F32) * lam_k1[0].astype(F32)))
           - jnp.exp(jnp.sum(lam_q2[0].astype(F32) * lam_k2[0].astype(F32))) + lam_init).reshape(1)
    ha = _attention(lam, p3, _bias_tiles(rel_bias.astype(F32), ATT_BLOCK),
                    da_subln[0].reshape(1, DA_V_DIM).astype(F32), 1.0 - lam_init)

    wr0, br0 = _router_weights(router_group_w[0], router_group_b[0], router_expert_w[0], router_expert_b[0])
    x1, h2, meta, cnt = _outproj_even(hm.reshape(t, mw), ha.reshape(t, -1), x2d, w_out_even[0].astype(BF16),
                                      norm_ffn[0].reshape(1, d), wr0, br0, triu)
    ff = exp_w_gate.shape[-1]
    experts = (exp_w_gate.reshape(-1, d, ff), exp_w_up.reshape(-1, d, ff), exp_w_down.reshape(-1, ff, d))
    y = _moe_layer(h2, meta, cnt, 0, *experts)

    wr1, br1 = _router_weights(router_group_w[1], router_group_b[1], router_expert_w[1], router_expert_b[1])
    cw = jnp.zeros((SUBLANES, d), F32).at[:conv_w.shape[1]].set(conv_w[0])
    x3, h2, meta, cnt = _conv_mixer(x1, y, norm_mix[1].reshape(1, d), w_in_odd[0].astype(BF16), cw,
                                    w_out_odd[0].astype(BF16), norm_ffn[1].reshape(1, d), wr1, br1, triu, seq)
    y = _moe_layer(h2, meta, cnt, 1, *experts)

    return _final(x3, y, norm_final.reshape(1, d)).reshape(bsz, seq, d)
```

```python
import functools
import math

import jax
import jax.numpy as jnp
import numpy as np
from jax import lax
from jax.experimental import pallas as pl
from jax.experimental.pallas import tpu as pltpu

F32 = jnp.float32
BF16 = jnp.bfloat16

NORM_EPS = 1e-6
NEG_BIG = -1e30

M_HEADS = 4
M_HEAD_DIM = 128
DA_HEADS = 4
DA_HEAD_DIM = 64
DA_V_DIM = 128
REL_BUCKETS = 32
REL_MAX_DIST = 128
MOE_GROUPS = 4
EXPERTS_PER_GROUP = 4
N_EXPERTS = 16
N_PAIRS = 6
N_BUCKETS = MOE_GROUPS * N_PAIRS

LANES = 128
SUBLANES = 8
VMEM_LIMIT = 56 * 1024 * 1024

ROW_TILE = 512
MLSTM_CHUNK = 256
ATT_BLOCK = 256
MOE_TILE = 512

_NT = (((1,), (1,)), ((), ()))


def _cparams(*sem):
    return pltpu.CompilerParams(dimension_semantics=sem, vmem_limit_bytes=VMEM_LIMIT)


def _rms(x, g):
    ms = jnp.mean(x * x, axis=-1, keepdims=True)
    return x * lax.rsqrt(ms + NORM_EPS) * g


def _dot(a, b):
    return jnp.dot(a, b, preferred_element_type=F32)


def _split3(x):
    hi = x.astype(BF16)
    r1 = x - hi.astype(F32)
    mid = r1.astype(BF16)
    lo = (r1 - mid.astype(F32)).astype(BF16)
    return hi, mid, lo


def _log_sigmoid(x):
    return jnp.minimum(x, 0.0) - jnp.log(1.0 + jnp.exp(-jnp.abs(x)))


def _bias_tiles_kernel(rb_ref, out_ref):
    h = pl.program_id(0)
    blk = out_ref.shape[-1]
    r = lax.broadcasted_iota(jnp.int32, (blk, blk), 0)
    c = lax.broadcasted_iota(jnp.int32, (blk, blk), 1)
    far = rb_ref[REL_BUCKETS - 1, h]
    max_exact = REL_BUCKETS // 2
    for j in range(2):
        rel = c - r - j * blk
        n = jnp.maximum(-rel, 0)
        nf = jnp.maximum(n, 1).astype(F32)
        large = max_exact + (jnp.log(nf / max_exact) / math.log(REL_MAX_DIST / max_exact)
                             * (REL_BUCKETS - max_exact)).astype(jnp.int32)
        large = jnp.minimum(large, REL_BUCKETS - 1)
        bucket = jnp.where(n < max_exact, n, large)
        val = jnp.zeros((blk, blk), F32)
        for b in range(REL_BUCKETS):
            val = jnp.where(bucket == b, rb_ref[b, h], val)
        val = val - far
        if j == 0:
            val = jnp.where(rel <= 0, val, NEG_BIG)
        out_ref[0, j] = val


def _bias_tiles(rel_bias, blk):
    return pl.pallas_call(
        _bias_tiles_kernel,
        out_shape=jax.ShapeDtypeStruct((DA_HEADS, 2, blk, blk), F32),
        grid=(DA_HEADS,),
        in_specs=[pl.BlockSpec(memory_space=pltpu.SMEM)],
        out_specs=pl.BlockSpec((1, 2, blk, blk), lambda h: (h, 0, 0, 0)),
        compiler_params=_cparams("arbitrary"),
        name="bias_tiles",
    )(rel_bias)


def _inproj_even_kernel(x_ref, g_ref, w_ref, wgc_ref, wgr_ref, p_ref, gc_ref, gr_ref):
    h = _rms(x_ref[...], g_ref[...]).astype(BF16)
    n = w_ref.shape[1]
    for j in range(n // 512):
        sl = slice(j * 512, (j + 1) * 512)
        p_ref[:, sl] = _dot(h, w_ref[:, sl]).astype(BF16)
    gc_ref[...] = _dot(h, wgc_ref[...])
    gr_ref[...] = lax.dot_general(wgr_ref[...], h, _NT, preferred_element_type=F32)


def _inproj_even(x2d, g, w_main, wg_col, wg_row):
    t, d = x2d.shape
    n = w_main.shape[1]
    tm = ROW_TILE
    return pl.pallas_call(
        _inproj_even_kernel,
        out_shape=(jax.ShapeDtypeStruct((t, n), BF16),
                   jax.ShapeDtypeStruct((t, LANES), F32),
                   jax.ShapeDtypeStruct((SUBLANES, t), F32)),
        grid=(t // tm,),
        in_specs=[pl.BlockSpec((tm, d), lambda i: (i, 0)),
                  pl.BlockSpec((1, d), lambda i: (0, 0)),
                  pl.BlockSpec((d, n), lambda i: (0, 0)),
                  pl.BlockSpec((d, LANES), lambda i: (0, 0)),
                  pl.BlockSpec((SUBLANES, d), lambda i: (0, 0))],
        out_specs=(pl.BlockSpec((tm, n), lambda i: (i, 0)),
                   pl.BlockSpec((tm, LANES), lambda i: (i, 0)),
                   pl.BlockSpec((SUBLANES, tm), lambda i: (0, i))),
        compiler_params=_cparams("arbitrary"),
        name="inproj_even",
    )(x2d, g, w_main, wg_col, wg_row)


def _mlstm_kernel(q_ref, k_ref, v_ref, og_ref, gc_ref, gr_ref, bc_ref, br_ref, hn_ref, out_ref,
                  c_ref, m_ref):
    L = q_ref.shape[1]
    dh = M_HEAD_DIM
    scale = dh ** -0.5

    @pl.when(pl.program_id(1) == 0)
    def _():
        c_ref[...] = jnp.zeros_like(c_ref)
        m_ref[...] = jnp.zeros_like(m_ref)

    gc = gc_ref[0] + bc_ref[...]
    gr = gr_ref[...] + br_ref[...]
    lf_c = _log_sigmoid(gc)
    lf_r = _log_sigmoid(gr)

    src = lax.broadcasted_iota(jnp.int32, (L, L), 0)
    qry = lax.broadcasted_iota(jnp.int32, (L, L), 1)
    visible = src <= qry
    tril = jnp.where(src >= qry, 1.0, 0.0).astype(BF16)
    triu = jnp.where(visible, 1.0, 0.0).astype(BF16)
    b_c = sum(_dot(tril, part) for part in _split3(lf_c))
    b_r = sum(_dot(part, triu) for part in _split3(lf_r))
    u_c = gc - pltpu.roll(b_c, LANES - M_HEADS, axis=1)
    u_parts = jnp.concatenate(_split3(u_c), axis=1)
    sel_row = lax.broadcasted_iota(jnp.int32, (3 * LANES, L), 0) % LANES

    sub = lax.broadcasted_iota(jnp.int32, (dh, L), 0)
    ones_row = jnp.where(sub == 0, 1.0, 0.0).astype(BF16)

    for h in range(M_HEADS):
        hs = slice(h * dh, (h + 1) * dh)
        q = q_ref[0, :, hs]
        k = k_ref[0, :, hs]
        kT = k.astype(F32).T
        vT = v_ref[0, :, hs].astype(F32).T.astype(BF16)
        i_r = gr[h:h + 1, :]
        b_rh = b_r[M_HEADS + h:M_HEADS + h + 1, :]
        m_prev = m_ref[h:h + 1, 0:1]

        u_bc = _dot(u_parts, jnp.where(sel_row == h, 1.0, 0.0).astype(BF16))
        dT = jnp.where(visible, u_bc + b_rh, NEG_BIG)
        inter = b_rh + m_prev
        mt = jnp.maximum(inter, jnp.max(dT, axis=0, keepdims=True))
        wT = jnp.exp(dT - mt)
        sc = jnp.exp(inter - mt)
        sT = lax.dot_general(k, q, _NT, preferred_element_type=F32)
        qkT = (sT * (wT * scale)).astype(BF16)

        vaugT = jnp.concatenate([vT, ones_row], axis=0)
        caugT = c_ref[h]
        numT = sc * lax.dot_general(caugT.astype(BF16), q, _NT, preferred_element_type=F32) + _dot(vaugT, qkT)
        den = numT[dh:dh + 1, :]
        hT = numT[:dh, :] / jnp.maximum(jnp.abs(den), jnp.exp(-mt))

        mu = jnp.mean(hT, axis=0, keepdims=True)
        cen = hT - mu
        var = jnp.mean(cen * cen, axis=0, keepdims=True)
        hn = (cen * lax.rsqrt(var + NORM_EPS)).T * hn_ref[:, hs]
        gate = jax.nn.sigmoid(og_ref[0, :, hs].astype(F32))
        out_ref[0, :, hs] = (hn * gate).astype(out_ref.dtype)

        g_end = b_rh[:, L - 1:L]
        ws = g_end - b_rh + i_r
        m_new = jnp.maximum(g_end + m_prev, jnp.max(ws, axis=1, keepdims=True))
        a = jnp.exp(g_end + m_prev - m_new)
        wkT = (kT * (jnp.exp(ws - m_new) * scale)).astype(BF16)
        c_ref[h] = a * caugT + lax.dot_general(vaugT, wkT, _NT, preferred_element_type=F32)
        m_ref[h:h + 1, :] = jnp.broadcast_to(m_new, (1, LANES))


def _mlstm(p3, gc3, gr, b_col, b_row, head_norm):
    bsz, s, _ = p3.shape
    L = MLSTM_CHUNK
    w = M_HEADS * M_HEAD_DIM
    nc = s // L
    blk = lambda j: pl.BlockSpec((1, L, w), lambda b, c, j=j: (b, c, j))
    return pl.pallas_call(
        _mlstm_kernel,
        out_shape=jax.ShapeDtypeStruct((bsz, s, w), BF16),
        grid=(bsz, nc),
        in_specs=[blk(0), blk(1), blk(2), blk(3),
                  pl.BlockSpec((1, L, LANES), lambda b, c: (b, c, 0)),
                  pl.BlockSpec((SUBLANES, L), lambda b, c: (0, b * nc + c)),
                  pl.BlockSpec((1, LANES), lambda b, c: (0, 0)),
                  pl.BlockSpec((SUBLANES, 1), lambda b, c: (0, 0)),
                  pl.BlockSpec((1, w), lambda b, c: (0, 0))],
        out_specs=pl.BlockSpec((1, L, w), lambda b, c: (b, c, 0)),
        scratch_shapes=[pltpu.VMEM((M_HEADS, 2 * M_HEAD_DIM, M_HEAD_DIM), F32),
                        pltpu.VMEM((SUBLANES, LANES), F32)],
        compiler_params=_cparams("arbitrary", "arbitrary"),
        name="mlstm",
    )(p3, p3, p3, p3, gc3, gr, b_col, b_row, head_norm)


def _attn_kernel(lam_ref, q_ref, k_ref, v_ref, bias_ref, sub_ref, out_ref, vaug_ref, *, out_scale, blk):
    s_len = q_ref.shape[1]
    d = DA_HEAD_DIM
    lane_s = lax.broadcasted_iota(jnp.int32, (s_len, DA_V_DIM), 1)
    vaug_ref[:, :DA_V_DIM] = v_ref[0]
    vaug_ref[:, DA_V_DIM:] = jnp.where(lane_s == 0, 1.0, 0.0).astype(BF16)
    lam = lam_ref[0]
    lane = lax.broadcasted_iota(jnp.int32, (blk, LANES), 1)
    for qi in range(s_len // blk):
        q0, q1 = qi * blk, (qi + 1) * blk
        qs = q_ref[0, q0:q1, :] * (d ** -0.5)
        zero = jnp.zeros_like(qs)
        heads = []
        for qm in (jnp.where(lane < d, qs, zero), jnp.where(lane >= d, qs, zero)):
            score = lambda a, b: lax.dot_general(qm, k_ref[0, a:b, :], _NT, preferred_element_type=F32)
            parts = [score(q0, q1) + bias_ref[0, 0]]
            if qi >= 1:
                parts.append(score(q0 - blk, q0) + bias_ref[0, 1])
            if qi >= 2:
                parts.append(score(0, q0 - blk))
            m = functools.reduce(jnp.maximum, [jnp.max(p, axis=-1, keepdims=True) for p in parts])
            pexp = [jnp.exp(p - m).astype(BF16) for p in parts]
            o = _dot(pexp[0], vaug_ref[q0:q1, :])
            if qi >= 1:
                o = o + _dot(pexp[1], vaug_ref[q0 - blk:q0, :])
            if qi >= 2:
                o = o + _dot(pexp[2], vaug_ref[0:q0 - blk, :])
            heads.append(o[:, :DA_V_DIM] / o[:, DA_V_DIM:DA_V_DIM + 1])
        o = heads[0] - lam * heads[1]
        ms = jnp.mean(o * o, axis=-1, keepdims=True)
        o = o * lax.rsqrt(ms + NORM_EPS) * sub_ref[...] * out_scale
        out_ref[0, q0:q1, :] = o.astype(out_ref.dtype)


def _attention(lam, p3, bias_tiles, subln, out_scale):
    bsz, s, _ = p3.shape
    blk = ATT_BLOCK
    w = DA_HEADS * DA_V_DIM
    q_blk = 4 * M_HEADS
    k_blk = q_blk + DA_HEADS
    v_blk = k_blk + DA_HEADS
    col = lambda base: pl.BlockSpec((1, s, LANES), lambda b, h: (b, 0, base + h))
    return pl.pallas_call(
        functools.partial(_attn_kernel, out_scale=out_scale, blk=blk),
        out_shape=jax.ShapeDtypeStruct((bsz, s, w), BF16),
        grid=(bsz, DA_HEADS),
        in_specs=[pl.BlockSpec(memory_space=pltpu.SMEM),
                  col(q_blk), col(k_blk), col(v_blk),
                  pl.BlockSpec((1, 2, blk, blk), lambda b, h: (h, 0, 0, 0)),
                  pl.BlockSpec((1, DA_V_DIM), lambda b, h: (0, 0))],
        out_specs=pl.BlockSpec((1, s, DA_V_DIM), lambda b, h: (b, 0, h)),
        scratch_shapes=[pltpu.VMEM((s, 2 * DA_V_DIM), BF16)],
        compiler_params=_cparams("arbitrary", "arbitrary"),
        name="diff_attention",
    )(lam, p3, p3, p3, bias_tiles, subln)


def _pack_bf16_pairs(v):
    half = v.shape[1] // 2
    hi = lax.bitcast_convert_type(v[:, :half].astype(F32), jnp.uint32)
    lo = lax.bitcast_convert_type(v[:, half:].astype(F32), jnp.uint32)
    return hi | (lo >> 16)


def _unpack_bf16_pairs(w):
    hi = lax.bitcast_convert_type(w & jnp.uint32(0xFFFF0000), F32)
    lo = lax.bitcast_convert_type(w << 16, F32)
    return hi, lo


def _router_epilogue(xn, g_ref, wr_ref, br_ref, triu_ref, hrow_ref, route_ref, cnt_ref):
    tm, d = xn.shape
    h2 = _rms(xn, g_ref[...]).astype(BF16)
    hrow_ref[:, :d // 2] = _pack_bf16_pairs(h2)

    lg = lax.dot_general(wr_ref[...], h2, _NT, preferred_element_type=F32) + br_ref[...]
    gl = [lg[g:g + 1, :] for g in range(MOE_GROUPS)]
    gmax = functools.reduce(jnp.maximum, gl)
    gsel = jnp.where(gl[0] == gmax, 0.0, jnp.where(gl[1] == gmax, 1.0, jnp.where(gl[2] == gmax, 2.0, 3.0)))
    pg = 1.0 / functools.reduce(lambda a, b: a + b, [jnp.exp(x - gmax) for x in gl])

    ev = []
    for j in range(EXPERTS_PER_GROUP):
        acc = jnp.zeros_like(gmax)
        for g in range(MOE_GROUPS):
            r0 = SUBLANES + g * EXPERTS_PER_GROUP + j
            acc = jnp.where(gsel == float(g), lg[r0:r0 + 1, :], acc)
        ev.append(acc)
    v1 = functools.reduce(jnp.maximum, ev)
    i1 = jnp.where(ev[0] == v1, 0.0, jnp.where(ev[1] == v1, 1.0, jnp.where(ev[2] == v1, 2.0, 3.0)))
    ev2 = [jnp.where(i1 == float(j), -jnp.inf, ev[j]) for j in range(EXPERTS_PER_GROUP)]
    v2 = functools.reduce(jnp.maximum, ev2)
    hit = [(ev2[j] == v2) & (i1 != float(j)) for j in range(EXPERTS_PER_GROUP)]
    i2 = jnp.where(hit[0], 0.0, jnp.where(hit[1], 1.0, jnp.where(hit[2], 2.0, 3.0)))
    e21 = jnp.exp(v2 - v1)
    w1 = 1.0 / (1.0 + e21)
    w2 = e21 * w1
    first_low = i1 < i2
    lo = jnp.minimum(i1, i2)
    hi = jnp.maximum(i1, i2)
    c_lo = pg * jnp.where(first_low, w1, w2)
    c_hi = pg * jnp.where(first_low, w2, w1)
    pair = lo * (7.0 - lo) * 0.5 + hi - lo - 1.0
    bucket = gsel * float(N_PAIRS) + pair

    @pl.when(pl.program_id(0) == 0)
    def _():
        cnt_ref[...] = jnp.zeros_like(cnt_ref)

    rows = lax.broadcasted_iota(jnp.int32, (32, tm), 0).astype(F32)
    onehot = rows == bucket
    cum = _dot(jnp.where(onehot, 1.0, 0.0).astype(BF16), triu_ref[...])
    carry = cnt_ref[:, 0:1]
    rank = jnp.sum(jnp.where(onehot, cum + carry, 0.0), axis=0, keepdims=True) - 1.0
    cnt_ref[...] = jnp.broadcast_to(carry + cum[:, tm - 1:tm], cnt_ref.shape)

    rid = lax.broadcasted_iota(jnp.int32, (LANES, tm), 0)
    rec = jnp.where(rid == 0, bucket, jnp.where(rid == 1, c_lo, jnp.where(rid == 2, c_hi,
                    jnp.where(rid == 3, rank, 0.0))))
    hrow_ref[:, d // 2:] = lax.bitcast_convert_type(rec.T, jnp.uint32)
    route_ref[...] = rec[:SUBLANES, :]


_EPI_OUT_SHAPES = lambda t, d: (jax.ShapeDtypeStruct((t, d), F32),
                                jax.ShapeDtypeStruct((t, d // 2 + LANES), jnp.uint32),
                                jax.ShapeDtypeStruct((SUBLANES, t), F32),
                                jax.ShapeDtypeStruct((32, LANES), F32))


def _epi_out_specs(tm, d):
    return (pl.BlockSpec((tm, d), lambda i: (i, 0)),
            pl.BlockSpec((tm, d // 2 + LANES), lambda i: (i, 0)),
            pl.BlockSpec((SUBLANES, tm), lambda i: (0, i)),
            pl.BlockSpec((32, LANES), lambda i: (0, 0)))


def _epi_in_specs(tm, d):
    return [pl.BlockSpec((1, d), lambda i: (0, 0)),
            pl.BlockSpec((32, d), lambda i: (0, 0)),
            pl.BlockSpec((32, 1), lambda i: (0, 0)),
            pl.BlockSpec((tm, tm), lambda i: (0, 0))]


def _outproj_even_kernel(hm_ref, ha_ref, x_ref, wo_ref, g_ref, wr_ref, br_ref, triu_ref,
                         xn_ref, h2_ref, meta_ref, cnt_ref):
    half = hm_ref.shape[1]
    y = _dot(hm_ref[...], wo_ref[:half, :]) + _dot(ha_ref[...], wo_ref[half:, :])
    xn = x_ref[...] + y
    xn_ref[...] = xn
    _router_epilogue(xn, g_ref, wr_ref, br_ref, triu_ref, h2_ref, meta_ref, cnt_ref)


def _outproj_even(hm, ha, x2d, w_out, g, wr, br, triu):
    t, d = x2d.shape
    tm = ROW_TILE
    half = hm.shape[1]
    return pl.pallas_call(
        _outproj_even_kernel,
        out_shape=_EPI_OUT_SHAPES(t, d),
        grid=(t // tm,),
        in_specs=[pl.BlockSpec((tm, half), lambda i: (i, 0)),
                  pl.BlockSpec((tm, half), lambda i: (i, 0)),
                  pl.BlockSpec((tm, d), lambda i: (i, 0)),
                  pl.BlockSpec((2 * half, d), lambda i: (0, 0))] + _epi_in_specs(tm, d),
        out_specs=_epi_out_specs(tm, d),
        compiler_params=_cparams("arbitrary"),
        name="outproj_even",
    )(hm, ha, x2d, w_out, g, wr, br, triu)


def _conv_mixer_kernel(x_ref, y_ref, gm_ref, wi_ref, cw_ref, wo_ref, g_ref, wr_ref, br_ref, triu_ref,
                       xn_ref, h2_ref, meta_ref, cnt_ref, tail_ref, *, tiles_per_seq):
    tm, d = x_ref.shape
    x = x_ref[...] + jnp.concatenate(_unpack_bf16_pairs(y_ref[...]), axis=1)
    h = _rms(x, gm_ref[...]).astype(BF16)

    @pl.when(pl.program_id(0) % tiles_per_seq == 0)
    def _():
        tail_ref[...] = jnp.zeros_like(tail_ref)

    row = lax.broadcasted_iota(jnp.int32, (tm, 512), 0)
    acc = jnp.zeros((tm, d), F32)
    for j in range(d // 512):
        sl = slice(j * 512, (j + 1) * 512)
        bg = _dot(h, wi_ref[:, sl])
        cg = _dot(h, wi_ref[:, d + j * 512:d + (j + 1) * 512])
        u = _dot(h, wi_ref[:, 2 * d + j * 512:2 * d + (j + 1) * 512])
        z = cg * u
        prev1 = tail_ref[SUBLANES - 1:SUBLANES, sl]
        prev2 = tail_ref[SUBLANES - 2:SUBLANES - 1, sl]
        z1 = jnp.where(row == 0, prev1, pltpu.roll(z, 1, axis=0))
        z2 = jnp.where(row == 0, prev2, jnp.where(row == 1, prev1, pltpu.roll(z, 2, axis=0)))
        zc = cw_ref[0:1, sl] * z2 + cw_ref[1:2, sl] * z1 + cw_ref[2:3, sl] * z
        tail_ref[:, sl] = z[tm - SUBLANES:, :]
        acc = acc + _dot((bg * zc).astype(BF16), wo_ref[sl, :])
    xn = x + acc
    xn_ref[...] = xn
    _router_epilogue(xn, g_ref, wr_ref, br_ref, triu_ref, h2_ref, meta_ref, cnt_ref)


def _conv_mixer(x2d, ymoe, gm, w_in, conv_w, w_out, g, wr, br, triu, seq):
    t, d = x2d.shape
    tm = ROW_TILE
    return pl.pallas_call(
        functools.partial(_conv_mixer_kernel, tiles_per_seq=seq // tm),
        out_shape=_EPI_OUT_SHAPES(t, d),
        grid=(t // tm,),
        in_specs=[pl.BlockSpec((tm, d), lambda i: (i, 0)),
                  pl.BlockSpec((tm, d // 2), lambda i: (i, 0)),
                  pl.BlockSpec((1, d), lambda i: (0, 0)),
                  pl.BlockSpec((d, 3 * d), lambda i: (0, 0)),
                  pl.BlockSpec((SUBLANES, d), lambda i: (0, 0)),
                  pl.BlockSpec((d, d), lambda i: (0, 0))] + _epi_in_specs(tm, d),
        out_specs=_epi_out_specs(tm, d),
        scratch_shapes=[pltpu.VMEM((SUBLANES, d), F32)],
        compiler_params=_cparams("arbitrary"),
        name="conv_mixer",
    )(x2d, ymoe, gm, w_in, conv_w, w_out, g, wr, br, triu)


def _moe_kernel(elo_ref, ehi_ref, valid_ref, fresh_ref,
                src_ref, src_next_ref, dst_prev_ref,
                h_hbm,
                wg_lo, wu_lo, wd_lo, wg_hi, wu_hi, wd_hi,
                y_hbm,
                xbuf0, xbuf1, obuf0, obuf1, zbuf, wbuf_in, wbuf_out, gsem, ssem, zsem):
    i = pl.program_id(0)
    n = pl.num_programs(0)
    tm, half = obuf0.shape
    xbufs = (xbuf0, xbuf1)
    obufs = (obuf0, obuf1)
    valid = valid_ref[i] != 0
    prev_valid = valid_ref[jnp.maximum(i - 1, 0)] != 0

    def row_gather(idx_ref, r, s):
        return pltpu.make_async_copy(h_hbm.at[pl.ds(idx_ref[0, 0, r], 1)], xbufs[s].at[pl.ds(r, 1)], gsem.at[s])

    def row_scatter(r, s):
        return pltpu.make_async_copy(obufs[s].at[pl.ds(r, 1)], y_hbm.at[pl.ds(dst_prev_ref[0, 0, r], 1)], ssem.at[s])

    wait_gather = lambda s: pltpu.make_async_copy(h_hbm.at[pl.ds(0, tm)], xbufs[s], gsem.at[s]).wait()
    wait_scatter = lambda s: pltpu.make_async_copy(obufs[s], y_hbm.at[pl.ds(0, tm)], ssem.at[s]).wait()

    @pl.when(i == 0)
    def _():
        obuf1[...] = jnp.zeros_like(obuf1)
        zbuf[...] = jnp.zeros_like(zbuf)

        @pl.loop(0, tm)
        def _(r):
            row_gather(src_ref, r, 0).start()

    @pl.when(fresh_ref[i] != 0)
    def _():
        for k, w in enumerate((wg_lo, wu_lo, wg_hi, wu_hi)):
            wbuf_in[k] = w[0].astype(BF16)
        for k, w in enumerate((wd_lo, wd_hi)):
            wbuf_out[k] = w[0].astype(BF16)

    for s in range(2):
        @pl.when(valid & (i % 2 == s))
        def _(s=s):
            wait_gather(s)
            for r in range(tm):
                row_gather(src_next_ref, r, 1 - s).start(priority=r % 2)
            for r in range(tm):
                row_scatter(r, 1 - s).start(priority=r % 2)
            rows = xbufs[s][...]
            xa, xb = _unpack_bf16_pairs(rows[:, :half])
            x = jnp.concatenate([xa.astype(BF16), xb.astype(BF16)], axis=1)
            rec = lax.bitcast_convert_type(rows[:, half:], F32)

            def ffn(k):
                a = _dot(x, wbuf_in[2 * k])
                a = (a * jax.nn.sigmoid(a)) * _dot(x, wbuf_in[2 * k + 1])
                return _dot(a.astype(BF16), wbuf_out[k])

            out = rec[:, 1:2] * ffn(0) + rec[:, 2:3] * ffn(1)
            obufs[s][...] = _pack_bf16_pairs(out.astype(BF16))
            wait_scatter(1 - s)

        @pl.when(jnp.logical_not(valid) & prev_valid & (i % 2 == s))
        def _(s=s):
            wait_gather(s)

            @pl.loop(0, tm)
            def _(r):
                row_scatter(r, 1 - s).start()

            wait_scatter(1 - s)

    @pl.when(jnp.logical_not(valid) & (i < n - 1))
    def _():
        fill = pltpu.make_async_copy(zbuf, y_hbm.at[pl.ds(pl.multiple_of(i * tm, tm), tm)], zsem)
        fill.start()
        fill.wait()


def _moe(tiles, src_rows, dst_rows, hrows, w_gate, w_up, w_down):
    tile_elo, tile_ehi, tile_valid, tile_fresh = tiles
    n_tiles = tile_elo.shape[0]
    t, wrow = hrows.shape
    d = w_gate.shape[1]
    ff = w_gate.shape[2]
    tm = MOE_TILE
    lo = lambda i, elo, *_: (elo[i], 0, 0)
    hi = lambda i, elo, ehi, *_: (ehi[i], 0, 0)
    cur = lambda i, *_: (i, 0, 0)
    nxt = lambda i, *_: ((i + 1) % n_tiles, 0, 0)
    prv = lambda i, *_: ((i + n_tiles - 1) % n_tiles, 0, 0)
    idx = lambda m: pl.BlockSpec((1, 1, tm), m, memory_space=pltpu.SMEM)
    return pl.pallas_call(
        _moe_kernel,
        out_shape=jax.ShapeDtypeStruct((n_tiles * tm, d // 2), jnp.uint32),
        grid_spec=pltpu.PrefetchScalarGridSpec(
            num_scalar_prefetch=4,
            grid=(n_tiles,),
            in_specs=[idx(cur), idx(nxt), idx(prv),
                      pl.BlockSpec(memory_space=pl.ANY),
                      pl.BlockSpec((1, d, ff), lo), pl.BlockSpec((1, d, ff), lo), pl.BlockSpec((1, ff, d), lo),
                      pl.BlockSpec((1, d, ff), hi), pl.BlockSpec((1, d, ff), hi), pl.BlockSpec((1, ff, d), hi)],
            out_specs=pl.BlockSpec(memory_space=pl.ANY),
            scratch_shapes=[pltpu.VMEM((tm, wrow), jnp.uint32),
                            pltpu.VMEM((tm, wrow), jnp.uint32),
                            pltpu.VMEM((tm, d // 2), jnp.uint32),
                            pltpu.VMEM((tm, d // 2), jnp.uint32),
                            pltpu.VMEM((tm, d // 2), jnp.uint32),
                            pltpu.VMEM((4, d, ff), BF16),
                            pltpu.VMEM((2, ff, d), BF16),
                            pltpu.SemaphoreType.DMA((2,)),
                            pltpu.SemaphoreType.DMA((2,)),
                            pltpu.SemaphoreType.DMA]),
        compiler_params=_cparams("arbitrary"),
        name="grouped_moe",
    )(tile_elo, tile_ehi, tile_valid, tile_fresh, src_rows, src_rows, dst_rows, hrows,
      w_gate, w_up, w_down, w_gate, w_up, w_down)


def _final_kernel(x_ref, y_ref, g_ref, out_ref):
    out_ref[...] = _rms(x_ref[...] + jnp.concatenate(_unpack_bf16_pairs(y_ref[...]), axis=1), g_ref[...])


def _final(x2d, ymoe, g):
    t, d = x2d.shape
    tm = ROW_TILE
    return pl.pallas_call(
        _final_kernel,
        out_shape=jax.ShapeDtypeStruct((t, d), F32),
        grid=(t // tm,),
        in_specs=[pl.BlockSpec((tm, d), lambda i: (i, 0)),
                  pl.BlockSpec((tm, d // 2), lambda i: (i, 0)),
                  pl.BlockSpec((1, d), lambda i: (0, 0))],
        out_specs=pl.BlockSpec((tm, d), lambda i: (i, 0)),
        compiler_params=_cparams("arbitrary"),
        name="final_norm",
    )(x2d, ymoe, g)


_PAIRS = [(a, b) for a in range(EXPERTS_PER_GROUP) for b in range(a + 1, EXPERTS_PER_GROUP)]
_BUCKET_ELO = np.array([g * EXPERTS_PER_GROUP + _PAIRS[p][0] for g in range(MOE_GROUPS) for p in range(N_PAIRS)], np.int32)
_BUCKET_EHI = np.array([g * EXPERTS_PER_GROUP + _PAIRS[p][1] for g in range(MOE_GROUPS) for p in range(N_PAIRS)], np.int32)


def _moe_layer(hrows, route, cnt, layer, w_gate, w_up, w_down):
    t = hrows.shape[0]
    tm = MOE_TILE
    n_tiles = t // tm + N_BUCKETS
    tp = n_tiles * tm
    bucket = route[0].astype(jnp.int32)
    rank = route[3].astype(jnp.int32)
    counts = cnt[:N_BUCKETS, 0].astype(jnp.int32)
    padded = ((counts + tm - 1) // tm) * tm
    ends = jnp.cumsum(padded)
    dest = (ends - padded)[bucket] + rank
    tile_start = jnp.arange(n_tiles, dtype=jnp.int32) * tm
    tile_valid = tile_start < ends[-1]
    last_start = jnp.maximum(ends[-1] - tm, 0)
    tile_bucket = jnp.sum((ends[None, :] <= jnp.minimum(tile_start, last_start)[:, None]).astype(jnp.int32), axis=1)
    prev_bucket = jnp.concatenate([jnp.full((1,), -1, jnp.int32), tile_bucket[:-1]])
    tile_fresh = tile_valid & (tile_bucket != prev_bucket)
    base = layer * N_EXPERTS
    token_at = jnp.full((tp,), -1, jnp.int32).at[dest].set(jnp.arange(t, dtype=jnp.int32))
    is_pad = token_at < 0
    src_rows = jnp.where(is_pad, 0, token_at)
    dst_rows = jnp.where(is_pad, t - 1 + jnp.cumsum(is_pad.astype(jnp.int32)), token_at)
    tiles = (jnp.asarray(_BUCKET_ELO)[tile_bucket] + base, jnp.asarray(_BUCKET_EHI)[tile_bucket] + base,
             tile_valid.astype(jnp.int32), tile_fresh.astype(jnp.int32))
    shape = (n_tiles, 1, tm)
    return _moe(tiles, src_rows.reshape(shape), dst_rows.reshape(shape), hrows, w_gate, w_up, w_down)


def _router_weights(wg, bg, we, be):
    d = wg.shape[0]
    wr = jnp.zeros((32, d), F32).at[:MOE_GROUPS].set(wg.T).at[SUBLANES:SUBLANES + N_EXPERTS].set(we.T)
    br = jnp.zeros((32, 1), F32).at[:MOE_GROUPS, 0].set(bg).at[SUBLANES:SUBLANES + N_EXPERTS, 0].set(be)
    return wr.astype(BF16), br


def kernel(x, rel_bias, norm_mix, norm_ffn, norm_final, w_in_even, w_out_even, m_igate_b, m_fgate_b,
           m_head_norm, lam_q1, lam_k1, lam_q2, lam_k2, da_subln, w_in_odd, conv_w, w_out_odd,
           router_group_w, router_group_b, router_expert_w, router_expert_b,
           exp_w_gate, exp_w_up, exp_w_down):
    bsz, seq, d = x.shape
    t = bsz * seq
    x2d = x.reshape(t, d)
    mw = M_HEADS * M_HEAD_DIM

    w_in = w_in_even[0]
    gate0 = 4 * mw
    w_main = jnp.concatenate([w_in[:, :gate0], w_in[:, gate0 + 2 * M_HEADS:]], axis=1).astype(BF16)
    w_gates = w_in[:, gate0:gate0 + 2 * M_HEADS]
    wg_col = jnp.zeros((d, LANES), F32).at[:, :2 * M_HEADS].set(w_gates).astype(BF16)
    wg_row = w_gates.T.astype(BF16)
    gate_b = jnp.concatenate([m_igate_b[0], m_fgate_b[0]]).astype(F32)
    b_col = jnp.zeros((1, LANES), F32).at[0, :2 * M_HEADS].set(gate_b)
    b_row = gate_b.reshape(2 * M_HEADS, 1)
    triu = jnp.triu(jnp.ones((ROW_TILE, ROW_TILE), F32)).astype(BF16)

    p, gc, gr = _inproj_even(x2d, norm_mix[0].reshape(1, d), w_main, wg_col, wg_row)
    p3 = p.reshape(bsz, seq, -1)
    hm = _mlstm(p3, gc.reshape(bsz, seq, LANES), gr, b_col, b_row, m_head_norm[0].reshape(1, mw))

    lam_init = 0.8 - 0.6 * math.exp(-0.3 * 0)
    lam = (jnp.exp(jnp.sum(lam_q1[0].astype(F32) * lam_k1[0].astype(F32)))
           - jnp.exp(jnp.sum(lam_q2[0].astype(F32) * lam_k2[0].astype(F32))) + lam_init).reshape(1)
    ha = _attention(lam, p3, _bias_tiles(rel_bias.astype(F32), ATT_BLOCK),
                    da_subln[0].reshape(1, DA_V_DIM).astype(F32), 1.0 - lam_init)

    wr0, br0 = _router_weights(router_group_w[0], router_group_b[0], router_expert_w[0], router_expert_b[0])
    x1, h2, meta, cnt = _outproj_even(hm.reshape(t, mw), ha.reshape(t, -1), x2d, w_out_even[0].astype(BF16),
                                      norm_ffn[0].reshape(1, d), wr0, br0, triu)
    ff = exp_w_gate.shape[-1]
    experts = (exp_w_gate.reshape(-1, d, ff), exp_w_up.reshape(-1, d, ff), exp_w_down.reshape(-1, ff, d))
    y = _moe_layer(h2, meta, cnt, 0, *experts)

    wr1, br1 = _router_weights(router_group_w[1], router_group_b[1], router_expert_w[1], router_expert_b[1])
    cw = jnp.zeros((SUBLANES, d), F32).at[:conv_w.shape[1]].set(conv_w[0])
    x3, h2, meta, cnt = _conv_mixer(x1, y, norm_mix[1].reshape(1, d), w_in_odd[0].astype(BF16), cw,
                                    w_out_odd[0].astype(BF16), norm_ffn[1].reshape(1, d), wr1, br1, triu, seq)
    y = _moe_layer(h2, meta, cnt, 1, *experts)

    return _final(x3, y, norm_final.reshape(1, d)).reshape(bsz, seq, d)
```

```python
import functools
import math

import jax
import jax.numpy as jnp
import numpy as np
from jax import lax
from jax.experimental import pallas as pl
from jax.experimental.pallas import tpu as pltpu

F32 = jnp.float32
BF16 = jnp.bfloat16

NORM_EPS = 1e-6
NEG_BIG = -1e30

M_HEADS = 4
M_HEAD_DIM = 128
DA_HEADS = 4
DA_HEAD_DIM = 64
DA_V_DIM = 128
REL_BUCKETS = 32
REL_MAX_DIST = 128
MOE_GROUPS = 4
EXPERTS_PER_GROUP = 4
N_EXPERTS = 16
N_PAIRS = 6
N_BUCKETS = MOE_GROUPS * N_PAIRS

LANES = 128
SUBLANES = 8
VMEM_LIMIT = 56 * 1024 * 1024

ROW_TILE = 512
MLSTM_CHUNK = 256
ATT_BLOCK = 256
MOE_TILE = 512

_NT = (((1,), (1,)), ((), ()))


def _cparams(*sem):
    return pltpu.CompilerParams(dimension_semantics=sem, vmem_limit_bytes=VMEM_LIMIT)


def _rms(x, g):
    ms = jnp.mean(x * x, axis=-1, keepdims=True)
    return x * lax.rsqrt(ms + NORM_EPS) * g


def _dot(a, b):
    return jnp.dot(a, b, preferred_element_type=F32)


def _split3(x):
    hi = x.astype(BF16)
    r1 = x - hi.astype(F32)
    mid = r1.astype(BF16)
    lo = (r1 - mid.astype(F32)).astype(BF16)
    return hi, mid, lo


def _log_sigmoid(x):
    return jnp.minimum(x, 0.0) - jnp.log(1.0 + jnp.exp(-jnp.abs(x)))


def _bias_tiles_kernel(rb_ref, out_ref):
    h = pl.program_id(0)
    blk = out_ref.shape[-1]
    r = lax.broadcasted_iota(jnp.int32, (blk, blk), 0)
    c = lax.broadcasted_iota(jnp.int32, (blk, blk), 1)
    far = rb_ref[REL_BUCKETS - 1, h]
    max_exact = REL_BUCKETS // 2
    for j in range(2):
        rel = c - r - j * blk
        n = jnp.maximum(-rel, 0)
        nf = jnp.maximum(n, 1).astype(F32)
        large = max_exact + (jnp.log(nf / max_exact) / math.log(REL_MAX_DIST / max_exact)
                             * (REL_BUCKETS - max_exact)).astype(jnp.int32)
        large = jnp.minimum(large, REL_BUCKETS - 1)
        bucket = jnp.where(n < max_exact, n, large)
        val = jnp.zeros((blk, blk), F32)
        for b in range(REL_BUCKETS):
            val = jnp.where(bucket == b, rb_ref[b, h], val)
        val = val - far
        if j == 0:
            val = jnp.where(rel <= 0, val, NEG_BIG)
        out_ref[0, j] = val


def _bias_tiles(rel_bias, blk):
    return pl.pallas_call(
        _bias_tiles_kernel,
        out_shape=jax.ShapeDtypeStruct((DA_HEADS, 2, blk, blk), F32),
        grid=(DA_HEADS,),
        in_specs=[pl.BlockSpec(memory_space=pltpu.SMEM)],
        out_specs=pl.BlockSpec((1, 2, blk, blk), lambda h: (h, 0, 0, 0)),
        compiler_params=_cparams("arbitrary"),
        name="bias_tiles",
    )(rel_bias)


def _inproj_even_kernel(x_ref, g_ref, w_ref, wgc_ref, wgr_ref, p_ref, gc_ref, gr_ref):
    h = _rms(x_ref[...], g_ref[...]).astype(BF16)
    n = w_ref.shape[1]
    for j in range(n // 512):
        sl = slice(j * 512, (j + 1) * 512)
        p_ref[:, sl] = _dot(h, w_ref[:, sl]).astype(BF16)
    gc_ref[...] = _dot(h, wgc_ref[...])
    gr_ref[...] = lax.dot_general(wgr_ref[...], h, _NT, preferred_element_type=F32)


def _inproj_even(x2d, g, w_main, wg_col, wg_row):
    t, d = x2d.shape
    n = w_main.shape[1]
    tm = ROW_TILE
    return pl.pallas_call(
        _inproj_even_kernel,
        out_shape=(jax.ShapeDtypeStruct((t, n), BF16),
                   jax.ShapeDtypeStruct((t, LANES), F32),
                   jax.ShapeDtypeStruct((SUBLANES, t), F32)),
        grid=(t // tm,),
        in_specs=[pl.BlockSpec((tm, d), lambda i: (i, 0)),
                  pl.BlockSpec((1, d), lambda i: (0, 0)),
                  pl.BlockSpec((d, n), lambda i: (0, 0)),
                  pl.BlockSpec((d, LANES), lambda i: (0, 0)),
                  pl.BlockSpec((SUBLANES, d), lambda i: (0, 0))],
        out_specs=(pl.BlockSpec((tm, n), lambda i: (i, 0)),
                   pl.BlockSpec((tm, LANES), lambda i: (i, 0)),
                   pl.BlockSpec((SUBLANES, tm), lambda i: (0, i))),
        compiler_params=_cparams("arbitrary"),
        name="inproj_even",
    )(x2d, g, w_main, wg_col, wg_row)


def _mlstm_kernel(q_ref, k_ref, v_ref, og_ref, gc_ref, gr_ref, bc_ref, br_ref, hn_ref, out_ref,
                  c_ref, m_ref):
    L = q_ref.shape[1]
    dh = M_HEAD_DIM
    scale = dh ** -0.5

    @pl.when(pl.program_id(1) == 0)
    def _():
        c_ref[...] = jnp.zeros_like(c_ref)
        m_ref[...] = jnp.zeros_like(m_ref)

    gc = gc_ref[0] + bc_ref[...]
    gr = gr_ref[...] + br_ref[...]
    lf_c = _log_sigmoid(gc)
    lf_r = _log_sigmoid(gr)

    src = lax.broadcasted_iota(jnp.int32, (L, L), 0)
    qry = lax.broadcasted_iota(jnp.int32, (L, L), 1)
    visible = src <= qry
    tril = jnp.where(src >= qry, 1.0, 0.0).astype(BF16)
    triu = jnp.where(visible, 1.0, 0.0).astype(BF16)
    b_c = sum(_dot(tril, part) for part in _split3(lf_c))
    b_r = sum(_dot(part, triu) for part in _split3(lf_r))
    u_c = gc - pltpu.roll(b_c, LANES - M_HEADS, axis=1)
    u_parts = jnp.concatenate(_split3(u_c), axis=1)
    sel_row = lax.broadcasted_iota(jnp.int32, (3 * LANES, L), 0) % LANES

    sub = lax.broadcasted_iota(jnp.int32, (dh, L), 0)
    ones_row = jnp.where(sub == 0, 1.0, 0.0).astype(BF16)

    for h in range(M_HEADS):
        hs = slice(h * dh, (h + 1) * dh)
        q = q_ref[0, :, hs]
        k = k_ref[0, :, hs]
        kT = k.astype(F32).T
        vT = v_ref[0, :, hs].astype(F32).T.astype(BF16)
        i_r = gr[h:h + 1, :]
        b_rh = b_r[M_HEADS + h:M_HEADS + h + 1, :]
        m_prev = m_ref[h:h + 1, 0:1]

        u_bc = _dot(u_parts, jnp.where(sel_row == h, 1.0, 0.0).astype(BF16))
        dT = jnp.where(visible, u_bc + b_rh, NEG_BIG)
        inter = b_rh + m_prev
        mt = jnp.maximum(inter, jnp.max(dT, axis=0, keepdims=True))
        wT = jnp.exp(dT - mt)
        sc = jnp.exp(inter - mt)
        sT = lax.dot_general(k, q, _NT, preferred_element_type=F32)
        qkT = (sT * (wT * scale)).astype(BF16)

        vaugT = jnp.concatenate([vT, ones_row], axis=0)
        caugT = c_ref[h]
        numT = sc * lax.dot_general(caugT.astype(BF16), q, _NT, preferred_element_type=F32) + _dot(vaugT, qkT)
        den = numT[dh:dh + 1, :]
        hT = numT[:dh, :] / jnp.maximum(jnp.abs(den), jnp.exp(-mt))

        mu = jnp.mean(hT, axis=0, keepdims=True)
        cen = hT - mu
        var = jnp.mean(cen * cen, axis=0, keepdims=True)
        hn = (cen * lax.rsqrt(var + NORM_EPS)).T * hn_ref[:, hs]
        gate = jax.nn.sigmoid(og_ref[0, :, hs].astype(F32))
        out_ref[0, :, hs] = (hn * gate).astype(out_ref.dtype)

        g_end = b_rh[:, L - 1:L]
        ws = g_end - b_rh + i_r
        m_new = jnp.maximum(g_end + m_prev, jnp.max(ws, axis=1, keepdims=True))
        a = jnp.exp(g_end + m_prev - m_new)
        wkT = (kT * (jnp.exp(ws - m_new) * scale)).astype(BF16)
        c_ref[h] = a * caugT + lax.dot_general(vaugT, wkT, _NT, preferred_element_type=F32)
        m_ref[h:h + 1, :] = jnp.broadcast_to(m_new, (1, LANES))


def _mlstm(p3, gc3, gr, b_col, b_row, head_norm):
    bsz, s, _ = p3.shape
    L = MLSTM_CHUNK
    w = M_HEADS * M_HEAD_DIM
    nc = s // L
    blk = lambda j: pl.BlockSpec((1, L, w), lambda b, c, j=j: (b, c, j))
    return pl.pallas_call(
        _mlstm_kernel,
        out_shape=jax.ShapeDtypeStruct((bsz, s, w), BF16),
        grid=(bsz, nc),
        in_specs=[blk(0), blk(1), blk(2), blk(3),
                  pl.BlockSpec((1, L, LANES), lambda b, c: (b, c, 0)),
                  pl.BlockSpec((SUBLANES, L), lambda b, c: (0, b * nc + c)),
                  pl.BlockSpec((1, LANES), lambda b, c: (0, 0)),
                  pl.BlockSpec((SUBLANES, 1), lambda b, c: (0, 0)),
                  pl.BlockSpec((1, w), lambda b, c: (0, 0))],
        out_specs=pl.BlockSpec((1, L, w), lambda b, c: (b, c, 0)),
        scratch_shapes=[pltpu.VMEM((M_HEADS, 2 * M_HEAD_DIM, M_HEAD_DIM), F32),
                        pltpu.VMEM((SUBLANES, LANES), F32)],
        compiler_params=_cparams("arbitrary", "arbitrary"),
        name="mlstm",
    )(p3, p3, p3, p3, gc3, gr, b_col, b_row, head_norm)


def _attn_kernel(lam_ref, q_ref, k_ref, v_ref, bias_ref, sub_ref, out_ref, vaug_ref, *, out_scale, blk):
    s_len = q_ref.shape[1]
    d = DA_HEAD_DIM
    lane_s = lax.broadcasted_iota(jnp.int32, (s_len, DA_V_DIM), 1)
    vaug_ref[:, :DA_V_DIM] = v_ref[0]
    vaug_ref[:, DA_V_DIM:] = jnp.where(lane_s == 0, 1.0, 0.0).astype(BF16)
    lam = lam_ref[0]
    lane = lax.broadcasted_iota(jnp.int32, (blk, LANES), 1)
    for qi in range(s_len // blk):
        q0, q1 = qi * blk, (qi + 1) * blk
        qs = q_ref[0, q0:q1, :] * (d ** -0.5)
        zero = jnp.zeros_like(qs)
        heads = []
        for qm in (jnp.where(lane < d, qs, zero), jnp.where(lane >= d, qs, zero)):
            score = lambda a, b: lax.dot_general(qm, k_ref[0, a:b, :], _NT, preferred_element_type=F32)
            parts = [score(q0, q1) + bias_ref[0, 0]]
            if qi >= 1:
                parts.append(score(q0 - blk, q0) + bias_ref[0, 1])
            if qi >= 2:
                parts.append(score(0, q0 - blk))
            m = functools.reduce(jnp.maximum, [jnp.max(p, axis=-1, keepdims=True) for p in parts])
            pexp = [jnp.exp(p - m).astype(BF16) for p in parts]
            o = _dot(pexp[0], vaug_ref[q0:q1, :])
            if qi >= 1:
                o = o + _dot(pexp[1], vaug_ref[q0 - blk:q0, :])
            if qi >= 2:
                o = o + _dot(pexp[2], vaug_ref[0:q0 - blk, :])
            heads.append(o[:, :DA_V_DIM] / o[:, DA_V_DIM:DA_V_DIM + 1])
        o = heads[0] - lam * heads[1]
        ms = jnp.mean(o * o, axis=-1, keepdims=True)
        o = o * lax.rsqrt(ms + NORM_EPS) * sub_ref[...] * out_scale
        out_ref[0, q0:q1, :] = o.astype(out_ref.dtype)


def _attention(lam, p3, bias_tiles, subln, out_scale):
    bsz, s, _ = p3.shape
    blk = ATT_BLOCK
    w = DA_HEADS * DA_V_DIM
    q_blk = 4 * M_HEADS
    k_blk = q_blk + DA_HEADS
    v_blk = k_blk + DA_HEADS
    col = lambda base: pl.BlockSpec((1, s, LANES), lambda b, h: (b, 0, base + h))
    return pl.pallas_call(
        functools.partial(_attn_kernel, out_scale=out_scale, blk=blk),
        out_shape=jax.ShapeDtypeStruct((bsz, s, w), BF16),
        grid=(bsz, DA_HEADS),
        in_specs=[pl.BlockSpec(memory_space=pltpu.SMEM),
                  col(q_blk), col(k_blk), col(v_blk),
                  pl.BlockSpec((1, 2, blk, blk), lambda b, h: (h, 0, 0, 0)),
                  pl.BlockSpec((1, DA_V_DIM), lambda b, h: (0, 0))],
        out_specs=pl.BlockSpec((1, s, DA_V_DIM), lambda b, h: (b, 0, h)),
        scratch_shapes=[pltpu.VMEM((s, 2 * DA_V_DIM), BF16)],
        compiler_params=_cparams("arbitrary", "arbitrary"),
        name="diff_attention",
    )(lam, p3, p3, p3, bias_tiles, subln)


def _pack_bf16_pairs(v):
    half = v.shape[1] // 2
    hi = lax.bitcast_convert_type(v[:, :half].astype(F32), jnp.uint32)
    lo = lax.bitcast_convert_type(v[:, half:].astype(F32), jnp.uint32)
    return hi | (lo >> 16)


def _unpack_bf16_pairs(w):
    hi = lax.bitcast_convert_type(w & jnp.uint32(0xFFFF0000), F32)
    lo = lax.bitcast_convert_type(w << 16, F32)
    return hi, lo


def _router_epilogue(xn, g_ref, wr_ref, br_ref, triu_ref, hrow_ref, route_ref, cnt_ref):
    tm, d = xn.shape
    h2 = _rms(xn, g_ref[...]).astype(BF16)
    hrow_ref[:, :d // 2] = _pack_bf16_pairs(h2)

    lg = lax.dot_general(wr_ref[...], h2, _NT, preferred_element_type=F32) + br_ref[...]
    gl = [lg[g:g + 1, :] for g in range(MOE_GROUPS)]
    gmax = functools.reduce(jnp.maximum, gl)
    gsel = jnp.where(gl[0] == gmax, 0.0, jnp.where(gl[1] == gmax, 1.0, jnp.where(gl[2] == gmax, 2.0, 3.0)))
    pg = 1.0 / functools.reduce(lambda a, b: a + b, [jnp.exp(x - gmax) for x in gl])

    ev = []
    for j in range(EXPERTS_PER_GROUP):
        acc = jnp.zeros_like(gmax)
        for g in range(MOE_GROUPS):
            r0 = SUBLANES + g * EXPERTS_PER_GROUP + j
            acc = jnp.where(gsel == float(g), lg[r0:r0 + 1, :], acc)
        ev.append(acc)
    v1 = functools.reduce(jnp.maximum, ev)
    i1 = jnp.where(ev[0] == v1, 0.0, jnp.where(ev[1] == v1, 1.0, jnp.where(ev[2] == v1, 2.0, 3.0)))
    ev2 = [jnp.where(i1 == float(j), -jnp.inf, ev[j]) for j in range(EXPERTS_PER_GROUP)]
    v2 = functools.reduce(jnp.maximum, ev2)
    hit = [(ev2[j] == v2) & (i1 != float(j)) for j in range(EXPERTS_PER_GROUP)]
    i2 = jnp.where(hit[0], 0.0, jnp.where(hit[1], 1.0, jnp.where(hit[2], 2.0, 3.0)))
    e21 = jnp.exp(v2 - v1)
    w1 = 1.0 / (1.0 + e21)
    w2 = e21 * w1
    first_low = i1 < i2
    lo = jnp.minimum(i1, i2)
    hi = jnp.maximum(i1, i2)
    c_lo = pg * jnp.where(first_low, w1, w2)
    c_hi = pg * jnp.where(first_low, w2, w1)
    pair = lo * (7.0 - lo) * 0.5 + hi - lo - 1.0
    bucket = gsel * float(N_PAIRS) + pair

    @pl.when(pl.program_id(0) == 0)
    def _():
        cnt_ref[...] = jnp.zeros_like(cnt_ref)

    rows = lax.broadcasted_iota(jnp.int32, (32, tm), 0).astype(F32)
    onehot = rows == bucket
    cum = _dot(jnp.where(onehot, 1.0, 0.0).astype(BF16), triu_ref[...])
    carry = cnt_ref[:, 0:1]
    rank = jnp.sum(jnp.where(onehot, cum + carry, 0.0), axis=0, keepdims=True) - 1.0
    cnt_ref[...] = jnp.broadcast_to(carry + cum[:, tm - 1:tm], cnt_ref.shape)

    token = (pl.program_id(0) * tm + 1 + lax.broadcasted_iota(jnp.int32, (1, tm), 1)).astype(F32)
    rid = lax.broadcasted_iota(jnp.int32, (LANES, tm), 0)
    rec = jnp.where(rid == 0, bucket, jnp.where(rid == 1, c_lo, jnp.where(rid == 2, c_hi,
                    jnp.where(rid == 3, rank, jnp.where(rid == 4, token, 0.0)))))
    hrow_ref[:, d // 2:] = lax.bitcast_convert_type(rec.T, jnp.uint32)
    route_ref[...] = rec[:SUBLANES, :]


_EPI_OUT_SHAPES = lambda t, d: (jax.ShapeDtypeStruct((t, d), F32),
                                jax.ShapeDtypeStruct((t, d // 2 + LANES), jnp.uint32),
                                jax.ShapeDtypeStruct((SUBLANES, t), F32),
                                jax.ShapeDtypeStruct((32, LANES), F32))


def _epi_out_specs(tm, d):
    return (pl.BlockSpec((tm, d), lambda i: (i, 0)),
            pl.BlockSpec((tm, d // 2 + LANES), lambda i: (i, 0)),
            pl.BlockSpec((SUBLANES, tm), lambda i: (0, i)),
            pl.BlockSpec((32, LANES), lambda i: (0, 0)))


def _epi_in_specs(tm, d):
    return [pl.BlockSpec((1, d), lambda i: (0, 0)),
            pl.BlockSpec((32, d), lambda i: (0, 0)),
            pl.BlockSpec((32, 1), lambda i: (0, 0)),
            pl.BlockSpec((tm, tm), lambda i: (0, 0))]


def _outproj_even_kernel(hm_ref, ha_ref, x_ref, wo_ref, g_ref, wr_ref, br_ref, triu_ref,
                         xn_ref, h2_ref, meta_ref, cnt_ref):
    half = hm_ref.shape[1]
    y = _dot(hm_ref[...], wo_ref[:half, :]) + _dot(ha_ref[...], wo_ref[half:, :])
    xn = x_ref[...] + y
    xn_ref[...] = xn
    _router_epilogue(xn, g_ref, wr_ref, br_ref, triu_ref, h2_ref, meta_ref, cnt_ref)


def _outproj_even(hm, ha, x2d, w_out, g, wr, br, triu):
    t, d = x2d.shape
    tm = ROW_TILE
    half = hm.shape[1]
    return pl.pallas_call(
        _outproj_even_kernel,
        out_shape=_EPI_OUT_SHAPES(t, d),
        grid=(t // tm,),
        in_specs=[pl.BlockSpec((tm, half), lambda i: (i, 0)),
                  pl.BlockSpec((tm, half), lambda i: (i, 0)),
                  pl.BlockSpec((tm, d), lambda i: (i, 0)),
                  pl.BlockSpec((2 * half, d), lambda i: (0, 0))] + _epi_in_specs(tm, d),
        out_specs=_epi_out_specs(tm, d),
        compiler_params=_cparams("arbitrary"),
        name="outproj_even",
    )(hm, ha, x2d, w_out, g, wr, br, triu)


def _conv_mixer_kernel(x_ref, y_ref, gm_ref, wi_ref, cw_ref, wo_ref, g_ref, wr_ref, br_ref, triu_ref,
                       xn_ref, h2_ref, meta_ref, cnt_ref, tail_ref, *, tiles_per_seq):
    tm, d = x_ref.shape
    x = x_ref[...] + jnp.concatenate(_unpack_bf16_pairs(y_ref[...]), axis=1)
    h = _rms(x, gm_ref[...]).astype(BF16)

    @pl.when(pl.program_id(0) % tiles_per_seq == 0)
    def _():
        tail_ref[...] = jnp.zeros_like(tail_ref)

    row = lax.broadcasted_iota(jnp.int32, (tm, 512), 0)
    acc = jnp.zeros((tm, d), F32)
    for j in range(d // 512):
        sl = slice(j * 512, (j + 1) * 512)
        bg = _dot(h, wi_ref[:, sl])
        cg = _dot(h, wi_ref[:, d + j * 512:d + (j + 1) * 512])
        u = _dot(h, wi_ref[:, 2 * d + j * 512:2 * d + (j + 1) * 512])
        z = cg * u
        prev1 = tail_ref[SUBLANES - 1:SUBLANES, sl]
        prev2 = tail_ref[SUBLANES - 2:SUBLANES - 1, sl]
        z1 = jnp.where(row == 0, prev1, pltpu.roll(z, 1, axis=0))
        z2 = jnp.where(row == 0, prev2, jnp.where(row == 1, prev1, pltpu.roll(z, 2, axis=0)))
        zc = cw_ref[0:1, sl] * z2 + cw_ref[1:2, sl] * z1 + cw_ref[2:3, sl] * z
        tail_ref[:, sl] = z[tm - SUBLANES:, :]
        acc = acc + _dot((bg * zc).astype(BF16), wo_ref[sl, :])
    xn = x + acc
    xn_ref[...] = xn
    _router_epilogue(xn, g_ref, wr_ref, br_ref, triu_ref, h2_ref, meta_ref, cnt_ref)


def _conv_mixer(x2d, ymoe, gm, w_in, conv_w, w_out, g, wr, br, triu, seq):
    t, d = x2d.shape
    tm = ROW_TILE
    return pl.pallas_call(
        functools.partial(_conv_mixer_kernel, tiles_per_seq=seq // tm),
        out_shape=_EPI_OUT_SHAPES(t, d),
        grid=(t // tm,),
        in_specs=[pl.BlockSpec((tm, d), lambda i: (i, 0)),
                  pl.BlockSpec((tm, d // 2), lambda i: (i, 0)),
                  pl.BlockSpec((1, d), lambda i: (0, 0)),
                  pl.BlockSpec((d, 3 * d), lambda i: (0, 0)),
                  pl.BlockSpec((SUBLANES, d), lambda i: (0, 0)),
                  pl.BlockSpec((d, d), lambda i: (0, 0))] + _epi_in_specs(tm, d),
        out_specs=_epi_out_specs(tm, d),
        scratch_shapes=[pltpu.VMEM((SUBLANES, d), F32)],
        compiler_params=_cparams("arbitrary"),
        name="conv_mixer",
    )(x2d, ymoe, gm, w_in, conv_w, w_out, g, wr, br, triu)


def _row_scatter_kernel(dst_ref, h_hbm, zeros_hbm, out_hbm, buf, lsem, ssem):
    del zeros_hbm
    i = pl.program_id(0)
    n = pl.num_programs(0)
    tm = buf.shape[1]
    slot = i % 3

    def load(j, s):
        return pltpu.make_async_copy(h_hbm.at[pl.ds(pl.multiple_of(j * tm, tm), tm)], buf.at[s], lsem.at[s])

    wait_rows = lambda s: pltpu.make_async_copy(buf.at[s], out_hbm.at[pl.ds(0, tm)], ssem.at[s]).wait()

    @pl.when(i == 0)
    def _():
        load(0, 0).start()

    @pl.when(i + 1 < n)
    def _():
        load(i + 1, (i + 1) % 3).start()

    load(i, slot).wait()
    for r in range(tm):
        pltpu.make_async_copy(buf.at[slot, pl.ds(r, 1)], out_hbm.at[pl.ds(dst_ref[0, 0, r], 1)],
                              ssem.at[slot]).start(priority=r % 2)

    @pl.when(i >= 1)
    def _():
        wait_rows((i + 2) % 3)

    @pl.when(i == n - 1)
    def _():
        wait_rows(slot)


def _row_scatter(hrows, dest, n_rows):
    t, wrow = hrows.shape
    tm = MOE_TILE
    return pl.pallas_call(
        _row_scatter_kernel,
        out_shape=jax.ShapeDtypeStruct((n_rows, wrow), hrows.dtype),
        grid=(t // tm,),
        in_specs=[pl.BlockSpec((1, 1, tm), lambda i: (i, 0, 0), memory_space=pltpu.SMEM),
                  pl.BlockSpec(memory_space=pl.ANY),
                  pl.BlockSpec(memory_space=pl.ANY)],
        out_specs=pl.BlockSpec(memory_space=pl.ANY),
        input_output_aliases={2: 0},
        scratch_shapes=[pltpu.VMEM((3, tm, wrow), hrows.dtype),
                        pltpu.SemaphoreType.DMA((3,)),
                        pltpu.SemaphoreType.DMA((3,))],
        compiler_params=_cparams("arbitrary"),
        name="row_scatter",
    )(dest.reshape(t // tm, 1, tm), hrows, jnp.zeros((n_rows, wrow), hrows.dtype))


def _moe_kernel(elo_ref, ehi_ref, valid_ref, fresh_ref,
                x_ref,
                wg_lo, wu_lo, wd_lo, wg_hi, wu_hi, wd_hi,
                y_hbm,
                obuf0, obuf1, idv0, idv1, ids0, ids1, wbuf_in, wbuf_out, ssem, isem):
    i = pl.program_id(0)
    tm, half = obuf0.shape
    n_tok = y_hbm.shape[0] - tm
    obufs, idvs, idss = (obuf0, obuf1), (idv0, idv1), (ids0, ids1)
    valid = valid_ref[i] != 0
    prev_valid = valid_ref[jnp.maximum(i - 1, 0)] != 0
    spare = n_tok + lax.broadcasted_iota(jnp.int32, (SUBLANES, tm), 1)

    ids_copy = lambda s: pltpu.make_async_copy(idvs[s], idss[s], isem.at[s])
    row_scatter = lambda r, s: pltpu.make_async_copy(obufs[s].at[pl.ds(r, 1)],
                                                     y_hbm.at[pl.ds(idss[s][0, r], 1)], ssem.at[s])
    wait_scatter = lambda s: pltpu.make_async_copy(obufs[s], y_hbm.at[pl.ds(0, tm)], ssem.at[s]).wait()

    @pl.when(i == 0)
    def _():
        obuf1[...] = jnp.zeros_like(obuf1)
        idv1[...] = spare
        ids_copy(1).start()

    @pl.when(fresh_ref[i] != 0)
    def _():
        for k, w in enumerate((wg_lo, wu_lo, wg_hi, wu_hi)):
            wbuf_in[k] = w[0].astype(BF16)
        for k, w in enumerate((wd_lo, wd_hi)):
            wbuf_out[k] = w[0].astype(BF16)

    for s in range(2):
        @pl.when(valid & (i % 2 == s))
        def _(s=s):
            rows = x_ref[...]
            rec = lax.bitcast_convert_type(rows[:, half:], F32)
            token = rec.T[4:5, :]
            idvs[s][...] = jnp.where(token > 0.0, token.astype(jnp.int32) - 1, spare)
            ids_copy(s).start()
            ids_copy(1 - s).wait()
            for r in range(tm):
                row_scatter(r, 1 - s).start(priority=r % 2)
            xa, xb = _unpack_bf16_pairs(rows[:, :half])
            x = jnp.concatenate([xa.astype(BF16), xb.astype(BF16)], axis=1)

            def ffn(k):
                a = _dot(x, wbuf_in[2 * k])
                a = (a * jax.nn.sigmoid(a)) * _dot(x, wbuf_in[2 * k + 1])
                return _dot(a.astype(BF16), wbuf_out[k])

            out = rec[:, 1:2] * ffn(0) + rec[:, 2:3] * ffn(1)
            obufs[s][...] = _pack_bf16_pairs(out.astype(BF16))
            wait_scatter(1 - s)

        @pl.when(jnp.logical_not(valid) & prev_valid & (i % 2 == s))
        def _(s=s):
            ids_copy(1 - s).wait()

            @pl.loop(0, tm)
            def _(r):
                row_scatter(r, 1 - s).start()

            wait_scatter(1 - s)


def _moe(tiles, xs, n_tok, w_gate, w_up, w_down):
    tile_elo, tile_ehi, tile_valid, tile_fresh = tiles
    n_tiles = tile_elo.shape[0]
    wrow = xs.shape[1]
    d = w_gate.shape[1]
    ff = w_gate.shape[2]
    tm = MOE_TILE
    lo = lambda i, elo, *_: (elo[i], 0, 0)
    hi = lambda i, elo, ehi, *_: (ehi[i], 0, 0)
    return pl.pallas_call(
        _moe_kernel,
        out_shape=jax.ShapeDtypeStruct((n_tok + tm, d // 2), jnp.uint32),
        grid_spec=pltpu.PrefetchScalarGridSpec(
            num_scalar_prefetch=4,
            grid=(n_tiles,),
            in_specs=[pl.BlockSpec((tm, wrow), lambda i, *_: (i, 0)),
                      pl.BlockSpec((1, d, ff), lo), pl.BlockSpec((1, d, ff), lo), pl.BlockSpec((1, ff, d), lo),
                      pl.BlockSpec((1, d, ff), hi), pl.BlockSpec((1, d, ff), hi), pl.BlockSpec((1, ff, d), hi)],
            out_specs=pl.BlockSpec(memory_space=pl.ANY),
            scratch_shapes=[pltpu.VMEM((tm, d // 2), jnp.uint32),
                            pltpu.VMEM((tm, d // 2), jnp.uint32),
                            pltpu.VMEM((SUBLANES, tm), jnp.int32),
                            pltpu.VMEM((SUBLANES, tm), jnp.int32),
                            pltpu.SMEM((SUBLANES, tm), jnp.int32),
                            pltpu.SMEM((SUBLANES, tm), jnp.int32),
                            pltpu.VMEM((4, d, ff), BF16),
                            pltpu.VMEM((2, ff, d), BF16),
                            pltpu.SemaphoreType.DMA((2,)),
                            pltpu.SemaphoreType.DMA((2,))]),
        compiler_params=_cparams("arbitrary"),
        name="grouped_moe",
    )(tile_elo, tile_ehi, tile_valid, tile_fresh, xs, w_gate, w_up, w_down, w_gate, w_up, w_down)


def _final_kernel(x_ref, y_ref, g_ref, out_ref):
    out_ref[...] = _rms(x_ref[...] + jnp.concatenate(_unpack_bf16_pairs(y_ref[...]), axis=1), g_ref[...])


def _final(x2d, ymoe, g):
    t, d = x2d.shape
    tm = ROW_TILE
    return pl.pallas_call(
        _final_kernel,
        out_shape=jax.ShapeDtypeStruct((t, d), F32),
        grid=(t // tm,),
        in_specs=[pl.BlockSpec((tm, d), lambda i: (i, 0)),
                  pl.BlockSpec((tm, d // 2), lambda i: (i, 0)),
                  pl.BlockSpec((1, d), lambda i: (0, 0))],
        out_specs=pl.BlockSpec((tm, d), lambda i: (i, 0)),
        compiler_params=_cparams("arbitrary"),
        name="final_norm",
    )(x2d, ymoe, g)


_PAIRS = [(a, b) for a in range(EXPERTS_PER_GROUP) for b in range(a + 1, EXPERTS_PER_GROUP)]
_BUCKET_ELO = np.array([g * EXPERTS_PER_GROUP + _PAIRS[p][0] for g in range(MOE_GROUPS) for p in range(N_PAIRS)], np.int32)
_BUCKET_EHI = np.array([g * EXPERTS_PER_GROUP + _PAIRS[p][1] for g in range(MOE_GROUPS) for p in range(N_PAIRS)], np.int32)


def _moe_layer(hrows, route, cnt, layer, w_gate, w_up, w_down):
    t = hrows.shape[0]
    tm = MOE_TILE
    n_tiles = t // tm + N_BUCKETS
    bucket = route[0].astype(jnp.int32)
    rank = route[3].astype(jnp.int32)
    counts = cnt[:N_BUCKETS, 0].astype(jnp.int32)
    padded = ((counts + tm - 1) // tm) * tm
    ends = jnp.cumsum(padded)
    dest = (ends - padded)[bucket] + rank
    tile_start = jnp.arange(n_tiles, dtype=jnp.int32) * tm
    tile_valid = tile_start < ends[-1]
    last_start = jnp.maximum(ends[-1] - tm, 0)
    tile_bucket = jnp.sum((ends[None, :] <= jnp.minimum(tile_start, last_start)[:, None]).astype(jnp.int32), axis=1)
    prev_bucket = jnp.concatenate([jnp.full((1,), -1, jnp.int32), tile_bucket[:-1]])
    tile_fresh = tile_valid & (tile_bucket != prev_bucket)
    base = layer * N_EXPERTS
    tiles = (jnp.asarray(_BUCKET_ELO)[tile_bucket] + base, jnp.asarray(_BUCKET_EHI)[tile_bucket] + base,
             tile_valid.astype(jnp.int32), tile_fresh.astype(jnp.int32))
    xs = _row_scatter(hrows, dest, n_tiles * tm)
    return _moe(tiles, xs, t, w_gate, w_up, w_down)


def _router_weights(wg, bg, we, be):
    d = wg.shape[0]
    wr = jnp.zeros((32, d), F32).at[:MOE_GROUPS].set(wg.T).at[SUBLANES:SUBLANES + N_EXPERTS].set(we.T)
    br = jnp.zeros((32, 1), F32).at[:MOE_GROUPS, 0].set(bg).at[SUBLANES:SUBLANES + N_EXPERTS, 0].set(be)
    return wr.astype(BF16), br


def kernel(x, rel_bias, norm_mix, norm_ffn, norm_final, w_in_even, w_out_even, m_igate_b, m_fgate_b,
           m_head_norm, lam_q1, lam_k1, lam_q2, lam_k2, da_subln, w_in_odd, conv_w, w_out_odd,
           router_group_w, router_group_b, router_expert_w, router_expert_b,
           exp_w_gate, exp_w_up, exp_w_down):
    bsz, seq, d = x.shape
    t = bsz * seq
    x2d = x.reshape(t, d)
    mw = M_HEADS * M_HEAD_DIM

    w_in = w_in_even[0]
    gate0 = 4 * mw
    w_main = jnp.concatenate([w_in[:, :gate0], w_in[:, gate0 + 2 * M_HEADS:]], axis=1).astype(BF16)
    w_gates = w_in[:, gate0:gate0 + 2 * M_HEADS]
    wg_col = jnp.zeros((d, LANES), F32).at[:, :2 * M_HEADS].set(w_gates).astype(BF16)
    wg_row = w_gates.T.astype(BF16)
    gate_b = jnp.concatenate([m_igate_b[0], m_fgate_b[0]]).astype(F32)
    b_col = jnp.zeros((1, LANES), F32).at[0, :2 * M_HEADS].set(gate_b)
    b_row = gate_b.reshape(2 * M_HEADS, 1)
    triu = jnp.triu(jnp.ones((ROW_TILE, ROW_TILE), F32)).astype(BF16)

    p, gc, gr = _inproj_even(x2d, norm_mix[0].reshape(1, d), w_main, wg_col, wg_row)
    p3 = p.reshape(bsz, seq, -1)
    hm = _mlstm(p3, gc.reshape(bsz, seq, LANES), gr, b_col, b_row, m_head_norm[0].reshape(1, mw))

    lam_init = 0.8 - 0.6 * math.exp(-0.3 * 0)
    lam = (jnp.exp(jnp.sum(lam_q1[0].astype(F32) * lam_k1[0].astype(F32)))
           - jnp.exp(jnp.sum(lam_q2[0].astype(F32) * lam_k2[0].astype(F32))) + lam_init).reshape(1)
    ha = _attention(lam, p3, _bias_tiles(rel_bias.astype(F32), ATT_BLOCK),
                    da_subln[0].reshape(1, DA_V_DIM).astype(F32), 1.0 - lam_init)

    wr0, br0 = _router_weights(router_group_w[0], router_group_b[0], router_expert_w[0], router_expert_b[0])
    x1, h2, meta, cnt = _outproj_even(hm.reshape(t, mw), ha.reshape(t, -1), x2d, w_out_even[0].astype(BF16),
                                      norm_ffn[0].reshape(1, d), wr0, br0, triu)
    ff = exp_w_gate.shape[-1]
    experts = (exp_w_gate.reshape(-1, d, ff), exp_w_up.reshape(-1, d, ff), exp_w_down.reshape(-1, ff, d))
    y = _moe_layer(h2, meta, cnt, 0, *experts)

    wr1, br1 = _router_weights(router_group_w[1], router_group_b[1], router_expert_w[1], router_expert_b[1])
    cw = jnp.zeros((SUBLANES, d), F32).at[:conv_w.shape[1]].set(conv_w[0])
    x3, h2, meta, cnt = _conv_mixer(x1, y, norm_mix[1].reshape(1, d), w_in_odd[0].astype(BF16), cw,
                                    w_out_odd[0].astype(BF16), norm_ffn[1].reshape(1, d), wr1, br1, triu, seq)
    y = _moe_layer(h2, meta, cnt, 1, *experts)

    return _final(x3, y, norm_final.reshape(1, d)).reshape(bsz, seq, d)
```

```python
import functools
import math

import jax
import jax.numpy as jnp
import numpy as np
from jax import lax
from jax.experimental import pallas as pl
from jax.experimental.pallas import tpu as pltpu

F32 = jnp.float32
BF16 = jnp.bfloat16

NORM_EPS = 1e-6
NEG_BIG = -1e30

M_HEADS = 4
M_HEAD_DIM = 128
DA_HEADS = 4
DA_HEAD_DIM = 64
DA_V_DIM = 128
REL_BUCKETS = 32
REL_MAX_DIST = 128
MOE_GROUPS = 4
EXPERTS_PER_GROUP = 4
N_EXPERTS = 16
N_PAIRS = 6
N_BUCKETS = MOE_GROUPS * N_PAIRS

LANES = 128
SUBLANES = 8
VMEM_LIMIT = 56 * 1024 * 1024

ROW_TILE = 512
MLSTM_CHUNK = 256
MLSTM_BATCH = 2
ATT_BLOCK = 256
MOE_TILE = 512

_NT = (((1,), (1,)), ((), ()))


def _cparams(*sem):
    return pltpu.CompilerParams(dimension_semantics=sem, vmem_limit_bytes=VMEM_LIMIT)


def _rms(x, g):
    ms = jnp.mean(x * x, axis=-1, keepdims=True)
    return x * lax.rsqrt(ms + NORM_EPS) * g


def _dot(a, b):
    return jnp.dot(a, b, preferred_element_type=F32)


def _split3(x):
    hi = x.astype(BF16)
    r1 = x - hi.astype(F32)
    mid = r1.astype(BF16)
    lo = (r1 - mid.astype(F32)).astype(BF16)
    return hi, mid, lo


def _log_sigmoid(x):
    return jnp.minimum(x, 0.0) - jnp.log(1.0 + jnp.exp(-jnp.abs(x)))


def _bias_tiles_kernel(rb_ref, out_ref):
    h = pl.program_id(0)
    blk = out_ref.shape[-1]
    r = lax.broadcasted_iota(jnp.int32, (blk, blk), 0)
    c = lax.broadcasted_iota(jnp.int32, (blk, blk), 1)
    far = rb_ref[REL_BUCKETS - 1, h]
    max_exact = REL_BUCKETS // 2
    for j in range(2):
        rel = c - r - j * blk
        n = jnp.maximum(-rel, 0)
        nf = jnp.maximum(n, 1).astype(F32)
        large = max_exact + (jnp.log(nf / max_exact) / math.log(REL_MAX_DIST / max_exact)
                             * (REL_BUCKETS - max_exact)).astype(jnp.int32)
        large = jnp.minimum(large, REL_BUCKETS - 1)
        bucket = jnp.where(n < max_exact, n, large)
        val = jnp.zeros((blk, blk), F32)
        for b in range(REL_BUCKETS):
            val = jnp.where(bucket == b, rb_ref[b, h], val)
        val = val - far
        if j == 0:
            val = jnp.where(rel <= 0, val, NEG_BIG)
        out_ref[0, j] = val


def _bias_tiles(rel_bias, blk):
    return pl.pallas_call(
        _bias_tiles_kernel,
        out_shape=jax.ShapeDtypeStruct((DA_HEADS, 2, blk, blk), F32),
        grid=(DA_HEADS,),
        in_specs=[pl.BlockSpec(memory_space=pltpu.SMEM)],
        out_specs=pl.BlockSpec((1, 2, blk, blk), lambda h: (h, 0, 0, 0)),
        compiler_params=_cparams("arbitrary"),
        name="bias_tiles",
    )(rel_bias)


def _inproj_even_kernel(x_ref, g_ref, w_ref, wgc_ref, wgr_ref, p_ref, gc_ref, gr_ref):
    h = _rms(x_ref[...], g_ref[...]).astype(BF16)
    n = w_ref.shape[1]
    for j in range(n // 512):
        sl = slice(j * 512, (j + 1) * 512)
        p_ref[:, sl] = _dot(h, w_ref[:, sl]).astype(BF16)
    gc_ref[...] = _dot(h, wgc_ref[...])
    gr_ref[...] = lax.dot_general(wgr_ref[...], h, _NT, preferred_element_type=F32)


def _inproj_even(x2d, g, w_main, wg_col, wg_row):
    t, d = x2d.shape
    n = w_main.shape[1]
    tm = ROW_TILE
    return pl.pallas_call(
        _inproj_even_kernel,
        out_shape=(jax.ShapeDtypeStruct((t, n), BF16),
                   jax.ShapeDtypeStruct((t, LANES), F32),
                   jax.ShapeDtypeStruct((SUBLANES, t), F32)),
        grid=(t // tm,),
        in_specs=[pl.BlockSpec((tm, d), lambda i: (i, 0)),
                  pl.BlockSpec((1, d), lambda i: (0, 0)),
                  pl.BlockSpec((d, n), lambda i: (0, 0)),
                  pl.BlockSpec((d, LANES), lambda i: (0, 0)),
                  pl.BlockSpec((SUBLANES, d), lambda i: (0, 0))],
        out_specs=(pl.BlockSpec((tm, n), lambda i: (i, 0)),
                   pl.BlockSpec((tm, LANES), lambda i: (i, 0)),
                   pl.BlockSpec((SUBLANES, tm), lambda i: (0, i))),
        compiler_params=_cparams("arbitrary"),
        name="inproj_even",
    )(x2d, g, w_main, wg_col, wg_row)


def _mlstm_kernel(q_ref, k_ref, v_ref, og_ref, gc_ref, gr0_ref, gr1_ref, bc_ref, br_ref, hn_ref, out_ref,
                  c_ref, m_ref):
    @pl.when(pl.program_id(1) == 0)
    def _():
        c_ref[...] = jnp.zeros_like(c_ref)
        m_ref[...] = jnp.zeros_like(m_ref)

    for bi, gr_ref in enumerate((gr0_ref, gr1_ref)):
        _mlstm_chunk(bi, q_ref, k_ref, v_ref, og_ref, gc_ref, gr_ref, bc_ref, br_ref, hn_ref, out_ref, c_ref, m_ref)


def _mlstm_chunk(bi, q_ref, k_ref, v_ref, og_ref, gc_ref, gr_ref, bc_ref, br_ref, hn_ref, out_ref, c_ref, m_ref):
    L = q_ref.shape[1]
    dh = M_HEAD_DIM
    scale = dh ** -0.5

    gc = gc_ref[bi] + bc_ref[...]
    gr = gr_ref[...] + br_ref[...]
    lf_c = _log_sigmoid(gc)
    lf_r = _log_sigmoid(gr)

    src = lax.broadcasted_iota(jnp.int32, (L, L), 0)
    qry = lax.broadcasted_iota(jnp.int32, (L, L), 1)
    visible = src <= qry
    tril = jnp.where(src >= qry, 1.0, 0.0).astype(BF16)
    triu = jnp.where(visible, 1.0, 0.0).astype(BF16)
    b_c = sum(_dot(tril, part) for part in _split3(lf_c))
    b_r = sum(_dot(part, triu) for part in _split3(lf_r))
    u_c = gc - pltpu.roll(b_c, LANES - M_HEADS, axis=1)
    u_parts = jnp.concatenate(_split3(u_c), axis=1)
    sel_row = lax.broadcasted_iota(jnp.int32, (3 * LANES, L), 0) % LANES

    sub = lax.broadcasted_iota(jnp.int32, (dh, L), 0)
    ones_row = jnp.where(sub == 0, 1.0, 0.0).astype(BF16)

    for h in range(M_HEADS):
        hs = slice(h * dh, (h + 1) * dh)
        st = bi * M_HEADS + h
        q = q_ref[bi, :, hs]
        k = k_ref[bi, :, hs]
        kT = k.astype(F32).T
        vT = v_ref[bi, :, hs].astype(F32).T.astype(BF16)
        i_r = gr[h:h + 1, :]
        b_rh = b_r[M_HEADS + h:M_HEADS + h + 1, :]
        m_prev = m_ref[st:st + 1, 0:1]

        u_bc = _dot(u_parts, jnp.where(sel_row == h, 1.0, 0.0).astype(BF16))
        dT = jnp.where(visible, u_bc + b_rh, NEG_BIG)
        inter = b_rh + m_prev
        mt = jnp.maximum(inter, jnp.max(dT, axis=0, keepdims=True))
        wT = jnp.exp(dT - mt)
        sc = jnp.exp(inter - mt)
        sT = lax.dot_general(k, q, _NT, preferred_element_type=F32)
        qkT = (sT * (wT * scale)).astype(BF16)

        vaugT = jnp.concatenate([vT, ones_row], axis=0)
        caugT = c_ref[st]
        numT = sc * lax.dot_general(caugT.astype(BF16), q, _NT, preferred_element_type=F32) + _dot(vaugT, qkT)
        den = numT[dh:dh + 1, :]
        hT = numT[:dh, :] / jnp.maximum(jnp.abs(den), jnp.exp(-mt))

        mu = jnp.mean(hT, axis=0, keepdims=True)
        cen = hT - mu
        var = jnp.mean(cen * cen, axis=0, keepdims=True)
        hn = (cen * lax.rsqrt(var + NORM_EPS)).T * hn_ref[:, hs]
        gate = jax.nn.sigmoid(og_ref[bi, :, hs].astype(F32))
        out_ref[bi, :, hs] = (hn * gate).astype(out_ref.dtype)

        g_end = b_rh[:, L - 1:L]
        ws = g_end - b_rh + i_r
        m_new = jnp.maximum(g_end + m_prev, jnp.max(ws, axis=1, keepdims=True))
        a = jnp.exp(g_end + m_prev - m_new)
        wkT = (kT * (jnp.exp(ws - m_new) * scale)).astype(BF16)
        c_ref[st] = a * caugT + lax.dot_general(vaugT, wkT, _NT, preferred_element_type=F32)
        m_ref[st:st + 1, :] = jnp.broadcast_to(m_new, (1, LANES))


def _mlstm(p3, gc3, gr, b_col, b_row, head_norm):
    bsz, s, _ = p3.shape
    L = MLSTM_CHUNK
    w = M_HEADS * M_HEAD_DIM
    nc = s // L
    nb = MLSTM_BATCH
    blk = lambda j: pl.BlockSpec((nb, L, w), lambda b, c, j=j: (b, c, j))
    gate_rows = lambda bi: pl.BlockSpec((SUBLANES, L), lambda b, c, bi=bi: (0, (nb * b + bi) * nc + c))
    return pl.pallas_call(
        _mlstm_kernel,
        out_shape=jax.ShapeDtypeStruct((bsz, s, w), BF16),
        grid=(bsz // nb, nc),
        in_specs=[blk(0), blk(1), blk(2), blk(3),
                  pl.BlockSpec((nb, L, LANES), lambda b, c: (b, c, 0)),
                  gate_rows(0), gate_rows(1),
                  pl.BlockSpec((1, LANES), lambda b, c: (0, 0)),
                  pl.BlockSpec((SUBLANES, 1), lambda b, c: (0, 0)),
                  pl.BlockSpec((1, w), lambda b, c: (0, 0))],
        out_specs=pl.BlockSpec((nb, L, w), lambda b, c: (b, c, 0)),
        scratch_shapes=[pltpu.VMEM((nb * M_HEADS, 2 * M_HEAD_DIM, M_HEAD_DIM), F32),
                        pltpu.VMEM((nb * M_HEADS, LANES), F32)],
        compiler_params=_cparams("arbitrary", "arbitrary"),
        name="mlstm",
    )(p3, p3, p3, p3, gc3, gr, gr, b_col, b_row, head_norm)


def _attn_kernel(lam_ref, q_ref, k_ref, v_ref, bias_ref, sub_ref, out_ref, vaug_ref, *, out_scale, blk):
    s_len = q_ref.shape[1]
    d = DA_HEAD_DIM
    lane_s = lax.broadcasted_iota(jnp.int32, (s_len, DA_V_DIM), 1)
    vaug_ref[:, :DA_V_DIM] = v_ref[0]
    vaug_ref[:, DA_V_DIM:] = jnp.where(lane_s == 0, 1.0, 0.0).astype(BF16)
    lam = lam_ref[0]
    lane = lax.broadcasted_iota(jnp.int32, (blk, LANES), 1)
    for qi in range(s_len // blk):
        q0, q1 = qi * blk, (qi + 1) * blk
        qs = q_ref[0, q0:q1, :] * (d ** -0.5)
        zero = jnp.zeros_like(qs)
        heads = []
        for qm in (jnp.where(lane < d, qs, zero), jnp.where(lane >= d, qs, zero)):
            score = lambda a, b: lax.dot_general(qm, k_ref[0, a:b, :], _NT, preferred_element_type=F32)
            parts = [score(q0, q1) + bias_ref[0, 0]]
            if qi >= 1:
                parts.append(score(q0 - blk, q0) + bias_ref[0, 1])
            if qi >= 2:
                parts.append(score(0, q0 - blk))
            m = functools.reduce(jnp.maximum, [jnp.max(p, axis=-1, keepdims=True) for p in parts])
            pexp = [jnp.exp(p - m).astype(BF16) for p in parts]
            o = _dot(pexp[0], vaug_ref[q0:q1, :])
            if qi >= 1:
                o = o + _dot(pexp[1], vaug_ref[q0 - blk:q0, :])
            if qi >= 2:
                o = o + _dot(pexp[2], vaug_ref[0:q0 - blk, :])
            heads.append(o[:, :DA_V_DIM] / o[:, DA_V_DIM:DA_V_DIM + 1])
        o = heads[0] - lam * heads[1]
        ms = jnp.mean(o * o, axis=-1, keepdims=True)
        o = o * lax.rsqrt(ms + NORM_EPS) * sub_ref[...] * out_scale
        out_ref[0, q0:q1, :] = o.astype(out_ref.dtype)


def _attention(lam, p3, bias_tiles, subln, out_scale):
    bsz, s, _ = p3.shape
    blk = ATT_BLOCK
    w = DA_HEADS * DA_V_DIM
    q_blk = 4 * M_HEADS
    k_blk = q_blk + DA_HEADS
    v_blk = k_blk + DA_HEADS
    col = lambda base: pl.BlockSpec((1, s, LANES), lambda b, h: (b, 0, base + h))
    return pl.pallas_call(
        functools.partial(_attn_kernel, out_scale=out_scale, blk=blk),
        out_shape=jax.ShapeDtypeStruct((bsz, s, w), BF16),
        grid=(bsz, DA_HEADS),
        in_specs=[pl.BlockSpec(memory_space=pltpu.SMEM),
                  col(q_blk), col(k_blk), col(v_blk),
                  pl.BlockSpec((1, 2, blk, blk), lambda b, h: (h, 0, 0, 0)),
                  pl.BlockSpec((1, DA_V_DIM), lambda b, h: (0, 0))],
        out_specs=pl.BlockSpec((1, s, DA_V_DIM), lambda b, h: (b, 0, h)),
        scratch_shapes=[pltpu.VMEM((s, 2 * DA_V_DIM), BF16)],
        compiler_params=_cparams("arbitrary", "arbitrary"),
        name="diff_attention",
    )(lam, p3, p3, p3, bias_tiles, subln)


def _pack_bf16_pairs(v):
    half = v.shape[1] // 2
    hi = lax.bitcast_convert_type(v[:, :half].astype(F32), jnp.uint32)
    lo = lax.bitcast_convert_type(v[:, half:].astype(F32), jnp.uint32)
    return hi | (lo >> 16)


def _unpack_bf16_pairs(w):
    hi = lax.bitcast_convert_type(w & jnp.uint32(0xFFFF0000), F32)
    lo = lax.bitcast_convert_type(w << 16, F32)
    return hi, lo


def _router_epilogue(xn, g_ref, wr_ref, br_ref, triu_ref, hrow_ref, route_ref, cnt_ref):
    tm, d = xn.shape
    h2 = _rms(xn, g_ref[...]).astype(BF16)
    hrow_ref[:, :d // 2] = _pack_bf16_pairs(h2)

    lg = lax.dot_general(wr_ref[...], h2, _NT, preferred_element_type=F32) + br_ref[...]
    gl = [lg[g:g + 1, :] for g in range(MOE_GROUPS)]
    gmax = functools.reduce(jnp.maximum, gl)
    gsel = jnp.where(gl[0] == gmax, 0.0, jnp.where(gl[1] == gmax, 1.0, jnp.where(gl[2] == gmax, 2.0, 3.0)))
    pg = 1.0 / functools.reduce(lambda a, b: a + b, [jnp.exp(x - gmax) for x in gl])

    ev = []
    for j in range(EXPERTS_PER_GROUP):
        acc = jnp.zeros_like(gmax)
        for g in range(MOE_GROUPS):
            r0 = SUBLANES + g * EXPERTS_PER_GROUP + j
            acc = jnp.where(gsel == float(g), lg[r0:r0 + 1, :], acc)
        ev.append(acc)
    v1 = functools.reduce(jnp.maximum, ev)
    i1 = jnp.where(ev[0] == v1, 0.0, jnp.where(ev[1] == v1, 1.0, jnp.where(ev[2] == v1, 2.0, 3.0)))
    ev2 = [jnp.where(i1 == float(j), -jnp.inf, ev[j]) for j in range(EXPERTS_PER_GROUP)]
    v2 = functools.reduce(jnp.maximum, ev2)
    hit = [(ev2[j] == v2) & (i1 != float(j)) for j in range(EXPERTS_PER_GROUP)]
    i2 = jnp.where(hit[0], 0.0, jnp.where(hit[1], 1.0, jnp.where(hit[2], 2.0, 3.0)))
    e21 = jnp.exp(v2 - v1)
    w1 = 1.0 / (1.0 + e21)
    w2 = e21 * w1
    first_low = i1 < i2
    lo = jnp.minimum(i1, i2)
    hi = jnp.maximum(i1, i2)
    c_lo = pg * jnp.where(first_low, w1, w2)
    c_hi = pg * jnp.where(first_low, w2, w1)
    pair = lo * (7.0 - lo) * 0.5 + hi - lo - 1.0
    bucket = gsel * float(N_PAIRS) + pair

    @pl.when(pl.program_id(0) == 0)
    def _():
        cnt_ref[...] = jnp.zeros_like(cnt_ref)

    rows = lax.broadcasted_iota(jnp.int32, (32, tm), 0).astype(F32)
    onehot = rows == bucket
    cum = _dot(jnp.where(onehot, 1.0, 0.0).astype(BF16), triu_ref[...])
    carry = cnt_ref[:, 0:1]
    rank = jnp.sum(jnp.where(onehot, cum + carry, 0.0), axis=0, keepdims=True) - 1.0
    cnt_ref[...] = jnp.broadcast_to(carry + cum[:, tm - 1:tm], cnt_ref.shape)

    token = (pl.program_id(0) * tm + 1 + lax.broadcasted_iota(jnp.int32, (1, tm), 1)).astype(F32)
    rid = lax.broadcasted_iota(jnp.int32, (LANES, tm), 0)
    rec = jnp.where(rid == 0, bucket, jnp.where(rid == 1, c_lo, jnp.where(rid == 2, c_hi,
                    jnp.where(rid == 3, rank, jnp.where(rid == 4, token, 0.0)))))
    hrow_ref[:, d // 2:] = lax.bitcast_convert_type(rec.T, jnp.uint32)
    route_ref[...] = rec[:SUBLANES, :]


_EPI_OUT_SHAPES = lambda t, d: (jax.ShapeDtypeStruct((t, d), F32),
                                jax.ShapeDtypeStruct((t, d // 2 + LANES), jnp.uint32),
                                jax.ShapeDtypeStruct((SUBLANES, t), F32),
                                jax.ShapeDtypeStruct((32, LANES), F32))


def _epi_out_specs(tm, d):
    return (pl.BlockSpec((tm, d), lambda i: (i, 0)),
            pl.BlockSpec((tm, d // 2 + LANES), lambda i: (i, 0)),
            pl.BlockSpec((SUBLANES, tm), lambda i: (0, i)),
            pl.BlockSpec((32, LANES), lambda i: (0, 0)))


def _epi_in_specs(tm, d):
    return [pl.BlockSpec((1, d), lambda i: (0, 0)),
            pl.BlockSpec((32, d), lambda i: (0, 0)),
            pl.BlockSpec((32, 1), lambda i: (0, 0)),
            pl.BlockSpec((tm, tm), lambda i: (0, 0))]


def _outproj_even_kernel(hm_ref, ha_ref, x_ref, wo_ref, g_ref, wr_ref, br_ref, triu_ref,
                         xn_ref, h2_ref, meta_ref, cnt_ref):
    half = hm_ref.shape[1]
    y = _dot(hm_ref[...], wo_ref[:half, :]) + _dot(ha_ref[...], wo_ref[half:, :])
    xn = x_ref[...] + y
    xn_ref[...] = xn
    _router_epilogue(xn, g_ref, wr_ref, br_ref, triu_ref, h2_ref, meta_ref, cnt_ref)


def _outproj_even(hm, ha, x2d, w_out, g, wr, br, triu):
    t, d = x2d.shape
    tm = ROW_TILE
    half = hm.shape[1]
    return pl.pallas_call(
        _outproj_even_kernel,
        out_shape=_EPI_OUT_SHAPES(t, d),
        grid=(t // tm,),
        in_specs=[pl.BlockSpec((tm, half), lambda i: (i, 0)),
                  pl.BlockSpec((tm, half), lambda i: (i, 0)),
                  pl.BlockSpec((tm, d), lambda i: (i, 0)),
                  pl.BlockSpec((2 * half, d), lambda i: (0, 0))] + _epi_in_specs(tm, d),
        out_specs=_epi_out_specs(tm, d),
        compiler_params=_cparams("arbitrary"),
        name="outproj_even",
    )(hm, ha, x2d, w_out, g, wr, br, triu)


def _conv_mixer_kernel(x_ref, y_ref, gm_ref, wi_ref, cw_ref, wo_ref, g_ref, wr_ref, br_ref, triu_ref,
                       xn_ref, h2_ref, meta_ref, cnt_ref, tail_ref, *, tiles_per_seq):
    tm, d = x_ref.shape
    x = x_ref[...] + jnp.concatenate(_unpack_bf16_pairs(y_ref[...]), axis=1)
    h = _rms(x, gm_ref[...]).astype(BF16)

    @pl.when(pl.program_id(0) % tiles_per_seq == 0)
    def _():
        tail_ref[...] = jnp.zeros_like(tail_ref)

    row = lax.broadcasted_iota(jnp.int32, (tm, 512), 0)
    acc = jnp.zeros((tm, d), F32)
    for j in range(d // 512):
        sl = slice(j * 512, (j + 1) * 512)
        bg = _dot(h, wi_ref[:, sl])
        cg = _dot(h, wi_ref[:, d + j * 512:d + (j + 1) * 512])
        u = _dot(h, wi_ref[:, 2 * d + j * 512:2 * d + (j + 1) * 512])
        z = cg * u
        prev1 = tail_ref[SUBLANES - 1:SUBLANES, sl]
        prev2 = tail_ref[SUBLANES - 2:SUBLANES - 1, sl]
        z1 = jnp.where(row == 0, prev1, pltpu.roll(z, 1, axis=0))
        z2 = jnp.where(row == 0, prev2, jnp.where(row == 1, prev1, pltpu.roll(z, 2, axis=0)))
        zc = cw_ref[0:1, sl] * z2 + cw_ref[1:2, sl] * z1 + cw_ref[2:3, sl] * z
        tail_ref[:, sl] = z[tm - SUBLANES:, :]
        acc = acc + _dot((bg * zc).astype(BF16), wo_ref[sl, :])
    xn = x + acc
    xn_ref[...] = xn
    _router_epilogue(xn, g_ref, wr_ref, br_ref, triu_ref, h2_ref, meta_ref, cnt_ref)


def _conv_mixer(x2d, ymoe, gm, w_in, conv_w, w_out, g, wr, br, triu, seq):
    t, d = x2d.shape
    tm = ROW_TILE
    return pl.pallas_call(
        functools.partial(_conv_mixer_kernel, tiles_per_seq=seq // tm),
        out_shape=_EPI_OUT_SHAPES(t, d),
        grid=(t // tm,),
        in_specs=[pl.BlockSpec((tm, d), lambda i: (i, 0)),
                  pl.BlockSpec((tm, d // 2), lambda i: (i, 0)),
                  pl.BlockSpec((1, d), lambda i: (0, 0)),
                  pl.BlockSpec((d, 3 * d), lambda i: (0, 0)),
                  pl.BlockSpec((SUBLANES, d), lambda i: (0, 0)),
                  pl.BlockSpec((d, d), lambda i: (0, 0))] + _epi_in_specs(tm, d),
        out_specs=_epi_out_specs(tm, d),
        scratch_shapes=[pltpu.VMEM((SUBLANES, d), F32)],
        compiler_params=_cparams("arbitrary"),
        name="conv_mixer",
    )(x2d, ymoe, gm, w_in, conv_w, w_out, g, wr, br, triu)


def _row_scatter_kernel(ztile_ref, zflag_ref, dst_ref, h_hbm, out_hbm, buf0, buf1, buf2, lsem, ssem, zsem):
    i = pl.program_id(0)
    n = pl.num_programs(0)
    bufs = (buf0, buf1, buf2)
    tm = buf0.shape[0]

    def load(j, s):
        return pltpu.make_async_copy(h_hbm.at[pl.ds(pl.multiple_of(j * tm, tm), tm)], bufs[s], lsem.at[s])

    wait_rows = lambda s: pltpu.make_async_copy(bufs[s], out_hbm.at[pl.ds(0, tm)], ssem.at[s]).wait()

    @pl.when(i == 0)
    def _():
        buf2[...] = jnp.zeros_like(buf2)
        fill = lambda k: pltpu.make_async_copy(
            buf2, out_hbm.at[pl.ds(pl.multiple_of(ztile_ref[k] * tm, tm), tm)], zsem)
        for k in range(ztile_ref.shape[0]):
            @pl.when(zflag_ref[k] != 0)
            def _(k=k):
                fill(k).start()
        for k in range(ztile_ref.shape[0]):
            @pl.when(zflag_ref[k] != 0)
            def _(k=k):
                fill(k).wait()
        load(0, 0).start()

    for s in range(3):
        @pl.when(i % 3 == s)
        def _(s=s):
            @pl.when(i + 1 < n)
            def _():
                load(i + 1, (s + 1) % 3).start()

            load(i, s).wait()
            for r in range(tm):
                pltpu.make_async_copy(bufs[s].at[pl.ds(r, 1)], out_hbm.at[pl.ds(dst_ref[0, 0, r], 1)],
                                      ssem.at[s]).start(priority=r % 2)

            @pl.when(i >= 1)
            def _():
                wait_rows((s + 2) % 3)

            @pl.when(i == n - 1)
            def _():
                wait_rows(s)


def _row_scatter(hrows, dest, zero_tiles, zero_flags, n_rows):
    t, wrow = hrows.shape
    tm = MOE_TILE
    return pl.pallas_call(
        _row_scatter_kernel,
        out_shape=jax.ShapeDtypeStruct((n_rows, wrow), hrows.dtype),
        grid_spec=pltpu.PrefetchScalarGridSpec(
            num_scalar_prefetch=2,
            grid=(t // tm,),
            in_specs=[pl.BlockSpec((1, 1, tm), lambda i, *_: (i, 0, 0), memory_space=pltpu.SMEM),
                      pl.BlockSpec(memory_space=pl.ANY)],
            out_specs=pl.BlockSpec(memory_space=pl.ANY),
            scratch_shapes=[pltpu.VMEM((tm, wrow), hrows.dtype),
                            pltpu.VMEM((tm, wrow), hrows.dtype),
                            pltpu.VMEM((tm, wrow), hrows.dtype),
                            pltpu.SemaphoreType.DMA((3,)),
                            pltpu.SemaphoreType.DMA((3,)),
                            pltpu.SemaphoreType.DMA]),
        compiler_params=_cparams("arbitrary"),
        name="row_scatter",
    )(zero_tiles, zero_flags, dest.reshape(t // tm, 1, tm), hrows)


def _moe_kernel(elo_ref, ehi_ref, valid_ref, fresh_ref,
                x_ref,
                wg_lo, wu_lo, wd_lo, wg_hi, wu_hi, wd_hi,
                y_hbm,
                obuf0, obuf1, idv0, idv1, ids0, ids1, wbuf_in, wbuf_out, ssem, isem):
    i = pl.program_id(0)
    tm, half = obuf0.shape
    n_tok = y_hbm.shape[0] - tm
    obufs, idvs, idss = (obuf0, obuf1), (idv0, idv1), (ids0, ids1)
    valid = valid_ref[i] != 0
    prev_valid = valid_ref[jnp.maximum(i - 1, 0)] != 0
    spare = n_tok + lax.broadcasted_iota(jnp.int32, (SUBLANES, tm), 1)

    ids_copy = lambda s: pltpu.make_async_copy(idvs[s], idss[s], isem.at[s])
    row_scatter = lambda r, s: pltpu.make_async_copy(obufs[s].at[pl.ds(r, 1)],
                                                     y_hbm.at[pl.ds(idss[s][0, r], 1)], ssem.at[s])
    wait_scatter = lambda s: pltpu.make_async_copy(obufs[s], y_hbm.at[pl.ds(0, tm)], ssem.at[s]).wait()

    @pl.when(i == 0)
    def _():
        obuf1[...] = jnp.zeros_like(obuf1)
        idv1[...] = spare
        ids_copy(1).start()

    @pl.when(fresh_ref[i] != 0)
    def _():
        for k, w in enumerate((wg_lo, wu_lo, wg_hi, wu_hi)):
            wbuf_in[k] = w[0].astype(BF16)
        for k, w in enumerate((wd_lo, wd_hi)):
            wbuf_out[k] = w[0].astype(BF16)

    for s in range(2):
        @pl.when(valid & (i % 2 == s))
        def _(s=s):
            rows = x_ref[...]
            rec = lax.bitcast_convert_type(rows[:, half:], F32)
            token = rec.T[4:5, :]
            idvs[s][...] = jnp.where(token > 0.0, token.astype(jnp.int32) - 1, spare)
            ids_copy(s).start()
            ids_copy(1 - s).wait()
            for r in range(tm):
                row_scatter(r, 1 - s).start(priority=r % 2)
            xa, xb = _unpack_bf16_pairs(rows[:, :half])
            x = jnp.concatenate([xa.astype(BF16), xb.astype(BF16)], axis=1)

            def ffn(k):
                a = _dot(x, wbuf_in[2 * k])
                a = (a * jax.nn.sigmoid(a)) * _dot(x, wbuf_in[2 * k + 1])
                return _dot(a.astype(BF16), wbuf_out[k])

            out = rec[:, 1:2] * ffn(0) + rec[:, 2:3] * ffn(1)
            obufs[s][...] = _pack_bf16_pairs(out.astype(BF16))
            wait_scatter(1 - s)

        @pl.when(jnp.logical_not(valid) & prev_valid & (i % 2 == s))
        def _(s=s):
            ids_copy(1 - s).wait()

            @pl.loop(0, tm)
            def _(r):
                row_scatter(r, 1 - s).start()

            wait_scatter(1 - s)


def _moe(tiles, xs, n_tok, w_gate, w_up, w_down):
    tile_elo, tile_ehi, tile_valid, tile_fresh = tiles
    n_tiles = tile_elo.shape[0]
    wrow = xs.shape[1]
    d = w_gate.shape[1]
    ff = w_gate.shape[2]
    tm = MOE_TILE
    lo = lambda i, elo, *_: (elo[i], 0, 0)
    hi = lambda i, elo, ehi, *_: (ehi[i], 0, 0)
    return pl.pallas_call(
        _moe_kernel,
        out_shape=jax.ShapeDtypeStruct((n_tok + tm, d // 2), jnp.uint32),
        grid_spec=pltpu.PrefetchScalarGridSpec(
            num_scalar_prefetch=4,
            grid=(n_tiles,),
            in_specs=[pl.BlockSpec((tm, wrow), lambda i, *_: (i, 0)),
                      pl.BlockSpec((1, d, ff), lo), pl.BlockSpec((1, d, ff), lo), pl.BlockSpec((1, ff, d), lo),
                      pl.BlockSpec((1, d, ff), hi), pl.BlockSpec((1, d, ff), hi), pl.BlockSpec((1, ff, d), hi)],
            out_specs=pl.BlockSpec(memory_space=pl.ANY),
            scratch_shapes=[pltpu.VMEM((tm, d // 2), jnp.uint32),
                            pltpu.VMEM((tm, d // 2), jnp.uint32),
                            pltpu.VMEM((SUBLANES, tm), jnp.int32),
                            pltpu.VMEM((SUBLANES, tm), jnp.int32),
                            pltpu.SMEM((SUBLANES, tm), jnp.int32),
                            pltpu.SMEM((SUBLANES, tm), jnp.int32),
                            pltpu.VMEM((4, d, ff), BF16),
                            pltpu.VMEM((2, ff, d), BF16),
                            pltpu.SemaphoreType.DMA((2,)),
                            pltpu.SemaphoreType.DMA((2,))]),
        compiler_params=_cparams("arbitrary"),
        name="grouped_moe",
    )(tile_elo, tile_ehi, tile_valid, tile_fresh, xs, w_gate, w_up, w_down, w_gate, w_up, w_down)


def _final_kernel(x_ref, y_ref, g_ref, out_ref):
    out_ref[...] = _rms(x_ref[...] + jnp.concatenate(_unpack_bf16_pairs(y_ref[...]), axis=1), g_ref[...])


def _final(x2d, ymoe, g):
    t, d = x2d.shape
    tm = ROW_TILE
    return pl.pallas_call(
        _final_kernel,
        out_shape=jax.ShapeDtypeStruct((t, d), F32),
        grid=(t // tm,),
        in_specs=[pl.BlockSpec((tm, d), lambda i: (i, 0)),
                  pl.BlockSpec((tm, d // 2), lambda i: (i, 0)),
                  pl.BlockSpec((1, d), lambda i: (0, 0))],
        out_specs=pl.BlockSpec((tm, d), lambda i: (i, 0)),
        compiler_params=_cparams("arbitrary"),
        name="final_norm",
    )(x2d, ymoe, g)


_PAIRS = [(a, b) for a in range(EXPERTS_PER_GROUP) for b in range(a + 1, EXPERTS_PER_GROUP)]
_BUCKET_ELO = np.array([g * EXPERTS_PER_GROUP + _PAIRS[p][0] for g in range(MOE_GROUPS) for p in range(N_PAIRS)], np.int32)
_BUCKET_EHI = np.array([g * EXPERTS_PER_GROUP + _PAIRS[p][1] for g in range(MOE_GROUPS) for p in range(N_PAIRS)], np.int32)


def _moe_layer(hrows, route, cnt, layer, w_gate, w_up, w_down):
    t = hrows.shape[0]
    tm = MOE_TILE
    n_tiles = t // tm + N_BUCKETS
    bucket = route[0].astype(jnp.int32)
    rank = route[3].astype(jnp.int32)
    counts = cnt[:N_BUCKETS, 0].astype(jnp.int32)
    padded = ((counts + tm - 1) // tm) * tm
    ends = jnp.cumsum(padded)
    dest = (ends - padded)[bucket] + rank
    tile_start = jnp.arange(n_tiles, dtype=jnp.int32) * tm
    tile_valid = tile_start < ends[-1]
    last_start = jnp.maximum(ends[-1] - tm, 0)
    tile_bucket = jnp.sum((ends[None, :] <= jnp.minimum(tile_start, last_start)[:, None]).astype(jnp.int32), axis=1)
    prev_bucket = jnp.concatenate([jnp.full((1,), -1, jnp.int32), tile_bucket[:-1]])
    tile_fresh = tile_valid & (tile_bucket != prev_bucket)
    base = layer * N_EXPERTS
    tiles = (jnp.asarray(_BUCKET_ELO)[tile_bucket] + base, jnp.asarray(_BUCKET_EHI)[tile_bucket] + base,
             tile_valid.astype(jnp.int32), tile_fresh.astype(jnp.int32))
    idle = t // tm + jnp.arange(N_BUCKETS, dtype=jnp.int32)
    zero_tiles = jnp.concatenate([jnp.maximum(ends // tm - 1, 0), idle])
    zero_flags = jnp.concatenate([counts % tm != 0, idle * tm >= ends[-1]]).astype(jnp.int32)
    xs = _row_scatter(hrows, dest, zero_tiles, zero_flags, n_tiles * tm)
    return _moe(tiles, xs, t, w_gate, w_up, w_down)


def _router_weights(wg, bg, we, be):
    d = wg.shape[0]
    wr = jnp.zeros((32, d), F32).at[:MOE_GROUPS].set(wg.T).at[SUBLANES:SUBLANES + N_EXPERTS].set(we.T)
    br = jnp.zeros((32, 1), F32).at[:MOE_GROUPS, 0].set(bg).at[SUBLANES:SUBLANES + N_EXPERTS, 0].set(be)
    return wr.astype(BF16), br


def kernel(x, rel_bias, norm_mix, norm_ffn, norm_final, w_in_even, w_out_even, m_igate_b, m_fgate_b,
           m_head_norm, lam_q1, lam_k1, lam_q2, lam_k2, da_subln, w_in_odd, conv_w, w_out_odd,
           router_group_w, router_group_b, router_expert_w, router_expert_b,
           exp_w_gate, exp_w_up, exp_w_down):
    bsz, seq, d = x.shape
    t = bsz * seq
    x2d = x.reshape(t, d)
    mw = M_HEADS * M_HEAD_DIM

    w_in = w_in_even[0]
    gate0 = 4 * mw
    w_main = jnp.concatenate([w_in[:, :gate0], w_in[:, gate0 + 2 * M_HEADS:]], axis=1).astype(BF16)
    w_gates = w_in[:, gate0:gate0 + 2 * M_HEADS]
    wg_col = jnp.zeros((d, LANES), F32).at[:, :2 * M_HEADS].set(w_gates).astype(BF16)
    wg_row = w_gates.T.astype(BF16)
    gate_b = jnp.concatenate([m_igate_b[0], m_fgate_b[0]]).astype(F32)
    b_col = jnp.zeros((1, LANES), F32).at[0, :2 * M_HEADS].set(gate_b)
    b_row = gate_b.reshape(2 * M_HEADS, 1)
    triu = jnp.triu(jnp.ones((ROW_TILE, ROW_TILE), F32)).astype(BF16)

    p, gc, gr = _inproj_even(x2d, norm_mix[0].reshape(1, d), w_main, wg_col, wg_row)
    p3 = p.reshape(bsz, seq, -1)
    hm = _mlstm(p3, gc.reshape(bsz, seq, LANES), gr, b_col, b_row, m_head_norm[0].reshape(1, mw))

    lam_init = 0.8 - 0.6 * math.exp(-0.3 * 0)
    lam = (jnp.exp(jnp.sum(lam_q1[0].astype(F32) * lam_k1[0].astype(F32)))
           - jnp.exp(jnp.sum(lam_q2[0].astype(F32) * lam_k2[0].astype(F32))) + lam_init).reshape(1)
    ha = _attention(lam, p3, _bias_tiles(rel_bias.astype(F32), ATT_BLOCK),
                    da_subln[0].reshape(1, DA_V_DIM).astype(F32), 1.0 - lam_init)

    wr0, br0 = _router_weights(router_group_w[0], router_group_b[0], router_expert_w[0], router_expert_b[0])
    x1, h2, meta, cnt = _outproj_even(hm.reshape(t, mw), ha.reshape(t, -1), x2d, w_out_even[0].astype(BF16),
                                      norm_ffn[0].reshape(1, d), wr0, br0, triu)
    ff = exp_w_gate.shape[-1]
    experts = (exp_w_gate.reshape(-1, d, ff), exp_w_up.reshape(-1, d, ff), exp_w_down.reshape(-1, ff, d))
    y = _moe_layer(h2, meta, cnt, 0, *experts)

    wr1, br1 = _router_weights(router_group_w[1], router_group_b[1], router_expert_w[1], router_expert_b[1])
    cw = jnp.zeros((SUBLANES, d), F32).at[:conv_w.shape[1]].set(conv_w[0])
    x3, h2, meta, cnt = _conv_mixer(x1, y, norm_mix[1].reshape(1, d), w_in_odd[0].astype(BF16), cw,
                                    w_out_odd[0].astype(BF16), norm_ffn[1].reshape(1, d), wr1, br1, triu, seq)
    y = _moe_layer(h2, meta, cnt, 1, *experts)

    return _final(x3, y, norm_final.reshape(1, d)).reshape(bsz, seq, d)
```

```python
import functools
import math

import jax
import jax.numpy as jnp
import numpy as np
from jax import lax
from jax.experimental import pallas as pl
from jax.experimental.pallas import tpu as pltpu

F32 = jnp.float32
BF16 = jnp.bfloat16

NORM_EPS = 1e-6
NEG_BIG = -1e30

M_HEADS = 4
M_HEAD_DIM = 128
DA_HEADS = 4
DA_HEAD_DIM = 64
DA_V_DIM = 128
REL_BUCKETS = 32
REL_MAX_DIST = 128
MOE_GROUPS = 4
EXPERTS_PER_GROUP = 4
N_EXPERTS = 16
N_PAIRS = 6
N_BUCKETS = MOE_GROUPS * N_PAIRS

LANES = 128
SUBLANES = 8
VMEM_LIMIT = 56 * 1024 * 1024

ROW_TILE = 512
OUTPROJ_SUBTILES = 2
INPROJ_ROWS = 1024
CONV_SUBTILES = 2
MLSTM_CHUNK = 256
MLSTM_BATCH = 2
ATT_BLOCK = 256
MOE_TILE = 512

_NT = (((1,), (1,)), ((), ()))


def _cparams(*sem):
    return pltpu.CompilerParams(dimension_semantics=sem, vmem_limit_bytes=VMEM_LIMIT)


def _rms(x, g):
    ms = jnp.mean(x * x, axis=-1, keepdims=True)
    return x * lax.rsqrt(ms + NORM_EPS) * g


def _dot(a, b):
    return jnp.dot(a, b, preferred_element_type=F32)


def _split3(x):
    hi = x.astype(BF16)
    r1 = x - hi.astype(F32)
    mid = r1.astype(BF16)
    lo = (r1 - mid.astype(F32)).astype(BF16)
    return hi, mid, lo


def _log_sigmoid(x):
    return jnp.minimum(x, 0.0) - jnp.log(1.0 + jnp.exp(-jnp.abs(x)))


def _bias_tiles_kernel(rb_ref, out_ref):
    h = pl.program_id(0)
    blk = out_ref.shape[-1]
    r = lax.broadcasted_iota(jnp.int32, (blk, blk), 0)
    c = lax.broadcasted_iota(jnp.int32, (blk, blk), 1)
    far = rb_ref[REL_BUCKETS - 1, h]
    max_exact = REL_BUCKETS // 2
    for j in range(2):
        rel = c - r - j * blk
        n = jnp.maximum(-rel, 0)
        nf = jnp.maximum(n, 1).astype(F32)
        large = max_exact + (jnp.log(nf / max_exact) / math.log(REL_MAX_DIST / max_exact)
                             * (REL_BUCKETS - max_exact)).astype(jnp.int32)
        large = jnp.minimum(large, REL_BUCKETS - 1)
        bucket = jnp.where(n < max_exact, n, large)
        val = jnp.zeros((blk, blk), F32)
        for b in range(REL_BUCKETS):
            val = jnp.where(bucket == b, rb_ref[b, h], val)
        val = val - far
        if j == 0:
            val = jnp.where(rel <= 0, val, NEG_BIG)
        out_ref[0, j] = val


def _bias_tiles(rel_bias, blk):
    return pl.pallas_call(
        _bias_tiles_kernel,
        out_shape=jax.ShapeDtypeStruct((DA_HEADS, 2, blk, blk), F32),
        grid=(DA_HEADS,),
        in_specs=[pl.BlockSpec(memory_space=pltpu.SMEM)],
        out_specs=pl.BlockSpec((1, 2, blk, blk), lambda h: (h, 0, 0, 0)),
        compiler_params=_cparams("arbitrary"),
        name="bias_tiles",
    )(rel_bias)


def _inproj_even_kernel(x_ref, g_ref, w_ref, wgc_ref, wgr_ref, p_ref, gc_ref, gr_ref):
    h = _rms(x_ref[...], g_ref[...]).astype(BF16)
    n = w_ref.shape[1]
    for j in range(n // 512):
        sl = slice(j * 512, (j + 1) * 512)
        p_ref[:, sl] = _dot(h, w_ref[:, sl]).astype(BF16)
    gc_ref[...] = _dot(h, wgc_ref[...])
    gr_ref[...] = lax.dot_general(wgr_ref[...], h, _NT, preferred_element_type=F32)


def _inproj_even(x2d, g, w_main, wg_col, wg_row):
    t, d = x2d.shape
    n = w_main.shape[1]
    tm = INPROJ_ROWS
    return pl.pallas_call(
        _inproj_even_kernel,
        out_shape=(jax.ShapeDtypeStruct((t, n), BF16),
                   jax.ShapeDtypeStruct((t, LANES), F32),
                   jax.ShapeDtypeStruct((SUBLANES, t), F32)),
        grid=(t // tm,),
        in_specs=[pl.BlockSpec((tm, d), lambda i: (i, 0)),
                  pl.BlockSpec((1, d), lambda i: (0, 0)),
                  pl.BlockSpec((d, n), lambda i: (0, 0)),
                  pl.BlockSpec((d, LANES), lambda i: (0, 0)),
                  pl.BlockSpec((SUBLANES, d), lambda i: (0, 0))],
        out_specs=(pl.BlockSpec((tm, n), lambda i: (i, 0)),
                   pl.BlockSpec((tm, LANES), lambda i: (i, 0)),
                   pl.BlockSpec((SUBLANES, tm), lambda i: (0, i))),
        compiler_params=_cparams("arbitrary"),
        name="inproj_even",
    )(x2d, g, w_main, wg_col, wg_row)


def _mlstm_kernel(q_ref, k_ref, v_ref, og_ref, gc_ref, gr0_ref, gr1_ref, bc_ref, br_ref, hn_ref, out_ref,
                  c_ref, m_ref):
    @pl.when(pl.program_id(1) == 0)
    def _():
        c_ref[...] = jnp.zeros_like(c_ref)
        m_ref[...] = jnp.zeros_like(m_ref)

    for bi, gr_ref in enumerate((gr0_ref, gr1_ref)):
        _mlstm_chunk(bi, q_ref, k_ref, v_ref, og_ref, gc_ref, gr_ref, bc_ref, br_ref, hn_ref, out_ref, c_ref, m_ref)


def _mlstm_chunk(bi, q_ref, k_ref, v_ref, og_ref, gc_ref, gr_ref, bc_ref, br_ref, hn_ref, out_ref, c_ref, m_ref):
    L = q_ref.shape[1]
    dh = M_HEAD_DIM
    scale = dh ** -0.5

    gc = gc_ref[bi] + bc_ref[...]
    gr = gr_ref[...] + br_ref[...]
    lf_c = _log_sigmoid(gc)
    lf_r = _log_sigmoid(gr)

    src = lax.broadcasted_iota(jnp.int32, (L, L), 0)
    qry = lax.broadcasted_iota(jnp.int32, (L, L), 1)
    visible = src <= qry
    tril = jnp.where(src >= qry, 1.0, 0.0).astype(BF16)
    triu = jnp.where(visible, 1.0, 0.0).astype(BF16)
    b_c = sum(_dot(tril, part) for part in _split3(lf_c))
    b_r = sum(_dot(part, triu) for part in _split3(lf_r))
    u_c = gc - pltpu.roll(b_c, LANES - M_HEADS, axis=1)
    u_parts = jnp.concatenate(_split3(u_c), axis=1)
    sel_row = lax.broadcasted_iota(jnp.int32, (3 * LANES, L), 0) % LANES

    sub = lax.broadcasted_iota(jnp.int32, (dh, L), 0)
    ones_row = jnp.where(sub == 0, 1.0, 0.0).astype(BF16)

    for h in range(M_HEADS):
        hs = slice(h * dh, (h + 1) * dh)
        st = bi * M_HEADS + h
        q = q_ref[bi, :, hs]
        k = k_ref[bi, :, hs]
        kT = k.astype(F32).T
        vT = v_ref[bi, :, hs].astype(F32).T.astype(BF16)
        i_r = gr[h:h + 1, :]
        b_rh = b_r[M_HEADS + h:M_HEADS + h + 1, :]
        m_prev = m_ref[st:st + 1, 0:1]

        u_bc = _dot(u_parts, jnp.where(sel_row == h, 1.0, 0.0).astype(BF16))
        dT = jnp.where(visible, u_bc + b_rh, NEG_BIG)
        inter = b_rh + m_prev
        mt = jnp.maximum(inter, jnp.max(dT, axis=0, keepdims=True))
        wT = jnp.exp(dT - mt)
        sc = jnp.exp(inter - mt)
        sT = lax.dot_general(k, q, _NT, preferred_element_type=F32)
        qkT = (sT * (wT * scale)).astype(BF16)

        vaugT = jnp.concatenate([vT, ones_row], axis=0)
        caugT = c_ref[st]
        numT = sc * lax.dot_general(caugT.astype(BF16), q, _NT, preferred_element_type=F32) + _dot(vaugT, qkT)
        den = numT[dh:dh + 1, :]
        hT = numT[:dh, :] / jnp.maximum(jnp.abs(den), jnp.exp(-mt))

        mu = jnp.mean(hT, axis=0, keepdims=True)
        cen = hT - mu
        var = jnp.mean(cen * cen, axis=0, keepdims=True)
        hn = (cen * lax.rsqrt(var + NORM_EPS)).T * hn_ref[:, hs]
        gate = jax.nn.sigmoid(og_ref[bi, :, hs].astype(F32))
        out_ref[bi, :, hs] = (hn * gate).astype(out_ref.dtype)

        g_end = b_rh[:, L - 1:L]
        ws = g_end - b_rh + i_r
        m_new = jnp.maximum(g_end + m_prev, jnp.max(ws, axis=1, keepdims=True))
        a = jnp.exp(g_end + m_prev - m_new)
        wkT = (kT * (jnp.exp(ws - m_new) * scale)).astype(BF16)
        c_ref[st] = a * caugT + lax.dot_general(vaugT, wkT, _NT, preferred_element_type=F32)
        m_ref[st:st + 1, :] = jnp.broadcast_to(m_new, (1, LANES))


def _mlstm(p3, gc3, gr, b_col, b_row, head_norm):
    bsz, s, _ = p3.shape
    L = MLSTM_CHUNK
    w = M_HEADS * M_HEAD_DIM
    nc = s // L
    nb = MLSTM_BATCH
    blk = lambda j: pl.BlockSpec((nb, L, w), lambda b, c, j=j: (b, c, j))
    gate_rows = lambda bi: pl.BlockSpec((SUBLANES, L), lambda b, c, bi=bi: (0, (nb * b + bi) * nc + c))
    return pl.pallas_call(
        _mlstm_kernel,
        out_shape=jax.ShapeDtypeStruct((bsz, s, w), BF16),
        grid=(bsz // nb, nc),
        in_specs=[blk(0), blk(1), blk(2), blk(3),
                  pl.BlockSpec((nb, L, LANES), lambda b, c: (b, c, 0)),
                  gate_rows(0), gate_rows(1),
                  pl.BlockSpec((1, LANES), lambda b, c: (0, 0)),
                  pl.BlockSpec((SUBLANES, 1), lambda b, c: (0, 0)),
                  pl.BlockSpec((1, w), lambda b, c: (0, 0))],
        out_specs=pl.BlockSpec((nb, L, w), lambda b, c: (b, c, 0)),
        scratch_shapes=[pltpu.VMEM((nb * M_HEADS, 2 * M_HEAD_DIM, M_HEAD_DIM), F32),
                        pltpu.VMEM((nb * M_HEADS, LANES), F32)],
        compiler_params=_cparams("arbitrary", "arbitrary"),
        name="mlstm",
    )(p3, p3, p3, p3, gc3, gr, gr, b_col, b_row, head_norm)


def _attn_kernel(lam_ref, q_ref, k_ref, v_ref, bias_ref, sub_ref, out_ref, vaug_ref, *, out_scale, blk):
    s_len = q_ref.shape[1]
    d = DA_HEAD_DIM
    lane_s = lax.broadcasted_iota(jnp.int32, (s_len, DA_V_DIM), 1)
    vaug_ref[:, :DA_V_DIM] = v_ref[0]
    vaug_ref[:, DA_V_DIM:] = jnp.where(lane_s == 0, 1.0, 0.0).astype(BF16)
    lam = lam_ref[0]
    lane = lax.broadcasted_iota(jnp.int32, (blk, LANES), 1)
    for qi in range(s_len // blk):
        q0, q1 = qi * blk, (qi + 1) * blk
        qs = q_ref[0, q0:q1, :] * (d ** -0.5)
        zero = jnp.zeros_like(qs)
        heads = []
        for qm in (jnp.where(lane < d, qs, zero), jnp.where(lane >= d, qs, zero)):
            score = lambda a, b: lax.dot_general(qm, k_ref[0, a:b, :], _NT, preferred_element_type=F32)
            parts = [score(q0, q1) + bias_ref[0, 0]]
            if qi >= 1:
                parts.append(score(q0 - blk, q0) + bias_ref[0, 1])
            if qi >= 2:
                parts.append(score(0, q0 - blk))
            m = functools.reduce(jnp.maximum, [jnp.max(p, axis=-1, keepdims=True) for p in parts])
            pexp = [jnp.exp(p - m).astype(BF16) for p in parts]
            o = _dot(pexp[0], vaug_ref[q0:q1, :])
            if qi >= 1:
                o = o + _dot(pexp[1], vaug_ref[q0 - blk:q0, :])
            if qi >= 2:
                o = o + _dot(pexp[2], vaug_ref[0:q0 - blk, :])
            heads.append(o[:, :DA_V_DIM] / o[:, DA_V_DIM:DA_V_DIM + 1])
        o = heads[0] - lam * heads[1]
        ms = jnp.mean(o * o, axis=-1, keepdims=True)
        o = o * lax.rsqrt(ms + NORM_EPS) * sub_ref[...] * out_scale
        out_ref[0, q0:q1, :] = o.astype(out_ref.dtype)


def _attention(lam, p3, bias_tiles, subln, out_scale):
    bsz, s, _ = p3.shape
    blk = ATT_BLOCK
    w = DA_HEADS * DA_V_DIM
    q_blk = 4 * M_HEADS
    k_blk = q_blk + DA_HEADS
    v_blk = k_blk + DA_HEADS
    col = lambda base: pl.BlockSpec((1, s, LANES), lambda b, h: (b, 0, base + h))
    return pl.pallas_call(
        functools.partial(_attn_kernel, out_scale=out_scale, blk=blk),
        out_shape=jax.ShapeDtypeStruct((bsz, s, w), BF16),
        grid=(bsz, DA_HEADS),
        in_specs=[pl.BlockSpec(memory_space=pltpu.SMEM),
                  col(q_blk), col(k_blk), col(v_blk),
                  pl.BlockSpec((1, 2, blk, blk), lambda b, h: (h, 0, 0, 0)),
                  pl.BlockSpec((1, DA_V_DIM), lambda b, h: (0, 0))],
        out_specs=pl.BlockSpec((1, s, DA_V_DIM), lambda b, h: (b, 0, h)),
        scratch_shapes=[pltpu.VMEM((s, 2 * DA_V_DIM), BF16)],
        compiler_params=_cparams("arbitrary", "arbitrary"),
        name="diff_attention",
    )(lam, p3, p3, p3, bias_tiles, subln)


def _pack_bf16_pairs(v):
    half = v.shape[1] // 2
    hi = lax.bitcast_convert_type(v[:, :half].astype(F32), jnp.uint32)
    lo = lax.bitcast_convert_type(v[:, half:].astype(F32), jnp.uint32)
    return hi | (lo >> 16)


def _unpack_bf16_pairs(w):
    hi = lax.bitcast_convert_type(w & jnp.uint32(0xFFFF0000), F32)
    lo = lax.bitcast_convert_type(w << 16, F32)
    return hi, lo


def _init_counts(cnt_ref):
    @pl.when(pl.program_id(0) == 0)
    def _():
        cnt_ref[...] = jnp.zeros_like(cnt_ref)


def _router_epilogue(xn, tile, g_ref, wr_ref, br_ref, triu_ref, hrow_ref, route_ref, cnt_ref):
    tm, d = xn.shape
    h2 = _rms(xn, g_ref[...]).astype(BF16)
    hrow_ref[:, :d // 2] = _pack_bf16_pairs(h2)

    lg = lax.dot_general(wr_ref[...], h2, _NT, preferred_element_type=F32) + br_ref[...]
    gl = [lg[g:g + 1, :] for g in range(MOE_GROUPS)]
    gmax = functools.reduce(jnp.maximum, gl)
    gsel = jnp.where(gl[0] == gmax, 0.0, jnp.where(gl[1] == gmax, 1.0, jnp.where(gl[2] == gmax, 2.0, 3.0)))
    pg = 1.0 / functools.reduce(lambda a, b: a + b, [jnp.exp(x - gmax) for x in gl])

    ev = []
    for j in range(EXPERTS_PER_GROUP):
        acc = jnp.zeros_like(gmax)
        for g in range(MOE_GROUPS):
            r0 = SUBLANES + g * EXPERTS_PER_GROUP + j
            acc = jnp.where(gsel == float(g), lg[r0:r0 + 1, :], acc)
        ev.append(acc)
    v1 = functools.reduce(jnp.maximum, ev)
    i1 = jnp.where(ev[0] == v1, 0.0, jnp.where(ev[1] == v1, 1.0, jnp.where(ev[2] == v1, 2.0, 3.0)))
    ev2 = [jnp.where(i1 == float(j), -jnp.inf, ev[j]) for j in range(EXPERTS_PER_GROUP)]
    v2 = functools.reduce(jnp.maximum, ev2)
    hit = [(ev2[j] == v2) & (i1 != float(j)) for j in range(EXPERTS_PER_GROUP)]
    i2 = jnp.where(hit[0], 0.0, jnp.where(hit[1], 1.0, jnp.where(hit[2], 2.0, 3.0)))
    e21 = jnp.exp(v2 - v1)
    w1 = 1.0 / (1.0 + e21)
    w2 = e21 * w1
    first_low = i1 < i2
    lo = jnp.minimum(i1, i2)
    hi = jnp.maximum(i1, i2)
    c_lo = pg * jnp.where(first_low, w1, w2)
    c_hi = pg * jnp.where(first_low, w2, w1)
    pair = lo * (7.0 - lo) * 0.5 + hi - lo - 1.0
    bucket = gsel * float(N_PAIRS) + pair

    rows = lax.broadcasted_iota(jnp.int32, (32, tm), 0).astype(F32)
    onehot = rows == bucket
    cum = _dot(jnp.where(onehot, 1.0, 0.0).astype(BF16), triu_ref[...])
    carry = cnt_ref[:, 0:1]
    rank = jnp.sum(jnp.where(onehot, cum + carry, 0.0), axis=0, keepdims=True) - 1.0
    cnt_ref[...] = jnp.broadcast_to(carry + cum[:, tm - 1:tm], cnt_ref.shape)

    token = (tile * tm + 1 + lax.broadcasted_iota(jnp.int32, (1, tm), 1)).astype(F32)
    rid = lax.broadcasted_iota(jnp.int32, (LANES, tm), 0)
    rec = jnp.where(rid == 0, bucket, jnp.where(rid == 1, c_lo, jnp.where(rid == 2, c_hi,
                    jnp.where(rid == 3, rank, jnp.where(rid == 4, token, 0.0)))))
    hrow_ref[:, d // 2:] = lax.bitcast_convert_type(rec.T, jnp.uint32)
    route_ref[...] = rec[:SUBLANES, :]


_EPI_OUT_SHAPES = lambda t, d: (jax.ShapeDtypeStruct((t, d), F32),
                                jax.ShapeDtypeStruct((t, d // 2 + LANES), jnp.uint32),
                                jax.ShapeDtypeStruct((SUBLANES, t), F32),
                                jax.ShapeDtypeStruct((32, LANES), F32))


def _epi_out_specs(tm, d):
    return (pl.BlockSpec((tm, d), lambda i: (i, 0)),
            pl.BlockSpec((tm, d // 2 + LANES), lambda i: (i, 0)),
            pl.BlockSpec((SUBLANES, tm), lambda i: (0, i)),
            pl.BlockSpec((32, LANES), lambda i: (0, 0)))


def _epi_in_specs(tm, d):
    return [pl.BlockSpec((1, d), lambda i: (0, 0)),
            pl.BlockSpec((32, d), lambda i: (0, 0)),
            pl.BlockSpec((32, 1), lambda i: (0, 0)),
            pl.BlockSpec((tm, tm), lambda i: (0, 0))]


def _outproj_even_kernel(hm_ref, ha_ref, x_ref, wo_ref, g_ref, wr_ref, br_ref, triu_ref,
                         xn_ref, hrow_ref, route_ref, cnt_ref):
    _init_counts(cnt_ref)
    half = hm_ref.shape[1]
    tm = triu_ref.shape[0]
    sub = x_ref.shape[0] // tm
    for j in range(sub):
        rs = slice(j * tm, (j + 1) * tm)
        y = _dot(hm_ref[rs, :], wo_ref[:half, :]) + _dot(ha_ref[rs, :], wo_ref[half:, :])
        xn = x_ref[rs, :] + y
        xn_ref[rs, :] = xn
        _router_epilogue(xn, pl.program_id(0) * sub + j, g_ref, wr_ref, br_ref, triu_ref,
                         hrow_ref.at[rs, :], route_ref.at[:, rs], cnt_ref)


def _outproj_even(hm, ha, x2d, w_out, g, wr, br, triu):
    t, d = x2d.shape
    tm = OUTPROJ_SUBTILES * ROW_TILE
    half = hm.shape[1]
    return pl.pallas_call(
        _outproj_even_kernel,
        out_shape=_EPI_OUT_SHAPES(t, d),
        grid=(t // tm,),
        in_specs=[pl.BlockSpec((tm, half), lambda i: (i, 0)),
                  pl.BlockSpec((tm, half), lambda i: (i, 0)),
                  pl.BlockSpec((tm, d), lambda i: (i, 0)),
                  pl.BlockSpec((2 * half, d), lambda i: (0, 0))] + _epi_in_specs(ROW_TILE, d),
        out_specs=_epi_out_specs(tm, d),
        compiler_params=_cparams("arbitrary"),
        name="outproj_even",
    )(hm, ha, x2d, w_out, g, wr, br, triu)


def _conv_mixer_kernel(x_ref, y_ref, gm_ref, wi_ref, cw_ref, wo_ref, g_ref, wr_ref, br_ref, triu_ref,
                       xn_ref, h2_ref, meta_ref, cnt_ref, tail_ref, *, tiles_per_seq):
    _init_counts(cnt_ref)
    tm = triu_ref.shape[0]
    d = x_ref.shape[1]
    sub = x_ref.shape[0] // tm
    assert tiles_per_seq % sub == 0

    @pl.when((pl.program_id(0) * sub) % tiles_per_seq == 0)
    def _():
        tail_ref[...] = jnp.zeros_like(tail_ref)

    row = lax.broadcasted_iota(jnp.int32, (tm, 512), 0)
    for s in range(sub):
        rs = slice(s * tm, (s + 1) * tm)
        x = x_ref[rs, :] + jnp.concatenate(_unpack_bf16_pairs(y_ref[rs, :]), axis=1)
        h = _rms(x, gm_ref[...]).astype(BF16)
        acc = jnp.zeros((tm, d), F32)
        for j in range(d // 512):
            sl = slice(j * 512, (j + 1) * 512)
            bg = _dot(h, wi_ref[:, sl])
            cg = _dot(h, wi_ref[:, d + j * 512:d + (j + 1) * 512])
            u = _dot(h, wi_ref[:, 2 * d + j * 512:2 * d + (j + 1) * 512])
            z = cg * u
            prev1 = tail_ref[SUBLANES - 1:SUBLANES, sl]
            prev2 = tail_ref[SUBLANES - 2:SUBLANES - 1, sl]
            z1 = jnp.where(row == 0, prev1, pltpu.roll(z, 1, axis=0))
            z2 = jnp.where(row == 0, prev2, jnp.where(row == 1, prev1, pltpu.roll(z, 2, axis=0)))
            zc = cw_ref[0:1, sl] * z2 + cw_ref[1:2, sl] * z1 + cw_ref[2:3, sl] * z
            tail_ref[:, sl] = z[tm - SUBLANES:, :]
            acc = acc + _dot((bg * zc).astype(BF16), wo_ref[sl, :])
        xn = x + acc
        xn_ref[rs, :] = xn
        _router_epilogue(xn, pl.program_id(0) * sub + s, g_ref, wr_ref, br_ref, triu_ref,
                         h2_ref.at[rs, :], meta_ref.at[:, rs], cnt_ref)


def _conv_mixer(x2d, ymoe, gm, w_in, conv_w, w_out, g, wr, br, triu, seq):
    t, d = x2d.shape
    tm = CONV_SUBTILES * ROW_TILE
    const = lambda shape: pl.BlockSpec(shape, lambda i: (0, 0), pipeline_mode=pl.Buffered(1))
    return pl.pallas_call(
        functools.partial(_conv_mixer_kernel, tiles_per_seq=seq // ROW_TILE),
        out_shape=_EPI_OUT_SHAPES(t, d),
        grid=(t // tm,),
        in_specs=[pl.BlockSpec((tm, d), lambda i: (i, 0)),
                  pl.BlockSpec((tm, d // 2), lambda i: (i, 0)),
                  pl.BlockSpec((1, d), lambda i: (0, 0)),
                  const((d, 3 * d)),
                  pl.BlockSpec((SUBLANES, d), lambda i: (0, 0)),
                  const((d, d))] + _epi_in_specs(ROW_TILE, d),
        out_specs=_epi_out_specs(tm, d),
        scratch_shapes=[pltpu.VMEM((SUBLANES, d), F32)],
        compiler_params=_cparams("arbitrary"),
        name="conv_mixer",
    )(x2d, ymoe, gm, w_in, conv_w, w_out, g, wr, br, triu)


def _row_scatter_kernel(ztile_ref, zflag_ref, dst_ref, h_hbm, out_hbm, buf0, buf1, buf2, lsem, ssem, zsem):
    i = pl.program_id(0)
    n = pl.num_programs(0)
    bufs = (buf0, buf1, buf2)
    tm = buf0.shape[0]

    def load(j, s):
        return pltpu.make_async_copy(h_hbm.at[pl.ds(pl.multiple_of(j * tm, tm), tm)], bufs[s], lsem.at[s])

    wait_rows = lambda s: pltpu.make_async_copy(bufs[s], out_hbm.at[pl.ds(0, tm)], ssem.at[s]).wait()

    @pl.when(i == 0)
    def _():
        buf2[...] = jnp.zeros_like(buf2)
        fill = lambda k: pltpu.make_async_copy(
            buf2, out_hbm.at[pl.ds(pl.multiple_of(ztile_ref[k] * tm, tm), tm)], zsem)
        for k in range(ztile_ref.shape[0]):
            @pl.when(zflag_ref[k] != 0)
            def _(k=k):
                fill(k).start()
        for k in range(ztile_ref.shape[0]):
            @pl.when(zflag_ref[k] != 0)
            def _(k=k):
                fill(k).wait()
        load(0, 0).start()

    for s in range(3):
        @pl.when(i % 3 == s)
        def _(s=s):
            @pl.when(i + 1 < n)
            def _():
                load(i + 1, (s + 1) % 3).start()

            load(i, s).wait()
            for r in range(tm):
                pltpu.make_async_copy(bufs[s].at[pl.ds(r, 1)], out_hbm.at[pl.ds(dst_ref[0, 0, r], 1)],
                                      ssem.at[s]).start(priority=r % 2)

            @pl.when(i >= 1)
            def _():
                wait_rows((s + 2) % 3)

            @pl.when(i == n - 1)
            def _():
                wait_rows(s)


def _row_scatter(hrows, dest, zero_tiles, zero_flags, n_rows):
    t, wrow = hrows.shape
    tm = MOE_TILE
    return pl.pallas_call(
        _row_scatter_kernel,
        out_shape=jax.ShapeDtypeStruct((n_rows, wrow), hrows.dtype),
        grid_spec=pltpu.PrefetchScalarGridSpec(
            num_scalar_prefetch=2,
            grid=(t // tm,),
            in_specs=[pl.BlockSpec((1, 1, tm), lambda i, *_: (i, 0, 0), memory_space=pltpu.SMEM),
                      pl.BlockSpec(memory_space=pl.ANY)],
            out_specs=pl.BlockSpec(memory_space=pl.ANY),
            scratch_shapes=[pltpu.VMEM((tm, wrow), hrows.dtype),
                            pltpu.VMEM((tm, wrow), hrows.dtype),
                            pltpu.VMEM((tm, wrow), hrows.dtype),
                            pltpu.SemaphoreType.DMA((3,)),
                            pltpu.SemaphoreType.DMA((3,)),
                            pltpu.SemaphoreType.DMA]),
        compiler_params=_cparams("arbitrary"),
        name="row_scatter",
    )(zero_tiles, zero_flags, dest.reshape(t // tm, 1, tm), hrows)


def _moe_kernel(elo_ref, ehi_ref, valid_ref, fresh_ref,
                x_ref,
                wg_lo, wu_lo, wd_lo, wg_hi, wu_hi, wd_hi,
                y_hbm,
                obuf0, obuf1, idv0, idv1, ids0, ids1, wbuf_in, wbuf_out, ssem, isem):
    i = pl.program_id(0)
    tm, half = obuf0.shape
    n_tok = y_hbm.shape[0] - tm
    obufs, idvs, idss = (obuf0, obuf1), (idv0, idv1), (ids0, ids1)
    valid = valid_ref[i] != 0
    prev_valid = valid_ref[jnp.maximum(i - 1, 0)] != 0
    spare = n_tok + lax.broadcasted_iota(jnp.int32, (SUBLANES, tm), 1)

    ids_copy = lambda s: pltpu.make_async_copy(idvs[s], idss[s], isem.at[s])
    row_scatter = lambda r, s: pltpu.make_async_copy(obufs[s].at[pl.ds(r, 1)],
                                                     y_hbm.at[pl.ds(idss[s][0, r], 1)], ssem.at[s])
    wait_scatter = lambda s: pltpu.make_async_copy(obufs[s], y_hbm.at[pl.ds(0, tm)], ssem.at[s]).wait()

    @pl.when(i == 0)
    def _():
        obuf1[...] = jnp.zeros_like(obuf1)
        idv1[...] = spare
        ids_copy(1).start()

    @pl.when(fresh_ref[i] != 0)
    def _():
        for k, w in enumerate((wg_lo, wu_lo, wg_hi, wu_hi)):
            wbuf_in[k] = w[0].astype(BF16)
        for k, w in enumerate((wd_lo, wd_hi)):
            wbuf_out[k] = w[0].astype(BF16)

    for s in range(2):
        @pl.when(valid & (i % 2 == s))
        def _(s=s):
            rows = x_ref[...]
            rec = lax.bitcast_convert_type(rows[:, half:], F32)
            token = rec.T[4:5, :]
            idvs[s][...] = jnp.where(token > 0.0, token.astype(jnp.int32) - 1, spare)
            ids_copy(s).start()
            ids_copy(1 - s).wait()
            for r in range(tm):
                row_scatter(r, 1 - s).start(priority=r % 2)
            xa, xb = _unpack_bf16_pairs(rows[:, :half])
            x = jnp.concatenate([xa.astype(BF16), xb.astype(BF16)], axis=1)

            def ffn(k):
                a = _dot(x, wbuf_in[2 * k])
                a = (a * jax.nn.sigmoid(a)) * _dot(x, wbuf_in[2 * k + 1])
                return _dot(a.astype(BF16), wbuf_out[k])

            out = rec[:, 1:2] * ffn(0) + rec[:, 2:3] * ffn(1)
            obufs[s][...] = _pack_bf16_pairs(out.astype(BF16))
            wait_scatter(1 - s)

        @pl.when(jnp.logical_not(valid) & prev_valid & (i % 2 == s))
        def _(s=s):
            ids_copy(1 - s).wait()

            @pl.loop(0, tm)
            def _(r):
                row_scatter(r, 1 - s).start()

            wait_scatter(1 - s)


def _moe(tiles, xs, n_tok, w_gate, w_up, w_down):
    tile_elo, tile_ehi, tile_valid, tile_fresh = tiles
    n_tiles = tile_elo.shape[0]
    wrow = xs.shape[1]
    d = w_gate.shape[1]
    ff = w_gate.shape[2]
    tm = MOE_TILE
    lo = lambda i, elo, *_: (elo[i], 0, 0)
    hi = lambda i, elo, ehi, *_: (ehi[i], 0, 0)
    return pl.pallas_call(
        _moe_kernel,
        out_shape=jax.ShapeDtypeStruct((n_tok + tm, d // 2), jnp.uint32),
        grid_spec=pltpu.PrefetchScalarGridSpec(
            num_scalar_prefetch=4,
            grid=(n_tiles,),
            in_specs=[pl.BlockSpec((tm, wrow), lambda i, *_: (i, 0)),
                      pl.BlockSpec((1, d, ff), lo), pl.BlockSpec((1, d, ff), lo), pl.BlockSpec((1, ff, d), lo),
                      pl.BlockSpec((1, d, ff), hi), pl.BlockSpec((1, d, ff), hi), pl.BlockSpec((1, ff, d), hi)],
            out_specs=pl.BlockSpec(memory_space=pl.ANY),
            scratch_shapes=[pltpu.VMEM((tm, d // 2), jnp.uint32),
                            pltpu.VMEM((tm, d // 2), jnp.uint32),
                            pltpu.VMEM((SUBLANES, tm), jnp.int32),
                            pltpu.VMEM((SUBLANES, tm), jnp.int32),
                            pltpu.SMEM((SUBLANES, tm), jnp.int32),
                            pltpu.SMEM((SUBLANES, tm), jnp.int32),
                            pltpu.VMEM((4, d, ff), BF16),
                            pltpu.VMEM((2, ff, d), BF16),
                            pltpu.SemaphoreType.DMA((2,)),
                            pltpu.SemaphoreType.DMA((2,))]),
        compiler_params=_cparams("arbitrary"),
        name="grouped_moe",
    )(tile_elo, tile_ehi, tile_valid, tile_fresh, xs, w_gate, w_up, w_down, w_gate, w_up, w_down)


def _final_kernel(x_ref, y_ref, g_ref, out_ref):
    out_ref[...] = _rms(x_ref[...] + jnp.concatenate(_unpack_bf16_pairs(y_ref[...]), axis=1), g_ref[...])


def _final(x2d, ymoe, g):
    t, d = x2d.shape
    tm = ROW_TILE
    return pl.pallas_call(
        _final_kernel,
        out_shape=jax.ShapeDtypeStruct((t, d), F32),
        grid=(t // tm,),
        in_specs=[pl.BlockSpec((tm, d), lambda i: (i, 0)),
                  pl.BlockSpec((tm, d // 2), lambda i: (i, 0)),
                  pl.BlockSpec((1, d), lambda i: (0, 0))],
        out_specs=pl.BlockSpec((tm, d), lambda i: (i, 0)),
        compiler_params=_cparams("arbitrary"),
        name="final_norm",
    )(x2d, ymoe, g)


_PAIRS = [(a, b) for a in range(EXPERTS_PER_GROUP) for b in range(a + 1, EXPERTS_PER_GROUP)]
_BUCKET_ELO = np.array([g * EXPERTS_PER_GROUP + _PAIRS[p][0] for g in range(MOE_GROUPS) for p in range(N_PAIRS)], np.int32)
_BUCKET_EHI = np.array([g * EXPERTS_PER_GROUP + _PAIRS[p][1] for g in range(MOE_GROUPS) for p in range(N_PAIRS)], np.int32)


def _moe_layer(hrows, route, cnt, layer, w_gate, w_up, w_down):
    t = hrows.shape[0]
    tm = MOE_TILE
    n_tiles = t // tm + N_BUCKETS
    bucket = route[0].astype(jnp.int32)
    rank = route[3].astype(jnp.int32)
    counts = cnt[:N_BUCKETS, 0].astype(jnp.int32)
    padded = ((counts + tm - 1) // tm) * tm
    ends = jnp.cumsum(padded)
    dest = (ends - padded)[bucket] + rank
    tile_start = jnp.arange(n_tiles, dtype=jnp.int32) * tm
    tile_valid = tile_start < ends[-1]
    last_start = jnp.maximum(ends[-1] - tm, 0)
    tile_bucket = jnp.sum((ends[None, :] <= jnp.minimum(tile_start, last_start)[:, None]).astype(jnp.int32), axis=1)
    prev_bucket = jnp.concatenate([jnp.full((1,), -1, jnp.int32), tile_bucket[:-1]])
    tile_fresh = tile_valid & (tile_bucket != prev_bucket)
    base = layer * N_EXPERTS
    tiles = (jnp.asarray(_BUCKET_ELO)[tile_bucket] + base, jnp.asarray(_BUCKET_EHI)[tile_bucket] + base,
             tile_valid.astype(jnp.int32), tile_fresh.astype(jnp.int32))
    idle = t // tm + jnp.arange(N_BUCKETS, dtype=jnp.int32)
    zero_tiles = jnp.concatenate([jnp.maximum(ends // tm - 1, 0), idle])
    zero_flags = jnp.concatenate([counts % tm != 0, idle * tm >= ends[-1]]).astype(jnp.int32)
    xs = _row_scatter(hrows, dest, zero_tiles, zero_flags, n_tiles * tm)
    return _moe(tiles, xs, t, w_gate, w_up, w_down)


def _router_weights(wg, bg, we, be):
    d = wg.shape[0]
    wr = jnp.zeros((32, d), F32).at[:MOE_GROUPS].set(wg.T).at[SUBLANES:SUBLANES + N_EXPERTS].set(we.T)
    br = jnp.zeros((32, 1), F32).at[:MOE_GROUPS, 0].set(bg).at[SUBLANES:SUBLANES + N_EXPERTS, 0].set(be)
    return wr.astype(BF16), br


def kernel(x, rel_bias, norm_mix, norm_ffn, norm_final, w_in_even, w_out_even, m_igate_b, m_fgate_b,
           m_head_norm, lam_q1, lam_k1, lam_q2, lam_k2, da_subln, w_in_odd, conv_w, w_out_odd,
           router_group_w, router_group_b, router_expert_w, router_expert_b,
           exp_w_gate, exp_w_up, exp_w_down):
    bsz, seq, d = x.shape
    t = bsz * seq
    x2d = x.reshape(t, d)
    mw = M_HEADS * M_HEAD_DIM

    w_in = w_in_even[0]
    gate0 = 4 * mw
    w_main = jnp.concatenate([w_in[:, :gate0], w_in[:, gate0 + 2 * M_HEADS:]], axis=1).astype(BF16)
    w_gates = w_in[:, gate0:gate0 + 2 * M_HEADS]
    wg_col = jnp.zeros((d, LANES), F32).at[:, :2 * M_HEADS].set(w_gates).astype(BF16)
    wg_row = w_gates.T.astype(BF16)
    gate_b = jnp.concatenate([m_igate_b[0], m_fgate_b[0]]).astype(F32)
    b_col = jnp.zeros((1, LANES), F32).at[0, :2 * M_HEADS].set(gate_b)
    b_row = gate_b.reshape(2 * M_HEADS, 1)
    triu = jnp.triu(jnp.ones((ROW_TILE, ROW_TILE), F32)).astype(BF16)

    p, gc, gr = _inproj_even(x2d, norm_mix[0].reshape(1, d), w_main, wg_col, wg_row)
    p3 = p.reshape(bsz, seq, -1)
    hm = _mlstm(p3, gc.reshape(bsz, seq, LANES), gr, b_col, b_row, m_head_norm[0].reshape(1, mw))

    lam_init = 0.8 - 0.6 * math.exp(-0.3 * 0)
    lam = (jnp.exp(jnp.sum(lam_q1[0].astype(F32) * lam_k1[0].astype(F32)))
           - jnp.exp(jnp.sum(lam_q2[0].astype(F32) * lam_k2[0].astype(F32))) + lam_init).reshape(1)
    ha = _attention(lam, p3, _bias_tiles(rel_bias.astype(F32), ATT_BLOCK),
                    da_subln[0].reshape(1, DA_V_DIM).astype(F32), 1.0 - lam_init)

    wr0, br0 = _router_weights(router_group_w[0], router_group_b[0], router_expert_w[0], router_expert_b[0])
    x1, h2, meta, cnt = _outproj_even(hm.reshape(t, mw), ha.reshape(t, -1), x2d, w_out_even[0].astype(BF16),
                                      norm_ffn[0].reshape(1, d), wr0, br0, triu)
    ff = exp_w_gate.shape[-1]
    experts = (exp_w_gate.reshape(-1, d, ff), exp_w_up.reshape(-1, d, ff), exp_w_down.reshape(-1, ff, d))
    y = _moe_layer(h2, meta, cnt, 0, *experts)

    wr1, br1 = _router_weights(router_group_w[1], router_group_b[1], router_expert_w[1], router_expert_b[1])
    cw = jnp.zeros((SUBLANES, d), F32).at[:conv_w.shape[1]].set(conv_w[0])
    x3, h2, meta, cnt = _conv_mixer(x1, y, norm_mix[1].reshape(1, d), w_in_odd[0].astype(BF16), cw,
                                    w_out_odd[0].astype(BF16), norm_ffn[1].reshape(1, d), wr1, br1, triu, seq)
    y = _moe_layer(h2, meta, cnt, 1, *experts)

    return _final(x3, y, norm_final.reshape(1, d)).reshape(bsz, seq, d)
```

```python
import functools
import math

import jax
import jax.numpy as jnp
import numpy as np
from jax import lax
from jax.experimental import pallas as pl
from jax.experimental.pallas import tpu as pltpu

F32 = jnp.float32
BF16 = jnp.bfloat16

NORM_EPS = 1e-6
NEG_BIG = -1e30

M_HEADS = 4
M_HEAD_DIM = 128
DA_HEADS = 4
DA_HEAD_DIM = 64
DA_V_DIM = 128
REL_BUCKETS = 32
REL_MAX_DIST = 128
MOE_GROUPS = 4
EXPERTS_PER_GROUP = 4
N_EXPERTS = 16
N_PAIRS = 6
N_BUCKETS = MOE_GROUPS * N_PAIRS

LANES = 128
SUBLANES = 8
VMEM_LIMIT = 56 * 1024 * 1024

ROW_TILE = 512
OUTPROJ_SUBTILES = 2
INPROJ_ROWS = 1024
CONV_SUBTILES = 2
FINAL_ROWS = 1024
MLSTM_CHUNK = 256
MLSTM_BATCH = 4
ATT_BLOCK = 256
MOE_TILE = 512

_NT = (((1,), (1,)), ((), ()))


def _cparams(*sem):
    return pltpu.CompilerParams(dimension_semantics=sem, vmem_limit_bytes=VMEM_LIMIT)


def _rms(x, g):
    ms = jnp.mean(x * x, axis=-1, keepdims=True)
    return x * lax.rsqrt(ms + NORM_EPS) * g


def _dot(a, b):
    return jnp.dot(a, b, preferred_element_type=F32)


def _split3(x):
    hi = x.astype(BF16)
    r1 = x - hi.astype(F32)
    mid = r1.astype(BF16)
    lo = (r1 - mid.astype(F32)).astype(BF16)
    return hi, mid, lo


def _log_sigmoid(x):
    return jnp.minimum(x, 0.0) - jnp.log(1.0 + jnp.exp(-jnp.abs(x)))


def _bias_tiles_kernel(rb_ref, out_ref):
    h = pl.program_id(0)
    blk = out_ref.shape[-1]
    r = lax.broadcasted_iota(jnp.int32, (blk, blk), 0)
    c = lax.broadcasted_iota(jnp.int32, (blk, blk), 1)
    far = rb_ref[REL_BUCKETS - 1, h]
    max_exact = REL_BUCKETS // 2
    for j in range(2):
        rel = c - r - j * blk
        n = jnp.maximum(-rel, 0)
        nf = jnp.maximum(n, 1).astype(F32)
        large = max_exact + (jnp.log(nf / max_exact) / math.log(REL_MAX_DIST / max_exact)
                             * (REL_BUCKETS - max_exact)).astype(jnp.int32)
        large = jnp.minimum(large, REL_BUCKETS - 1)
        bucket = jnp.where(n < max_exact, n, large)
        val = jnp.zeros((blk, blk), F32)
        for b in range(REL_BUCKETS):
            val = jnp.where(bucket == b, rb_ref[b, h], val)
        val = val - far
        if j == 0:
            val = jnp.where(rel <= 0, val, NEG_BIG)
        out_ref[0, j] = val


def _bias_tiles(rel_bias, blk):
    return pl.pallas_call(
        _bias_tiles_kernel,
        out_shape=jax.ShapeDtypeStruct((DA_HEADS, 2, blk, blk), F32),
        grid=(DA_HEADS,),
        in_specs=[pl.BlockSpec(memory_space=pltpu.SMEM)],
        out_specs=pl.BlockSpec((1, 2, blk, blk), lambda h: (h, 0, 0, 0)),
        compiler_params=_cparams("arbitrary"),
        name="bias_tiles",
    )(rel_bias)


def _inproj_even_kernel(x_ref, g_ref, w_ref, wgc_ref, wgr_ref, p_ref, gc_ref, gr_ref):
    h = _rms(x_ref[...], g_ref[...]).astype(BF16)
    n = w_ref.shape[1]
    for j in range(n // 512):
        sl = slice(j * 512, (j + 1) * 512)
        p_ref[:, sl] = _dot(h, w_ref[:, sl]).astype(BF16)
    gc_ref[...] = _dot(h, wgc_ref[...])
    gr_ref[...] = lax.dot_general(wgr_ref[...], h, _NT, preferred_element_type=F32)


def _inproj_even(x2d, g, w_main, wg_col, wg_row):
    t, d = x2d.shape
    n = w_main.shape[1]
    tm = INPROJ_ROWS
    return pl.pallas_call(
        _inproj_even_kernel,
        out_shape=(jax.ShapeDtypeStruct((t, n), BF16),
                   jax.ShapeDtypeStruct((t, LANES), F32),
                   jax.ShapeDtypeStruct((SUBLANES, t), F32)),
        grid=(t // tm,),
        in_specs=[pl.BlockSpec((tm, d), lambda i: (i, 0)),
                  pl.BlockSpec((1, d), lambda i: (0, 0)),
                  pl.BlockSpec((d, n), lambda i: (0, 0)),
                  pl.BlockSpec((d, LANES), lambda i: (0, 0)),
                  pl.BlockSpec((SUBLANES, d), lambda i: (0, 0))],
        out_specs=(pl.BlockSpec((tm, n), lambda i: (i, 0)),
                   pl.BlockSpec((tm, LANES), lambda i: (i, 0)),
                   pl.BlockSpec((SUBLANES, tm), lambda i: (0, i))),
        compiler_params=_cparams("arbitrary"),
        name="inproj_even",
    )(x2d, g, w_main, wg_col, wg_row)


def _mlstm_kernel(q_ref, k_ref, v_ref, og_ref, gc_ref, *rest):
    gr_refs = rest[:MLSTM_BATCH]
    bc_ref, br_ref, hn_ref, out_ref, c_ref, m_ref = rest[MLSTM_BATCH:]

    @pl.when(pl.program_id(1) == 0)
    def _():
        c_ref[...] = jnp.zeros_like(c_ref)
        m_ref[...] = jnp.zeros_like(m_ref)

    for bi, gr_ref in enumerate(gr_refs):
        _mlstm_chunk(bi, q_ref, k_ref, v_ref, og_ref, gc_ref, gr_ref, bc_ref, br_ref, hn_ref, out_ref, c_ref, m_ref)


def _mlstm_chunk(bi, q_ref, k_ref, v_ref, og_ref, gc_ref, gr_ref, bc_ref, br_ref, hn_ref, out_ref, c_ref, m_ref):
    L = q_ref.shape[1]
    dh = M_HEAD_DIM
    scale = dh ** -0.5

    gc = gc_ref[bi] + bc_ref[...]
    gr = gr_ref[...] + br_ref[...]
    lf_c = _log_sigmoid(gc)
    lf_r = _log_sigmoid(gr)

    src = lax.broadcasted_iota(jnp.int32, (L, L), 0)
    qry = lax.broadcasted_iota(jnp.int32, (L, L), 1)
    visible = src <= qry
    tril = jnp.where(src >= qry, 1.0, 0.0).astype(BF16)
    triu = jnp.where(visible, 1.0, 0.0).astype(BF16)
    b_c = sum(_dot(tril, part) for part in _split3(lf_c))
    b_r = sum(_dot(part, triu) for part in _split3(lf_r))
    u_c = gc - pltpu.roll(b_c, LANES - M_HEADS, axis=1)
    u_parts = jnp.concatenate(_split3(u_c), axis=1)
    sel_row = lax.broadcasted_iota(jnp.int32, (3 * LANES, L), 0) % LANES

    sub = lax.broadcasted_iota(jnp.int32, (dh, L), 0)
    ones_row = jnp.where(sub == 0, 1.0, 0.0).astype(BF16)

    for h in range(M_HEADS):
        hs = slice(h * dh, (h + 1) * dh)
        st = bi * M_HEADS + h
        q = q_ref[bi, :, hs]
        k = k_ref[bi, :, hs]
        kT = k.astype(F32).T
        vT = v_ref[bi, :, hs].astype(F32).T.astype(BF16)
        i_r = gr[h:h + 1, :]
        b_rh = b_r[M_HEADS + h:M_HEADS + h + 1, :]
        m_prev = m_ref[st:st + 1, 0:1]

        u_bc = _dot(u_parts, jnp.where(sel_row == h, 1.0, 0.0).astype(BF16))
        dT = jnp.where(visible, u_bc + b_rh, NEG_BIG)
        inter = b_rh + m_prev
        mt = jnp.maximum(inter, jnp.max(dT, axis=0, keepdims=True))
        wT = jnp.exp(dT - mt)
        sc = jnp.exp(inter - mt)
        sT = lax.dot_general(k, q, _NT, preferred_element_type=F32)
        qkT = (sT * (wT * scale)).astype(BF16)

        vaugT = jnp.concatenate([vT, ones_row], axis=0)
        caugT = c_ref[st]
        numT = sc * lax.dot_general(caugT.astype(BF16), q, _NT, preferred_element_type=F32) + _dot(vaugT, qkT)
        den = numT[dh:dh + 1, :]
        hT = numT[:dh, :] / jnp.maximum(jnp.abs(den), jnp.exp(-mt))

        mu = jnp.mean(hT, axis=0, keepdims=True)
        cen = hT - mu
        var = jnp.mean(cen * cen, axis=0, keepdims=True)
        hn = (cen * lax.rsqrt(var + NORM_EPS)).T * hn_ref[:, hs]
        gate = jax.nn.sigmoid(og_ref[bi, :, hs].astype(F32))
        out_ref[bi, :, hs] = (hn * gate).astype(out_ref.dtype)

        g_end = b_rh[:, L - 1:L]
        ws = g_end - b_rh + i_r
        m_new = jnp.maximum(g_end + m_prev, jnp.max(ws, axis=1, keepdims=True))
        a = jnp.exp(g_end + m_prev - m_new)
        wkT = (kT * (jnp.exp(ws - m_new) * scale)).astype(BF16)
        c_ref[st] = a * caugT + lax.dot_general(vaugT, wkT, _NT, preferred_element_type=F32)
        m_ref[st:st + 1, :] = jnp.broadcast_to(m_new, (1, LANES))


def _mlstm(p3, gc3, gr, b_col, b_row, head_norm):
    bsz, s, _ = p3.shape
    L = MLSTM_CHUNK
    w = M_HEADS * M_HEAD_DIM
    nc = s // L
    nb = MLSTM_BATCH
    blk = lambda j: pl.BlockSpec((nb, L, w), lambda b, c, j=j: (b, c, j))
    gate_rows = lambda bi: pl.BlockSpec((SUBLANES, L), lambda b, c, bi=bi: (0, (nb * b + bi) * nc + c))
    return pl.pallas_call(
        _mlstm_kernel,
        out_shape=jax.ShapeDtypeStruct((bsz, s, w), BF16),
        grid=(bsz // nb, nc),
        in_specs=[blk(0), blk(1), blk(2), blk(3),
                  pl.BlockSpec((nb, L, LANES), lambda b, c: (b, c, 0)),
                  *[gate_rows(bi) for bi in range(nb)],
                  pl.BlockSpec((1, LANES), lambda b, c: (0, 0)),
                  pl.BlockSpec((SUBLANES, 1), lambda b, c: (0, 0)),
                  pl.BlockSpec((1, w), lambda b, c: (0, 0))],
        out_specs=pl.BlockSpec((nb, L, w), lambda b, c: (b, c, 0)),
        scratch_shapes=[pltpu.VMEM((nb * M_HEADS, 2 * M_HEAD_DIM, M_HEAD_DIM), F32),
                        pltpu.VMEM((nb * M_HEADS, LANES), F32)],
        compiler_params=_cparams("arbitrary", "arbitrary"),
        name="mlstm",
    )(p3, p3, p3, p3, gc3, *([gr] * nb), b_col, b_row, head_norm)


def _attn_kernel(lam_ref, q_ref, k_ref, v_ref, bias_ref, sub_ref, out_ref, vaug_ref, *, out_scale, blk):
    s_len = q_ref.shape[1]
    d = DA_HEAD_DIM
    lane_s = lax.broadcasted_iota(jnp.int32, (s_len, DA_V_DIM), 1)
    vaug_ref[:, :DA_V_DIM] = v_ref[0]
    vaug_ref[:, DA_V_DIM:] = jnp.where(lane_s == 0, 1.0, 0.0).astype(BF16)
    lam = lam_ref[0]
    lane = lax.broadcasted_iota(jnp.int32, (blk, LANES), 1)
    for qi in range(s_len // blk):
        q0, q1 = qi * blk, (qi + 1) * blk
        qs = q_ref[0, q0:q1, :] * (d ** -0.5)
        zero = jnp.zeros_like(qs)
        heads = []
        for qm in (jnp.where(lane < d, qs, zero), jnp.where(lane >= d, qs, zero)):
            score = lambda a, b: lax.dot_general(qm, k_ref[0, a:b, :], _NT, preferred_element_type=F32)
            parts = [score(q0, q1) + bias_ref[0, 0]]
            if qi >= 1:
                parts.append(score(q0 - blk, q0) + bias_ref[0, 1])
            if qi >= 2:
                parts.append(score(0, q0 - blk))
            m = functools.reduce(jnp.maximum, [jnp.max(p, axis=-1, keepdims=True) for p in parts])
            pexp = [jnp.exp(p - m).astype(BF16) for p in parts]
            o = _dot(pexp[0], vaug_ref[q0:q1, :])
            if qi >= 1:
                o = o + _dot(pexp[1], vaug_ref[q0 - blk:q0, :])
            if qi >= 2:
                o = o + _dot(pexp[2], vaug_ref[0:q0 - blk, :])
            heads.append(o[:, :DA_V_DIM] / o[:, DA_V_DIM:DA_V_DIM + 1])
        o = heads[0] - lam * heads[1]
        ms = jnp.mean(o * o, axis=-1, keepdims=True)
        o = o * lax.rsqrt(ms + NORM_EPS) * sub_ref[...] * out_scale
        out_ref[0, q0:q1, :] = o.astype(out_ref.dtype)


def _attention(lam, p3, bias_tiles, subln, out_scale):
    bsz, s, _ = p3.shape
    blk = ATT_BLOCK
    w = DA_HEADS * DA_V_DIM
    q_blk = 4 * M_HEADS
    k_blk = q_blk + DA_HEADS
    v_blk = k_blk + DA_HEADS
    col = lambda base: pl.BlockSpec((1, s, LANES), lambda b, h: (b, 0, base + h))
    return pl.pallas_call(
        functools.partial(_attn_kernel, out_scale=out_scale, blk=blk),
        out_shape=jax.ShapeDtypeStruct((bsz, s, w), BF16),
        grid=(bsz, DA_HEADS),
        in_specs=[pl.BlockSpec(memory_space=pltpu.SMEM),
                  col(q_blk), col(k_blk), col(v_blk),
                  pl.BlockSpec((1, 2, blk, blk), lambda b, h: (h, 0, 0, 0)),
                  pl.BlockSpec((1, DA_V_DIM), lambda b, h: (0, 0))],
        out_specs=pl.BlockSpec((1, s, DA_V_DIM), lambda b, h: (b, 0, h)),
        scratch_shapes=[pltpu.VMEM((s, 2 * DA_V_DIM), BF16)],
        compiler_params=_cparams("arbitrary", "arbitrary"),
        name="diff_attention",
    )(lam, p3, p3, p3, bias_tiles, subln)


def _pack_bf16_pairs(v):
    half = v.shape[1] // 2
    hi = lax.bitcast_convert_type(v[:, :half].astype(F32), jnp.uint32)
    lo = lax.bitcast_convert_type(v[:, half:].astype(F32), jnp.uint32)
    return hi | (lo >> 16)


def _unpack_bf16_pairs(w):
    hi = lax.bitcast_convert_type(w & jnp.uint32(0xFFFF0000), F32)
    lo = lax.bitcast_convert_type(w << 16, F32)
    return hi, lo


def _init_counts(cnt_ref):
    @pl.when(pl.program_id(0) == 0)
    def _():
        cnt_ref[...] = jnp.zeros_like(cnt_ref)


def _router_epilogue(xn, tile, g_ref, wr_ref, br_ref, triu_ref, hrow_ref, route_ref, cnt_ref):
    tm, d = xn.shape
    h2 = _rms(xn, g_ref[...]).astype(BF16)
    hrow_ref[:, :d // 2] = _pack_bf16_pairs(h2)

    lg = lax.dot_general(wr_ref[...], h2, _NT, preferred_element_type=F32) + br_ref[...]
    gl = [lg[g:g + 1, :] for g in range(MOE_GROUPS)]
    gmax = functools.reduce(jnp.maximum, gl)
    gsel = jnp.where(gl[0] == gmax, 0.0, jnp.where(gl[1] == gmax, 1.0, jnp.where(gl[2] == gmax, 2.0, 3.0)))
    pg = 1.0 / functools.reduce(lambda a, b: a + b, [jnp.exp(x - gmax) for x in gl])

    ev = []
    for j in range(EXPERTS_PER_GROUP):
        acc = jnp.zeros_like(gmax)
        for g in range(MOE_GROUPS):
            r0 = SUBLANES + g * EXPERTS_PER_GROUP + j
            acc = jnp.where(gsel == float(g), lg[r0:r0 + 1, :], acc)
        ev.append(acc)
    v1 = functools.reduce(jnp.maximum, ev)
    i1 = jnp.where(ev[0] == v1, 0.0, jnp.where(ev[1] == v1, 1.0, jnp.where(ev[2] == v1, 2.0, 3.0)))
    ev2 = [jnp.where(i1 == float(j), -jnp.inf, ev[j]) for j in range(EXPERTS_PER_GROUP)]
    v2 = functools.reduce(jnp.maximum, ev2)
    hit = [(ev2[j] == v2) & (i1 != float(j)) for j in range(EXPERTS_PER_GROUP)]
    i2 = jnp.where(hit[0], 0.0, jnp.where(hit[1], 1.0, jnp.where(hit[2], 2.0, 3.0)))
    e21 = jnp.exp(v2 - v1)
    w1 = 1.0 / (1.0 + e21)
    w2 = e21 * w1
    first_low = i1 < i2
    lo = jnp.minimum(i1, i2)
    hi = jnp.maximum(i1, i2)
    c_lo = pg * jnp.where(first_low, w1, w2)
    c_hi = pg * jnp.where(first_low, w2, w1)
    pair = lo * (7.0 - lo) * 0.5 + hi - lo - 1.0
    bucket = gsel * float(N_PAIRS) + pair

    rows = lax.broadcasted_iota(jnp.int32, (32, tm), 0).astype(F32)
    onehot = rows == bucket
    cum = _dot(jnp.where(onehot, 1.0, 0.0).astype(BF16), triu_ref[...])
    carry = cnt_ref[:, 0:1]
    rank = jnp.sum(jnp.where(onehot, cum + carry, 0.0), axis=0, keepdims=True) - 1.0
    cnt_ref[...] = jnp.broadcast_to(carry + cum[:, tm - 1:tm], cnt_ref.shape)

    token = (tile * tm + 1 + lax.broadcasted_iota(jnp.int32, (1, tm), 1)).astype(F32)
    rid = lax.broadcasted_iota(jnp.int32, (LANES, tm), 0)
    rec = jnp.where(rid == 0, bucket, jnp.where(rid == 1, c_lo, jnp.where(rid == 2, c_hi,
                    jnp.where(rid == 3, rank, jnp.where(rid == 4, token, 0.0)))))
    hrow_ref[:, d // 2:] = lax.bitcast_convert_type(rec.T, jnp.uint32)
    route_ref[...] = rec[:SUBLANES, :]


_EPI_OUT_SHAPES = lambda t, d: (jax.ShapeDtypeStruct((t, d), F32),
                                jax.ShapeDtypeStruct((t, d // 2 + LANES), jnp.uint32),
                                jax.ShapeDtypeStruct((SUBLANES, t), F32),
                                jax.ShapeDtypeStruct((32, LANES), F32))


def _epi_out_specs(tm, d):
    return (pl.BlockSpec((tm, d), lambda i: (i, 0)),
            pl.BlockSpec((tm, d // 2 + LANES), lambda i: (i, 0)),
            pl.BlockSpec((SUBLANES, tm), lambda i: (0, i)),
            pl.BlockSpec((32, LANES), lambda i: (0, 0)))


def _epi_in_specs(tm, d):
    return [pl.BlockSpec((1, d), lambda i: (0, 0)),
            pl.BlockSpec((32, d), lambda i: (0, 0)),
            pl.BlockSpec((32, 1), lambda i: (0, 0)),
            pl.BlockSpec((tm, tm), lambda i: (0, 0))]


def _outproj_even_kernel(hm_ref, ha_ref, x_ref, wo_ref, g_ref, wr_ref, br_ref, triu_ref,
                         xn_ref, hrow_ref, route_ref, cnt_ref):
    _init_counts(cnt_ref)
    half = hm_ref.shape[1]
    tm = triu_ref.shape[0]
    sub = x_ref.shape[0] // tm
    for j in range(sub):
        rs = slice(j * tm, (j + 1) * tm)
        y = _dot(hm_ref[rs, :], wo_ref[:half, :]) + _dot(ha_ref[rs, :], wo_ref[half:, :])
        xn = x_ref[rs, :] + y
        xn_ref[rs, :] = xn
        _router_epilogue(xn, pl.program_id(0) * sub + j, g_ref, wr_ref, br_ref, triu_ref,
                         hrow_ref.at[rs, :], route_ref.at[:, rs], cnt_ref)


def _outproj_even(hm, ha, x2d, w_out, g, wr, br, triu):
    t, d = x2d.shape
    tm = OUTPROJ_SUBTILES * ROW_TILE
    half = hm.shape[1]
    return pl.pallas_call(
        _outproj_even_kernel,
        out_shape=_EPI_OUT_SHAPES(t, d),
        grid=(t // tm,),
        in_specs=[pl.BlockSpec((tm, half), lambda i: (i, 0)),
                  pl.BlockSpec((tm, half), lambda i: (i, 0)),
                  pl.BlockSpec((tm, d), lambda i: (i, 0)),
                  pl.BlockSpec((2 * half, d), lambda i: (0, 0))] + _epi_in_specs(ROW_TILE, d),
        out_specs=_epi_out_specs(tm, d),
        compiler_params=_cparams("arbitrary"),
        name="outproj_even",
    )(hm, ha, x2d, w_out, g, wr, br, triu)


def _conv_mixer_kernel(x_ref, y_ref, gm_ref, wi_ref, cw_ref, wo_ref, g_ref, wr_ref, br_ref, triu_ref,
                       xn_ref, h2_ref, meta_ref, cnt_ref, tail_ref, *, tiles_per_seq):
    _init_counts(cnt_ref)
    tm = triu_ref.shape[0]
    d = x_ref.shape[1]
    sub = x_ref.shape[0] // tm
    assert tiles_per_seq % sub == 0

    @pl.when((pl.program_id(0) * sub) % tiles_per_seq == 0)
    def _():
        tail_ref[...] = jnp.zeros_like(tail_ref)

    row = lax.broadcasted_iota(jnp.int32, (tm, 512), 0)
    for s in range(sub):
        rs = slice(s * tm, (s + 1) * tm)
        x = x_ref[rs, :] + jnp.concatenate(_unpack_bf16_pairs(y_ref[rs, :]), axis=1)
        h = _rms(x, gm_ref[...]).astype(BF16)
        acc = jnp.zeros((tm, d), F32)
        for j in range(d // 512):
            sl = slice(j * 512, (j + 1) * 512)
            bg = _dot(h, wi_ref[:, sl])
            cg = _dot(h, wi_ref[:, d + j * 512:d + (j + 1) * 512])
            u = _dot(h, wi_ref[:, 2 * d + j * 512:2 * d + (j + 1) * 512])
            z = cg * u
            prev1 = tail_ref[SUBLANES - 1:SUBLANES, sl]
            prev2 = tail_ref[SUBLANES - 2:SUBLANES - 1, sl]
            z1 = jnp.where(row == 0, prev1, pltpu.roll(z, 1, axis=0))
            z2 = jnp.where(row == 0, prev2, jnp.where(row == 1, prev1, pltpu.roll(z, 2, axis=0)))
            zc = cw_ref[0:1, sl] * z2 + cw_ref[1:2, sl] * z1 + cw_ref[2:3, sl] * z
            tail_ref[:, sl] = z[tm - SUBLANES:, :]
            acc = acc + _dot((bg * zc).astype(BF16), wo_ref[sl, :])
        xn = x + acc
        xn_ref[rs, :] = xn
        _router_epilogue(xn, pl.program_id(0) * sub + s, g_ref, wr_ref, br_ref, triu_ref,
                         h2_ref.at[rs, :], meta_ref.at[:, rs], cnt_ref)


def _conv_mixer(x2d, ymoe, gm, w_in, conv_w, w_out, g, wr, br, triu, seq):
    t, d = x2d.shape
    tm = CONV_SUBTILES * ROW_TILE
    const = lambda shape: pl.BlockSpec(shape, lambda i: (0, 0), pipeline_mode=pl.Buffered(1))
    return pl.pallas_call(
        functools.partial(_conv_mixer_kernel, tiles_per_seq=seq // ROW_TILE),
        out_shape=_EPI_OUT_SHAPES(t, d),
        grid=(t // tm,),
        in_specs=[pl.BlockSpec((tm, d), lambda i: (i, 0)),
                  pl.BlockSpec((tm, d // 2), lambda i: (i, 0)),
                  pl.BlockSpec((1, d), lambda i: (0, 0)),
                  const((d, 3 * d)),
                  pl.BlockSpec((SUBLANES, d), lambda i: (0, 0)),
                  const((d, d))] + _epi_in_specs(ROW_TILE, d),
        out_specs=_epi_out_specs(tm, d),
        scratch_shapes=[pltpu.VMEM((SUBLANES, d), F32)],
        compiler_params=_cparams("arbitrary"),
        name="conv_mixer",
    )(x2d, ymoe, gm, w_in, conv_w, w_out, g, wr, br, triu)


def _row_scatter_kernel(ztile_ref, zflag_ref, dst_ref, h_hbm, out_hbm, buf0, buf1, buf2, lsem, ssem, zsem):
    i = pl.program_id(0)
    n = pl.num_programs(0)
    bufs = (buf0, buf1, buf2)
    tm = buf0.shape[0]

    def load(j, s):
        return pltpu.make_async_copy(h_hbm.at[pl.ds(pl.multiple_of(j * tm, tm), tm)], bufs[s], lsem.at[s])

    wait_rows = lambda s: pltpu.make_async_copy(bufs[s], out_hbm.at[pl.ds(0, tm)], ssem.at[s]).wait()

    @pl.when(i == 0)
    def _():
        buf2[...] = jnp.zeros_like(buf2)
        fill = lambda k: pltpu.make_async_copy(
            buf2, out_hbm.at[pl.ds(pl.multiple_of(ztile_ref[k] * tm, tm), tm)], zsem)
        for k in range(ztile_ref.shape[0]):
            @pl.when(zflag_ref[k] != 0)
            def _(k=k):
                fill(k).start()
        for k in range(ztile_ref.shape[0]):
            @pl.when(zflag_ref[k] != 0)
            def _(k=k):
                fill(k).wait()
        load(0, 0).start()

    for s in range(3):
        @pl.when(i % 3 == s)
        def _(s=s):
            @pl.when(i + 1 < n)
            def _():
                load(i + 1, (s + 1) % 3).start()

            load(i, s).wait()
            for r in range(tm):
                pltpu.make_async_copy(bufs[s].at[pl.ds(r, 1)], out_hbm.at[pl.ds(dst_ref[0, 0, r], 1)],
                                      ssem.at[s]).start(priority=r % 2)

            @pl.when(i >= 1)
            def _():
                wait_rows((s + 2) % 3)

            @pl.when(i == n - 1)
            def _():
                wait_rows(s)


def _row_scatter(hrows, dest, zero_tiles, zero_flags, n_rows):
    t, wrow = hrows.shape
    tm = MOE_TILE
    return pl.pallas_call(
        _row_scatter_kernel,
        out_shape=jax.ShapeDtypeStruct((n_rows, wrow), hrows.dtype),
        grid_spec=pltpu.PrefetchScalarGridSpec(
            num_scalar_prefetch=2,
            grid=(t // tm,),
            in_specs=[pl.BlockSpec((1, 1, tm), lambda i, *_: (i, 0, 0), memory_space=pltpu.SMEM),
                      pl.BlockSpec(memory_space=pl.ANY)],
            out_specs=pl.BlockSpec(memory_space=pl.ANY),
            scratch_shapes=[pltpu.VMEM((tm, wrow), hrows.dtype),
                            pltpu.VMEM((tm, wrow), hrows.dtype),
                            pltpu.VMEM((tm, wrow), hrows.dtype),
                            pltpu.SemaphoreType.DMA((3,)),
                            pltpu.SemaphoreType.DMA((3,)),
                            pltpu.SemaphoreType.DMA]),
        compiler_params=_cparams("arbitrary"),
        name="row_scatter",
    )(zero_tiles, zero_flags, dest.reshape(t // tm, 1, tm), hrows)


def _moe_kernel(elo_ref, ehi_ref, valid_ref, fresh_ref,
                x_ref,
                wg_lo, wu_lo, wd_lo, wg_hi, wu_hi, wd_hi,
                y_hbm,
                obuf0, obuf1, idv0, idv1, ids0, ids1, wbuf_in, wbuf_out, ssem, isem):
    i = pl.program_id(0)
    tm, half = obuf0.shape
    n_tok = y_hbm.shape[0] - 2 * tm
    obufs, idvs, idss = (obuf0, obuf1), (idv0, idv1), (ids0, ids1)
    valid = valid_ref[i] != 0
    prev_valid = valid_ref[jnp.maximum(i - 1, 0)] != 0
    spare = n_tok + lax.broadcasted_iota(jnp.int32, (SUBLANES, tm), 1)

    ids_copy = lambda s: pltpu.make_async_copy(idvs[s], idss[s], isem.at[s])
    row_scatter = lambda r, s: pltpu.make_async_copy(obufs[s].at[pl.ds(r, 1)],
                                                     y_hbm.at[pl.ds(idss[s][0, r], 1)], ssem.at[s])
    wait_scatter = lambda s: pltpu.make_async_copy(obufs[s], y_hbm.at[pl.ds(0, tm)], ssem.at[s]).wait()

    @pl.when(i == 0)
    def _():
        obuf0[...] = jnp.zeros_like(obuf0)
        obuf1[...] = jnp.zeros_like(obuf1)
        idv1[...] = spare
        ids_copy(1).start()
        pltpu.make_async_copy(obuf0, y_hbm.at[pl.ds(n_tok + tm, tm)], ssem.at[0]).start()

    @pl.when(fresh_ref[i] != 0)
    def _():
        for k, w in enumerate((wg_lo, wu_lo, wg_hi, wu_hi)):
            wbuf_in[k] = w[0].astype(BF16)
        for k, w in enumerate((wd_lo, wd_hi)):
            wbuf_out[k] = w[0].astype(BF16)

    for s in range(2):
        @pl.when(valid & (i % 2 == s))
        def _(s=s):
            rows = x_ref[...]
            rec = lax.bitcast_convert_type(rows[:, half:], F32)
            token = rec.T[4:5, :]
            idvs[s][...] = jnp.where(token > 0.0, token.astype(jnp.int32) - 1, spare)
            ids_copy(s).start()
            ids_copy(1 - s).wait()
            for r in range(tm):
                row_scatter(r, 1 - s).start(priority=r % 2)
            xa, xb = _unpack_bf16_pairs(rows[:, :half])
            x = jnp.concatenate([xa.astype(BF16), xb.astype(BF16)], axis=1)

            def ffn(k):
                a = _dot(x, wbuf_in[2 * k])
                a = (a * jax.nn.sigmoid(a)) * _dot(x, wbuf_in[2 * k + 1])
                return _dot(a.astype(BF16), wbuf_out[k])

            out = rec[:, 1:2] * ffn(0) + rec[:, 2:3] * ffn(1)
            wait_scatter(s)
            obufs[s][...] = _pack_bf16_pairs(out.astype(BF16))

        @pl.when(jnp.logical_not(valid) & prev_valid & (i % 2 == s))
        def _(s=s):
            wait_scatter(s)
            ids_copy(1 - s).wait()

            @pl.loop(0, tm)
            def _(r):
                row_scatter(r, 1 - s).start()

            wait_scatter(1 - s)


def _moe(tiles, xs, n_tok, w_gate, w_up, w_down):
    tile_elo, tile_ehi, tile_valid, tile_fresh = tiles
    n_tiles = tile_elo.shape[0]
    wrow = xs.shape[1]
    d = w_gate.shape[1]
    ff = w_gate.shape[2]
    tm = MOE_TILE
    lo = lambda i, elo, *_: (elo[i], 0, 0)
    hi = lambda i, elo, ehi, *_: (ehi[i], 0, 0)
    return pl.pallas_call(
        _moe_kernel,
        out_shape=jax.ShapeDtypeStruct((n_tok + 2 * tm, d // 2), jnp.uint32),
        grid_spec=pltpu.PrefetchScalarGridSpec(
            num_scalar_prefetch=4,
            grid=(n_tiles,),
            in_specs=[pl.BlockSpec((tm, wrow), lambda i, *_: (i, 0)),
                      pl.BlockSpec((1, d, ff), lo), pl.BlockSpec((1, d, ff), lo), pl.BlockSpec((1, ff, d), lo),
                      pl.BlockSpec((1, d, ff), hi), pl.BlockSpec((1, d, ff), hi), pl.BlockSpec((1, ff, d), hi)],
            out_specs=pl.BlockSpec(memory_space=pl.ANY),
            scratch_shapes=[pltpu.VMEM((tm, d // 2), jnp.uint32),
                            pltpu.VMEM((tm, d // 2), jnp.uint32),
                            pltpu.VMEM((SUBLANES, tm), jnp.int32),
                            pltpu.VMEM((SUBLANES, tm), jnp.int32),
                            pltpu.SMEM((SUBLANES, tm), jnp.int32),
                            pltpu.SMEM((SUBLANES, tm), jnp.int32),
                            pltpu.VMEM((4, d, ff), BF16),
                            pltpu.VMEM((2, ff, d), BF16),
                            pltpu.SemaphoreType.DMA((2,)),
                            pltpu.SemaphoreType.DMA((2,))]),
        compiler_params=_cparams("arbitrary"),
        name="grouped_moe",
    )(tile_elo, tile_ehi, tile_valid, tile_fresh, xs, w_gate, w_up, w_down, w_gate, w_up, w_down)


def _final_kernel(x_ref, y_ref, g_ref, out_ref):
    out_ref[...] = _rms(x_ref[...] + jnp.concatenate(_unpack_bf16_pairs(y_ref[...]), axis=1), g_ref[...])


def _final(x2d, ymoe, g):
    t, d = x2d.shape
    tm = FINAL_ROWS
    return pl.pallas_call(
        _final_kernel,
        out_shape=jax.ShapeDtypeStruct((t, d), F32),
        grid=(t // tm,),
        in_specs=[pl.BlockSpec((tm, d), lambda i: (i, 0)),
                  pl.BlockSpec((tm, d // 2), lambda i: (i, 0)),
                  pl.BlockSpec((1, d), lambda i: (0, 0))],
        out_specs=pl.BlockSpec((tm, d), lambda i: (i, 0)),
        compiler_params=_cparams("arbitrary"),
        name="final_norm",
    )(x2d, ymoe, g)


_PAIRS = [(a, b) for a in range(EXPERTS_PER_GROUP) for b in range(a + 1, EXPERTS_PER_GROUP)]
_BUCKET_ELO = np.array([g * EXPERTS_PER_GROUP + _PAIRS[p][0] for g in range(MOE_GROUPS) for p in range(N_PAIRS)], np.int32)
_BUCKET_EHI = np.array([g * EXPERTS_PER_GROUP + _PAIRS[p][1] for g in range(MOE_GROUPS) for p in range(N_PAIRS)], np.int32)


def _moe_layer(hrows, route, cnt, layer, w_gate, w_up, w_down):
    t = hrows.shape[0]
    tm = MOE_TILE
    n_tiles = t // tm + N_BUCKETS
    bucket = route[0].astype(jnp.int32)
    rank = route[3].astype(jnp.int32)
    counts = cnt[:N_BUCKETS, 0].astype(jnp.int32)
    padded = ((counts + tm - 1) // tm) * tm
    ends = jnp.cumsum(padded)
    dest = (ends - padded)[bucket] + rank
    tile_start = jnp.arange(n_tiles, dtype=jnp.int32) * tm
    tile_valid = tile_start < ends[-1]
    last_start = jnp.maximum(ends[-1] - tm, 0)
    tile_bucket = jnp.sum((ends[None, :] <= jnp.minimum(tile_start, last_start)[:, None]).astype(jnp.int32), axis=1)
    prev_bucket = jnp.concatenate([jnp.full((1,), -1, jnp.int32), tile_bucket[:-1]])
    tile_fresh = tile_valid & (tile_bucket != prev_bucket)
    base = layer * N_EXPERTS
    tiles = (jnp.asarray(_BUCKET_ELO)[tile_bucket] + base, jnp.asarray(_BUCKET_EHI)[tile_bucket] + base,
             tile_valid.astype(jnp.int32), tile_fresh.astype(jnp.int32))
    idle = t // tm + jnp.arange(N_BUCKETS, dtype=jnp.int32)
    zero_tiles = jnp.concatenate([jnp.maximum(ends // tm - 1, 0), idle])
    zero_flags = jnp.concatenate([counts % tm != 0, idle * tm >= ends[-1]]).astype(jnp.int32)
    xs = _row_scatter(hrows, dest, zero_tiles, zero_flags, n_tiles * tm)
    return _moe(tiles, xs, t, w_gate, w_up, w_down)


def _router_weights(wg, bg, we, be):
    d = wg.shape[0]
    wr = jnp.zeros((32, d), F32).at[:MOE_GROUPS].set(wg.T).at[SUBLANES:SUBLANES + N_EXPERTS].set(we.T)
    br = jnp.zeros((32, 1), F32).at[:MOE_GROUPS, 0].set(bg).at[SUBLANES:SUBLANES + N_EXPERTS, 0].set(be)
    return wr.astype(BF16), br


def kernel(x, rel_bias, norm_mix, norm_ffn, norm_final, w_in_even, w_out_even, m_igate_b, m_fgate_b,
           m_head_norm, lam_q1, lam_k1, lam_q2, lam_k2, da_subln, w_in_odd, conv_w, w_out_odd,
           router_group_w, router_group_b, router_expert_w, router_expert_b,
           exp_w_gate, exp_w_up, exp_w_down):
    bsz, seq, d = x.shape
    t = bsz * seq
    x2d = x.reshape(t, d)
    mw = M_HEADS * M_HEAD_DIM

    w_in = w_in_even[0]
    gate0 = 4 * mw
    w_main = jnp.concatenate([w_in[:, :gate0], w_in[:, gate0 + 2 * M_HEADS:]], axis=1).astype(BF16)
    w_gates = w_in[:, gate0:gate0 + 2 * M_HEADS]
    wg_col = jnp.zeros((d, LANES), F32).at[:, :2 * M_HEADS].set(w_gates).astype(BF16)
    wg_row = w_gates.T.astype(BF16)
    gate_b = jnp.concatenate([m_igate_b[0], m_fgate_b[0]]).astype(F32)
    b_col = jnp.zeros((1, LANES), F32).at[0, :2 * M_HEADS].set(gate_b)
    b_row = gate_b.reshape(2 * M_HEADS, 1)
    triu = jnp.triu(jnp.ones((ROW_TILE, ROW_TILE), F32)).astype(BF16)

    p, gc, gr = _inproj_even(x2d, norm_mix[0].reshape(1, d), w_main, wg_col, wg_row)
    p3 = p.reshape(bsz, seq, -1)
    hm = _mlstm(p3, gc.reshape(bsz, seq, LANES), gr, b_col, b_row, m_head_norm[0].reshape(1, mw))

    lam_init = 0.8 - 0.6 * math.exp(-0.3 * 0)
    lam = (jnp.exp(jnp.sum(lam_q1[0].astype(F32) * lam_k1[0].astype(F32)))
           - jnp.exp(jnp.sum(lam_q2[0].astype(F32) * lam_k2[0].astype(F32))) + lam_init).reshape(1)
    ha = _attention(lam, p3, _bias_tiles(rel_bias.astype(F32), ATT_BLOCK),
                    da_subln[0].reshape(1, DA_V_DIM).astype(F32), 1.0 - lam_init)

    wr0, br0 = _router_weights(router_group_w[0], router_group_b[0], router_expert_w[0], router_expert_b[0])
    x1, h2, meta, cnt = _outproj_even(hm.reshape(t, mw), ha.reshape(t, -1), x2d, w_out_even[0].astype(BF16),
                                      norm_ffn[0].reshape(1, d), wr0, br0, triu)
    ff = exp_w_gate.shape[-1]
    experts = (exp_w_gate.reshape(-1, d, ff), exp_w_up.reshape(-1, d, ff), exp_w_down.reshape(-1, ff, d))
    y = _moe_layer(h2, meta, cnt, 0, *experts)

    wr1, br1 = _router_weights(router_group_w[1], router_group_b[1], router_expert_w[1], router_expert_b[1])
    cw = jnp.zeros((SUBLANES, d), F32).at[:conv_w.shape[1]].set(conv_w[0])
    x3, h2, meta, cnt = _conv_mixer(x1, y, norm_mix[1].reshape(1, d), w_in_odd[0].astype(BF16), cw,
                                    w_out_odd[0].astype(BF16), norm_ffn[1].reshape(1, d), wr1, br1, triu, seq)
    y = _moe_layer(h2, meta, cnt, 1, *experts)

    return _final(x3, y, norm_final.reshape(1, d)).reshape(bsz, seq, d)
```

```python
import functools
import math

import jax
import jax.numpy as jnp
import numpy as np
from jax import lax
from jax.experimental import pallas as pl
from jax.experimental.pallas import tpu as pltpu

F32 = jnp.float32
BF16 = jnp.bfloat16

NORM_EPS = 1e-6
NEG_BIG = -1e30

M_HEADS = 4
M_HEAD_DIM = 128
DA_HEADS = 4
DA_HEAD_DIM = 64
DA_V_DIM = 128
REL_BUCKETS = 32
REL_MAX_DIST = 128
MOE_GROUPS = 4
EXPERTS_PER_GROUP = 4
N_EXPERTS = 16
N_PAIRS = 6
N_BUCKETS = MOE_GROUPS * N_PAIRS

LANES = 128
SUBLANES = 8
VMEM_LIMIT = 56 * 1024 * 1024

ROW_TILE = 512
OUTPROJ_SUBTILES = 2
INPROJ_ROWS = 1024
CONV_SUBTILES = 2
FINAL_ROWS = 1024
MLSTM_CHUNK = 256
MLSTM_BATCH = 4
ATT_BLOCK = 256
ATT_HEADS_PER_STEP = 2
MOE_TILE = 512

_NT = (((1,), (1,)), ((), ()))


def _cparams(*sem):
    return pltpu.CompilerParams(dimension_semantics=sem, vmem_limit_bytes=VMEM_LIMIT)


def _rms(x, g):
    ms = jnp.mean(x * x, axis=-1, keepdims=True)
    return x * lax.rsqrt(ms + NORM_EPS) * g


def _dot(a, b):
    return jnp.dot(a, b, preferred_element_type=F32)


def _split3(x):
    hi = x.astype(BF16)
    r1 = x - hi.astype(F32)
    mid = r1.astype(BF16)
    lo = (r1 - mid.astype(F32)).astype(BF16)
    return hi, mid, lo


def _log_sigmoid(x):
    return jnp.minimum(x, 0.0) - jnp.log(1.0 + jnp.exp(-jnp.abs(x)))


def _bias_tiles_kernel(rb_ref, out_ref):
    h = pl.program_id(0)
    blk = out_ref.shape[-1]
    r = lax.broadcasted_iota(jnp.int32, (blk, blk), 0)
    c = lax.broadcasted_iota(jnp.int32, (blk, blk), 1)
    far = rb_ref[REL_BUCKETS - 1, h]
    max_exact = REL_BUCKETS // 2
    for j in range(2):
        rel = c - r - j * blk
        n = jnp.maximum(-rel, 0)
        nf = jnp.maximum(n, 1).astype(F32)
        large = max_exact + (jnp.log(nf / max_exact) / math.log(REL_MAX_DIST / max_exact)
                             * (REL_BUCKETS - max_exact)).astype(jnp.int32)
        large = jnp.minimum(large, REL_BUCKETS - 1)
        bucket = jnp.where(n < max_exact, n, large)
        val = jnp.zeros((blk, blk), F32)
        for b in range(REL_BUCKETS):
            val = jnp.where(bucket == b, rb_ref[b, h], val)
        val = val - far
        if j == 0:
            val = jnp.where(rel <= 0, val, NEG_BIG)
        out_ref[0, j] = val


def _bias_tiles(rel_bias, blk):
    return pl.pallas_call(
        _bias_tiles_kernel,
        out_shape=jax.ShapeDtypeStruct((DA_HEADS, 2, blk, blk), F32),
        grid=(DA_HEADS,),
        in_specs=[pl.BlockSpec(memory_space=pltpu.SMEM)],
        out_specs=pl.BlockSpec((1, 2, blk, blk), lambda h: (h, 0, 0, 0)),
        compiler_params=_cparams("arbitrary"),
        name="bias_tiles",
    )(rel_bias)


def _inproj_even_kernel(x_ref, g_ref, w_ref, wgc_ref, wgr_ref, p_ref, gc_ref, gr_ref):
    h = _rms(x_ref[...], g_ref[...]).astype(BF16)
    n = w_ref.shape[1]
    for j in range(n // 512):
        sl = slice(j * 512, (j + 1) * 512)
        p_ref[:, sl] = _dot(h, w_ref[:, sl]).astype(BF16)
    gc_ref[...] = _dot(h, wgc_ref[...])
    gr_ref[...] = lax.dot_general(wgr_ref[...], h, _NT, preferred_element_type=F32)


def _inproj_even(x2d, g, w_main, wg_col, wg_row):
    t, d = x2d.shape
    n = w_main.shape[1]
    tm = INPROJ_ROWS
    return pl.pallas_call(
        _inproj_even_kernel,
        out_shape=(jax.ShapeDtypeStruct((t, n), BF16),
                   jax.ShapeDtypeStruct((t, LANES), F32),
                   jax.ShapeDtypeStruct((SUBLANES, t), F32)),
        grid=(t // tm,),
        in_specs=[pl.BlockSpec((tm, d), lambda i: (i, 0)),
                  pl.BlockSpec((1, d), lambda i: (0, 0)),
                  pl.BlockSpec((d, n), lambda i: (0, 0)),
                  pl.BlockSpec((d, LANES), lambda i: (0, 0)),
                  pl.BlockSpec((SUBLANES, d), lambda i: (0, 0))],
        out_specs=(pl.BlockSpec((tm, n), lambda i: (i, 0)),
                   pl.BlockSpec((tm, LANES), lambda i: (i, 0)),
                   pl.BlockSpec((SUBLANES, tm), lambda i: (0, i))),
        compiler_params=_cparams("arbitrary"),
        name="inproj_even",
    )(x2d, g, w_main, wg_col, wg_row)


def _mlstm_kernel(q_ref, k_ref, v_ref, og_ref, gc_ref, *rest):
    gr_refs = rest[:MLSTM_BATCH]
    bc_ref, br_ref, hn_ref, out_ref, c_ref, m_ref = rest[MLSTM_BATCH:]

    @pl.when(pl.program_id(1) == 0)
    def _():
        c_ref[...] = jnp.zeros_like(c_ref)
        m_ref[...] = jnp.zeros_like(m_ref)

    for bi, gr_ref in enumerate(gr_refs):
        _mlstm_chunk(bi, q_ref, k_ref, v_ref, og_ref, gc_ref, gr_ref, bc_ref, br_ref, hn_ref, out_ref, c_ref, m_ref)


def _mlstm_chunk(bi, q_ref, k_ref, v_ref, og_ref, gc_ref, gr_ref, bc_ref, br_ref, hn_ref, out_ref, c_ref, m_ref):
    L = q_ref.shape[1]
    dh = M_HEAD_DIM
    scale = dh ** -0.5

    gc = gc_ref[bi] + bc_ref[...]
    gr = gr_ref[...] + br_ref[...]
    lf_c = _log_sigmoid(gc)
    lf_r = _log_sigmoid(gr)

    src = lax.broadcasted_iota(jnp.int32, (L, L), 0)
    qry = lax.broadcasted_iota(jnp.int32, (L, L), 1)
    visible = src <= qry
    tril = jnp.where(src >= qry, 1.0, 0.0).astype(BF16)
    triu = jnp.where(visible, 1.0, 0.0).astype(BF16)
    b_c = sum(_dot(tril, part) for part in _split3(lf_c))
    b_r = sum(_dot(part, triu) for part in _split3(lf_r))
    u_c = gc - pltpu.roll(b_c, LANES - M_HEADS, axis=1)
    u_parts = jnp.concatenate(_split3(u_c), axis=1)
    sel_row = lax.broadcasted_iota(jnp.int32, (3 * LANES, L), 0) % LANES

    sub = lax.broadcasted_iota(jnp.int32, (dh, L), 0)
    ones_row = jnp.where(sub == 0, 1.0, 0.0).astype(BF16)

    for h in range(M_HEADS):
        hs = slice(h * dh, (h + 1) * dh)
        st = bi * M_HEADS + h
        q = q_ref[bi, :, hs]
        k = k_ref[bi, :, hs]
        kT = k.astype(F32).T
        vT = v_ref[bi, :, hs].astype(F32).T.astype(BF16)
        i_r = gr[h:h + 1, :]
        b_rh = b_r[M_HEADS + h:M_HEADS + h + 1, :]
        m_prev = m_ref[st:st + 1, 0:1]

        u_bc = _dot(u_parts, jnp.where(sel_row == h, 1.0, 0.0).astype(BF16))
        dT = jnp.where(visible, u_bc + b_rh, NEG_BIG)
        inter = b_rh + m_prev
        mt = jnp.maximum(inter, jnp.max(dT, axis=0, keepdims=True))
        wT = jnp.exp(dT - mt)
        sc = jnp.exp(inter - mt)
        sT = lax.dot_general(k, q, _NT, preferred_element_type=F32)
        qkT = (sT * (wT * scale)).astype(BF16)

        vaugT = jnp.concatenate([vT, ones_row], axis=0)
        caugT = c_ref[st]
        numT = sc * lax.dot_general(caugT.astype(BF16), q, _NT, preferred_element_type=F32) + _dot(vaugT, qkT)
        den = numT[dh:dh + 1, :]
        hT = numT[:dh, :] / jnp.maximum(jnp.abs(den), jnp.exp(-mt))

        mu = jnp.mean(hT, axis=0, keepdims=True)
        cen = hT - mu
        var = jnp.mean(cen * cen, axis=0, keepdims=True)
        hn = (cen * lax.rsqrt(var + NORM_EPS)).T * hn_ref[:, hs]
        gate = jax.nn.sigmoid(og_ref[bi, :, hs].astype(F32))
        out_ref[bi, :, hs] = (hn * gate).astype(out_ref.dtype)

        g_end = b_rh[:, L - 1:L]
        ws = g_end - b_rh + i_r
        m_new = jnp.maximum(g_end + m_prev, jnp.max(ws, axis=1, keepdims=True))
        a = jnp.exp(g_end + m_prev - m_new)
        wkT = (kT * (jnp.exp(ws - m_new) * scale)).astype(BF16)
        c_ref[st] = a * caugT + lax.dot_general(vaugT, wkT, _NT, preferred_element_type=F32)
        m_ref[st:st + 1, :] = jnp.broadcast_to(m_new, (1, LANES))


def _mlstm(p3, gc3, gr, b_col, b_row, head_norm):
    bsz, s, _ = p3.shape
    L = MLSTM_CHUNK
    w = M_HEADS * M_HEAD_DIM
    nc = s // L
    nb = MLSTM_BATCH
    blk = lambda j: pl.BlockSpec((nb, L, w), lambda b, c, j=j: (b, c, j))
    gate_rows = lambda bi: pl.BlockSpec((SUBLANES, L), lambda b, c, bi=bi: (0, (nb * b + bi) * nc + c))
    return pl.pallas_call(
        _mlstm_kernel,
        out_shape=jax.ShapeDtypeStruct((bsz, s, w), BF16),
        grid=(bsz // nb, nc),
        in_specs=[blk(0), blk(1), blk(2), blk(3),
                  pl.BlockSpec((nb, L, LANES), lambda b, c: (b, c, 0)),
                  *[gate_rows(bi) for bi in range(nb)],
                  pl.BlockSpec((1, LANES), lambda b, c: (0, 0)),
                  pl.BlockSpec((SUBLANES, 1), lambda b, c: (0, 0)),
                  pl.BlockSpec((1, w), lambda b, c: (0, 0))],
        out_specs=pl.BlockSpec((nb, L, w), lambda b, c: (b, c, 0)),
        scratch_shapes=[pltpu.VMEM((nb * M_HEADS, 2 * M_HEAD_DIM, M_HEAD_DIM), F32),
                        pltpu.VMEM((nb * M_HEADS, LANES), F32)],
        compiler_params=_cparams("arbitrary", "arbitrary"),
        name="mlstm",
    )(p3, p3, p3, p3, gc3, *([gr] * nb), b_col, b_row, head_norm)


def _attn_kernel(lam_ref, q_ref, k_ref, v_ref, bias_ref, sub_ref, out_ref, vaug_all_ref, *, out_scale, blk):
    for hh in range(ATT_HEADS_PER_STEP):
        _attn_head(hh, lam_ref, q_ref, k_ref, v_ref, bias_ref, sub_ref, out_ref, vaug_all_ref.at[hh],
                   out_scale=out_scale, blk=blk)


def _attn_head(hh, lam_ref, q_ref, k_ref, v_ref, bias_ref, sub_ref, out_ref, vaug_ref, *, out_scale, blk):
    s_len = q_ref.shape[1]
    d = DA_HEAD_DIM
    cs = slice(hh * LANES, (hh + 1) * LANES)
    lane_s = lax.broadcasted_iota(jnp.int32, (s_len, DA_V_DIM), 1)
    vaug_ref[:, :DA_V_DIM] = v_ref[0, :, cs]
    vaug_ref[:, DA_V_DIM:] = jnp.where(lane_s == 0, 1.0, 0.0).astype(BF16)
    lam = lam_ref[0]
    lane = lax.broadcasted_iota(jnp.int32, (blk, LANES), 1)
    for qi in range(s_len // blk):
        q0, q1 = qi * blk, (qi + 1) * blk
        qs = q_ref[0, q0:q1, cs] * (d ** -0.5)
        zero = jnp.zeros_like(qs)
        heads = []
        for qm in (jnp.where(lane < d, qs, zero), jnp.where(lane >= d, qs, zero)):
            score = lambda a, b: lax.dot_general(qm, k_ref[0, a:b, cs], _NT, preferred_element_type=F32)
            parts = [score(q0, q1) + bias_ref[hh, 0]]
            if qi >= 1:
                parts.append(score(q0 - blk, q0) + bias_ref[hh, 1])
            if qi >= 2:
                parts.append(score(0, q0 - blk))
            m = functools.reduce(jnp.maximum, [jnp.max(p, axis=-1, keepdims=True) for p in parts])
            pexp = [jnp.exp(p - m).astype(BF16) for p in parts]
            o = _dot(pexp[0], vaug_ref[q0:q1, :])
            if qi >= 1:
                o = o + _dot(pexp[1], vaug_ref[q0 - blk:q0, :])
            if qi >= 2:
                o = o + _dot(pexp[2], vaug_ref[0:q0 - blk, :])
            heads.append(o[:, :DA_V_DIM] / o[:, DA_V_DIM:DA_V_DIM + 1])
        o = heads[0] - lam * heads[1]
        ms = jnp.mean(o * o, axis=-1, keepdims=True)
        o = o * lax.rsqrt(ms + NORM_EPS) * sub_ref[...] * out_scale
        out_ref[0, q0:q1, cs] = o.astype(out_ref.dtype)


def _attention(lam, p3, bias_tiles, subln, out_scale):
    bsz, s, _ = p3.shape
    blk = ATT_BLOCK
    w = DA_HEADS * DA_V_DIM
    q_blk = 4 * M_HEADS
    k_blk = q_blk + DA_HEADS
    v_blk = k_blk + DA_HEADS
    nh = ATT_HEADS_PER_STEP
    assert q_blk % nh == 0 and k_blk % nh == 0 and v_blk % nh == 0 and DA_HEADS % nh == 0
    col = lambda base: pl.BlockSpec((1, s, nh * LANES), lambda b, h: (b, 0, base // nh + h))
    return pl.pallas_call(
        functools.partial(_attn_kernel, out_scale=out_scale, blk=blk),
        out_shape=jax.ShapeDtypeStruct((bsz, s, w), BF16),
        grid=(bsz, DA_HEADS // nh),
        in_specs=[pl.BlockSpec(memory_space=pltpu.SMEM),
                  col(q_blk), col(k_blk), col(v_blk),
                  pl.BlockSpec((nh, 2, blk, blk), lambda b, h: (h, 0, 0, 0)),
                  pl.BlockSpec((1, DA_V_DIM), lambda b, h: (0, 0))],
        out_specs=pl.BlockSpec((1, s, nh * DA_V_DIM), lambda b, h: (b, 0, h)),
        scratch_shapes=[pltpu.VMEM((nh, s, 2 * DA_V_DIM), BF16)],
        compiler_params=_cparams("arbitrary", "arbitrary"),
        name="diff_attention",
    )(lam, p3, p3, p3, bias_tiles, subln)


def _pack_bf16_pairs(v):
    half = v.shape[1] // 2
    hi = lax.bitcast_convert_type(v[:, :half].astype(F32), jnp.uint32)
    lo = lax.bitcast_convert_type(v[:, half:].astype(F32), jnp.uint32)
    return hi | (lo >> 16)


def _unpack_bf16_pairs(w):
    hi = lax.bitcast_convert_type(w & jnp.uint32(0xFFFF0000), F32)
    lo = lax.bitcast_convert_type(w << 16, F32)
    return hi, lo


def _init_counts(cnt_ref):
    @pl.when(pl.program_id(0) == 0)
    def _():
        cnt_ref[...] = jnp.zeros_like(cnt_ref)


def _router_epilogue(xn, tile, g_ref, wr_ref, br_ref, triu_ref, hrow_ref, route_ref, cnt_ref):
    tm, d = xn.shape
    h2 = _rms(xn, g_ref[...]).astype(BF16)
    hrow_ref[:, :d // 2] = _pack_bf16_pairs(h2)

    lg = lax.dot_general(wr_ref[...], h2, _NT, preferred_element_type=F32) + br_ref[...]
    gl = [lg[g:g + 1, :] for g in range(MOE_GROUPS)]
    gmax = functools.reduce(jnp.maximum, gl)
    gsel = jnp.where(gl[0] == gmax, 0.0, jnp.where(gl[1] == gmax, 1.0, jnp.where(gl[2] == gmax, 2.0, 3.0)))
    pg = 1.0 / functools.reduce(lambda a, b: a + b, [jnp.exp(x - gmax) for x in gl])

    ev = []
    for j in range(EXPERTS_PER_GROUP):
        acc = jnp.zeros_like(gmax)
        for g in range(MOE_GROUPS):
            r0 = SUBLANES + g * EXPERTS_PER_GROUP + j
            acc = jnp.where(gsel == float(g), lg[r0:r0 + 1, :], acc)
        ev.append(acc)
    v1 = functools.reduce(jnp.maximum, ev)
    i1 = jnp.where(ev[0] == v1, 0.0, jnp.where(ev[1] == v1, 1.0, jnp.where(ev[2] == v1, 2.0, 3.0)))
    ev2 = [jnp.where(i1 == float(j), -jnp.inf, ev[j]) for j in range(EXPERTS_PER_GROUP)]
    v2 = functools.reduce(jnp.maximum, ev2)
    hit = [(ev2[j] == v2) & (i1 != float(j)) for j in range(EXPERTS_PER_GROUP)]
    i2 = jnp.where(hit[0], 0.0, jnp.where(hit[1], 1.0, jnp.where(hit[2], 2.0, 3.0)))
    e21 = jnp.exp(v2 - v1)
    w1 = 1.0 / (1.0 + e21)
    w2 = e21 * w1
    first_low = i1 < i2
    lo = jnp.minimum(i1, i2)
    hi = jnp.maximum(i1, i2)
    c_lo = pg * jnp.where(first_low, w1, w2)
    c_hi = pg * jnp.where(first_low, w2, w1)
    pair = lo * (7.0 - lo) * 0.5 + hi - lo - 1.0
    bucket = gsel * float(N_PAIRS) + pair

    rows = lax.broadcasted_iota(jnp.int32, (32, tm), 0).astype(F32)
    onehot = rows == bucket
    cum = _dot(jnp.where(onehot, 1.0, 0.0).astype(BF16), triu_ref[...])
    carry = cnt_ref[:, 0:1]
    rank = jnp.sum(jnp.where(onehot, cum + carry, 0.0), axis=0, keepdims=True) - 1.0
    cnt_ref[...] = jnp.broadcast_to(carry + cum[:, tm - 1:tm], cnt_ref.shape)

    token = (tile * tm + 1 + lax.broadcasted_iota(jnp.int32, (1, tm), 1)).astype(F32)
    rid = lax.broadcasted_iota(jnp.int32, (LANES, tm), 0)
    rec = jnp.where(rid == 0, bucket, jnp.where(rid == 1, c_lo, jnp.where(rid == 2, c_hi,
                    jnp.where(rid == 3, rank, jnp.where(rid == 4, token, 0.0)))))
    hrow_ref[:, d // 2:] = lax.bitcast_convert_type(rec.T, jnp.uint32)
    route_ref[...] = rec[:SUBLANES, :]


_EPI_OUT_SHAPES = lambda t, d: (jax.ShapeDtypeStruct((t, d), F32),
                                jax.ShapeDtypeStruct((t, d // 2 + LANES), jnp.uint32),
                                jax.ShapeDtypeStruct((SUBLANES, t), F32),
                                jax.ShapeDtypeStruct((32, LANES), F32))


def _epi_out_specs(tm, d):
    return (pl.BlockSpec((tm, d), lambda i: (i, 0)),
            pl.BlockSpec((tm, d // 2 + LANES), lambda i: (i, 0)),
            pl.BlockSpec((SUBLANES, tm), lambda i: (0, i)),
            pl.BlockSpec((32, LANES), lambda i: (0, 0)))


def _epi_in_specs(tm, d):
    return [pl.BlockSpec((1, d), lambda i: (0, 0)),
            pl.BlockSpec((32, d), lambda i: (0, 0)),
            pl.BlockSpec((32, 1), lambda i: (0, 0)),
            pl.BlockSpec((tm, tm), lambda i: (0, 0))]


def _outproj_even_kernel(hm_ref, ha_ref, x_ref, wo_ref, g_ref, wr_ref, br_ref, triu_ref,
                         xn_ref, hrow_ref, route_ref, cnt_ref):
    _init_counts(cnt_ref)
    half = hm_ref.shape[1]
    tm = triu_ref.shape[0]
    sub = x_ref.shape[0] // tm
    for j in range(sub):
        rs = slice(j * tm, (j + 1) * tm)
        y = _dot(hm_ref[rs, :], wo_ref[:half, :]) + _dot(ha_ref[rs, :], wo_ref[half:, :])
        xn = x_ref[rs, :] + y
        xn_ref[rs, :] = xn
        _router_epilogue(xn, pl.program_id(0) * sub + j, g_ref, wr_ref, br_ref, triu_ref,
                         hrow_ref.at[rs, :], route_ref.at[:, rs], cnt_ref)


def _outproj_even(hm, ha, x2d, w_out, g, wr, br, triu):
    t, d = x2d.shape
    tm = OUTPROJ_SUBTILES * ROW_TILE
    half = hm.shape[1]
    return pl.pallas_call(
        _outproj_even_kernel,
        out_shape=_EPI_OUT_SHAPES(t, d),
        grid=(t // tm,),
        in_specs=[pl.BlockSpec((tm, half), lambda i: (i, 0)),
                  pl.BlockSpec((tm, half), lambda i: (i, 0)),
                  pl.BlockSpec((tm, d), lambda i: (i, 0)),
                  pl.BlockSpec((2 * half, d), lambda i: (0, 0))] + _epi_in_specs(ROW_TILE, d),
        out_specs=_epi_out_specs(tm, d),
        compiler_params=_cparams("arbitrary"),
        name="outproj_even",
    )(hm, ha, x2d, w_out, g, wr, br, triu)


def _conv_mixer_kernel(x_ref, y_ref, gm_ref, wi_ref, cw_ref, wo_ref, g_ref, wr_ref, br_ref, triu_ref,
                       xn_ref, h2_ref, meta_ref, cnt_ref, tail_ref, *, tiles_per_seq):
    _init_counts(cnt_ref)
    tm = triu_ref.shape[0]
    d = x_ref.shape[1]
    sub = x_ref.shape[0] // tm
    assert tiles_per_seq % sub == 0

    @pl.when((pl.program_id(0) * sub) % tiles_per_seq == 0)
    def _():
        tail_ref[...] = jnp.zeros_like(tail_ref)

    row = lax.broadcasted_iota(jnp.int32, (tm, 512), 0)
    for s in range(sub):
        rs = slice(s * tm, (s + 1) * tm)
        x = x_ref[rs, :] + jnp.concatenate(_unpack_bf16_pairs(y_ref[rs, :]), axis=1)
        h = _rms(x, gm_ref[...]).astype(BF16)
        acc = jnp.zeros((tm, d), F32)
        for j in range(d // 512):
            sl = slice(j * 512, (j + 1) * 512)
            bg = _dot(h, wi_ref[:, sl])
            cg = _dot(h, wi_ref[:, d + j * 512:d + (j + 1) * 512])
            u = _dot(h, wi_ref[:, 2 * d + j * 512:2 * d + (j + 1) * 512])
            z = cg * u
            prev1 = tail_ref[SUBLANES - 1:SUBLANES, sl]
            prev2 = tail_ref[SUBLANES - 2:SUBLANES - 1, sl]
            z1 = jnp.where(row == 0, prev1, pltpu.roll(z, 1, axis=0))
            z2 = jnp.where(row == 0, prev2, jnp.where(row == 1, prev1, pltpu.roll(z, 2, axis=0)))
            zc = cw_ref[0:1, sl] * z2 + cw_ref[1:2, sl] * z1 + cw_ref[2:3, sl] * z
            tail_ref[:, sl] = z[tm - SUBLANES:, :]
            acc = acc + _dot((bg * zc).astype(BF16), wo_ref[sl, :])
        xn = x + acc
        xn_ref[rs, :] = xn
        _router_epilogue(xn, pl.program_id(0) * sub + s, g_ref, wr_ref, br_ref, triu_ref,
                         h2_ref.at[rs, :], meta_ref.at[:, rs], cnt_ref)


def _conv_mixer(x2d, ymoe, gm, w_in, conv_w, w_out, g, wr, br, triu, seq):
    t, d = x2d.shape
    tm = CONV_SUBTILES * ROW_TILE
    const = lambda shape: pl.BlockSpec(shape, lambda i: (0, 0), pipeline_mode=pl.Buffered(1))
    return pl.pallas_call(
        functools.partial(_conv_mixer_kernel, tiles_per_seq=seq // ROW_TILE),
        out_shape=_EPI_OUT_SHAPES(t, d),
        grid=(t // tm,),
        in_specs=[pl.BlockSpec((tm, d), lambda i: (i, 0)),
                  pl.BlockSpec((tm, d // 2), lambda i: (i, 0)),
                  pl.BlockSpec((1, d), lambda i: (0, 0)),
                  const((d, 3 * d)),
                  pl.BlockSpec((SUBLANES, d), lambda i: (0, 0)),
                  const((d, d))] + _epi_in_specs(ROW_TILE, d),
        out_specs=_epi_out_specs(tm, d),
        scratch_shapes=[pltpu.VMEM((SUBLANES, d), F32)],
        compiler_params=_cparams("arbitrary"),
        name="conv_mixer",
    )(x2d, ymoe, gm, w_in, conv_w, w_out, g, wr, br, triu)


def _row_scatter_kernel(ztile_ref, zflag_ref, dst_ref, h_hbm, out_hbm, buf0, buf1, buf2, lsem, ssem, zsem):
    i = pl.program_id(0)
    n = pl.num_programs(0)
    bufs = (buf0, buf1, buf2)
    tm = buf0.shape[0]

    def load(j, s):
        return pltpu.make_async_copy(h_hbm.at[pl.ds(pl.multiple_of(j * tm, tm), tm)], bufs[s], lsem.at[s])

    wait_rows = lambda s: pltpu.make_async_copy(bufs[s], out_hbm.at[pl.ds(0, tm)], ssem.at[s]).wait()

    @pl.when(i == 0)
    def _():
        buf2[...] = jnp.zeros_like(buf2)
        fill = lambda k: pltpu.make_async_copy(
            buf2, out_hbm.at[pl.ds(pl.multiple_of(ztile_ref[k] * tm, tm), tm)], zsem)
        for k in range(ztile_ref.shape[0]):
            @pl.when(zflag_ref[k] != 0)
            def _(k=k):
                fill(k).start()
        for k in range(ztile_ref.shape[0]):
            @pl.when(zflag_ref[k] != 0)
            def _(k=k):
                fill(k).wait()
        load(0, 0).start()

    for s in range(3):
        @pl.when(i % 3 == s)
        def _(s=s):
            @pl.when(i + 1 < n)
            def _():
                load(i + 1, (s + 1) % 3).start()

            load(i, s).wait()
            for r in range(tm):
                pltpu.make_async_copy(bufs[s].at[pl.ds(r, 1)], out_hbm.at[pl.ds(dst_ref[0, 0, r], 1)],
                                      ssem.at[s]).start(priority=r % 2)

            @pl.when(i >= 1)
            def _():
                wait_rows((s + 2) % 3)

            @pl.when(i == n - 1)
            def _():
                wait_rows(s)


def _row_scatter(hrows, dest, zero_tiles, zero_flags, n_rows):
    t, wrow = hrows.shape
    tm = MOE_TILE
    return pl.pallas_call(
        _row_scatter_kernel,
        out_shape=jax.ShapeDtypeStruct((n_rows, wrow), hrows.dtype),
        grid_spec=pltpu.PrefetchScalarGridSpec(
            num_scalar_prefetch=2,
            grid=(t // tm,),
            in_specs=[pl.BlockSpec((1, 1, tm), lambda i, *_: (i, 0, 0), memory_space=pltpu.SMEM),
                      pl.BlockSpec(memory_space=pl.ANY)],
            out_specs=pl.BlockSpec(memory_space=pl.ANY),
            scratch_shapes=[pltpu.VMEM((tm, wrow), hrows.dtype),
                            pltpu.VMEM((tm, wrow), hrows.dtype),
                            pltpu.VMEM((tm, wrow), hrows.dtype),
                            pltpu.SemaphoreType.DMA((3,)),
                            pltpu.SemaphoreType.DMA((3,)),
                            pltpu.SemaphoreType.DMA]),
        compiler_params=_cparams("arbitrary"),
        name="row_scatter",
    )(zero_tiles, zero_flags, dest.reshape(t // tm, 1, tm), hrows)


def _moe_kernel(elo_ref, ehi_ref, valid_ref, fresh_ref,
                x_ref,
                wg_lo, wu_lo, wd_lo, wg_hi, wu_hi, wd_hi,
                y_hbm,
                obuf0, obuf1, idv0, idv1, ids0, ids1, wbuf_in, wbuf_out, ssem, isem):
    i = pl.program_id(0)
    tm, half = obuf0.shape
    n_tok = y_hbm.shape[0] - 2 * tm
    obufs, idvs, idss = (obuf0, obuf1), (idv0, idv1), (ids0, ids1)
    valid = valid_ref[i] != 0
    prev_valid = valid_ref[jnp.maximum(i - 1, 0)] != 0
    spare = n_tok + lax.broadcasted_iota(jnp.int32, (SUBLANES, tm), 1)

    ids_copy = lambda s: pltpu.make_async_copy(idvs[s], idss[s], isem.at[s])
    row_scatter = lambda r, s: pltpu.make_async_copy(obufs[s].at[pl.ds(r, 1)],
                                                     y_hbm.at[pl.ds(idss[s][0, r], 1)], ssem.at[s])
    wait_scatter = lambda s: pltpu.make_async_copy(obufs[s], y_hbm.at[pl.ds(0, tm)], ssem.at[s]).wait()

    @pl.when(i == 0)
    def _():
        obuf0[...] = jnp.zeros_like(obuf0)
        obuf1[...] = jnp.zeros_like(obuf1)
        idv1[...] = spare
        ids_copy(1).start()
        pltpu.make_async_copy(obuf0, y_hbm.at[pl.ds(n_tok + tm, tm)], ssem.at[0]).start()

    @pl.when(fresh_ref[i] != 0)
    def _():
        for k, w in enumerate((wg_lo, wu_lo, wg_hi, wu_hi)):
            wbuf_in[k] = w[0].astype(BF16)
        for k, w in enumerate((wd_lo, wd_hi)):
            wbuf_out[k] = w[0].astype(BF16)

    for s in range(2):
        @pl.when(valid & (i % 2 == s))
        def _(s=s):
            rows = x_ref[...]
            rec = lax.bitcast_convert_type(rows[:, half:], F32)
            token = rec.T[4:5, :]
            idvs[s][...] = jnp.where(token > 0.0, token.astype(jnp.int32) - 1, spare)
            ids_copy(s).start()
            ids_copy(1 - s).wait()
            for r in range(tm):
                row_scatter(r, 1 - s).start(priority=r % 2)
            xa, xb = _unpack_bf16_pairs(rows[:, :half])
            x = jnp.concatenate([xa.astype(BF16), xb.astype(BF16)], axis=1)

            def ffn(k):
                a = _dot(x, wbuf_in[2 * k])
                a = (a * jax.nn.sigmoid(a)) * _dot(x, wbuf_in[2 * k + 1])
                return _dot(a.astype(BF16), wbuf_out[k])

            out = rec[:, 1:2] * ffn(0) + rec[:, 2:3] * ffn(1)
            wait_scatter(s)
            obufs[s][...] = _pack_bf16_pairs(out.astype(BF16))

        @pl.when(jnp.logical_not(valid) & prev_valid & (i % 2 == s))
        def _(s=s):
            wait_scatter(s)
            ids_copy(1 - s).wait()

            @pl.loop(0, tm)
            def _(r):
                row_scatter(r, 1 - s).start()

            wait_scatter(1 - s)


def _moe(tiles, xs, n_tok, w_gate, w_up, w_down):
    tile_elo, tile_ehi, tile_valid, tile_fresh = tiles
    n_tiles = tile_elo.shape[0]
    wrow = xs.shape[1]
    d = w_gate.shape[1]
    ff = w_gate.shape[2]
    tm = MOE_TILE
    lo = lambda i, elo, *_: (elo[i], 0, 0)
    hi = lambda i, elo, ehi, *_: (ehi[i], 0, 0)
    return pl.pallas_call(
        _moe_kernel,
        out_shape=jax.ShapeDtypeStruct((n_tok + 2 * tm, d // 2), jnp.uint32),
        grid_spec=pltpu.PrefetchScalarGridSpec(
            num_scalar_prefetch=4,
            grid=(n_tiles,),
            in_specs=[pl.BlockSpec((tm, wrow), lambda i, *_: (i, 0)),
                      pl.BlockSpec((1, d, ff), lo), pl.BlockSpec((1, d, ff), lo), pl.BlockSpec((1, ff, d), lo),
                      pl.BlockSpec((1, d, ff), hi), pl.BlockSpec((1, d, ff), hi), pl.BlockSpec((1, ff, d), hi)],
            out_specs=pl.BlockSpec(memory_space=pl.ANY),
            scratch_shapes=[pltpu.VMEM((tm, d // 2), jnp.uint32),
                            pltpu.VMEM((tm, d // 2), jnp.uint32),
                            pltpu.VMEM((SUBLANES, tm), jnp.int32),
                            pltpu.VMEM((SUBLANES, tm), jnp.int32),
                            pltpu.SMEM((SUBLANES, tm), jnp.int32),
                            pltpu.SMEM((SUBLANES, tm), jnp.int32),
                            pltpu.VMEM((4, d, ff), BF16),
                            pltpu.VMEM((2, ff, d), BF16),
                            pltpu.SemaphoreType.DMA((2,)),
                            pltpu.SemaphoreType.DMA((2,))]),
        compiler_params=_cparams("arbitrary"),
        name="grouped_moe",
    )(tile_elo, tile_ehi, tile_valid, tile_fresh, xs, w_gate, w_up, w_down, w_gate, w_up, w_down)


def _final_kernel(x_ref, y_ref, g_ref, out_ref):
    out_ref[...] = _rms(x_ref[...] + jnp.concatenate(_unpack_bf16_pairs(y_ref[...]), axis=1), g_ref[...])


def _final(x2d, ymoe, g):
    t, d = x2d.shape
    tm = FINAL_ROWS
    return pl.pallas_call(
        _final_kernel,
        out_shape=jax.ShapeDtypeStruct((t, d), F32),
        grid=(t // tm,),
        in_specs=[pl.BlockSpec((tm, d), lambda i: (i, 0)),
                  pl.BlockSpec((tm, d // 2), lambda i: (i, 0)),
                  pl.BlockSpec((1, d), lambda i: (0, 0))],
        out_specs=pl.BlockSpec((tm, d), lambda i: (i, 0)),
        compiler_params=_cparams("arbitrary"),
        name="final_norm",
    )(x2d, ymoe, g)


_PAIRS = [(a, b) for a in range(EXPERTS_PER_GROUP) for b in range(a + 1, EXPERTS_PER_GROUP)]
_BUCKET_ELO = np.array([g * EXPERTS_PER_GROUP + _PAIRS[p][0] for g in range(MOE_GROUPS) for p in range(N_PAIRS)], np.int32)
_BUCKET_EHI = np.array([g * EXPERTS_PER_GROUP + _PAIRS[p][1] for g in range(MOE_GROUPS) for p in range(N_PAIRS)], np.int32)


def _moe_layer(hrows, route, cnt, layer, w_gate, w_up, w_down):
    t = hrows.shape[0]
    tm = MOE_TILE
    n_tiles = t // tm + N_BUCKETS
    bucket = route[0].astype(jnp.int32)
    rank = route[3].astype(jnp.int32)
    counts = cnt[:N_BUCKETS, 0].astype(jnp.int32)
    padded = ((counts + tm - 1) // tm) * tm
    ends = jnp.cumsum(padded)
    dest = (ends - padded)[bucket] + rank
    tile_start = jnp.arange(n_tiles, dtype=jnp.int32) * tm
    tile_valid = tile_start < ends[-1]
    last_start = jnp.maximum(ends[-1] - tm, 0)
    tile_bucket = jnp.sum((ends[None, :] <= jnp.minimum(tile_start, last_start)[:, None]).astype(jnp.int32), axis=1)
    prev_bucket = jnp.concatenate([jnp.full((1,), -1, jnp.int32), tile_bucket[:-1]])
    tile_fresh = tile_valid & (tile_bucket != prev_bucket)
    base = layer * N_EXPERTS
    tiles = (jnp.asarray(_BUCKET_ELO)[tile_bucket] + base, jnp.asarray(_BUCKET_EHI)[tile_bucket] + base,
             tile_valid.astype(jnp.int32), tile_fresh.astype(jnp.int32))
    idle = t // tm + jnp.arange(N_BUCKETS, dtype=jnp.int32)
    zero_tiles = jnp.concatenate([jnp.maximum(ends // tm - 1, 0), idle])
    zero_flags = jnp.concatenate([counts % tm != 0, idle * tm >= ends[-1]]).astype(jnp.int32)
    xs = _row_scatter(hrows, dest, zero_tiles, zero_flags, n_tiles * tm)
    return _moe(tiles, xs, t, w_gate, w_up, w_down)


def _router_weights(wg, bg, we, be):
    d = wg.shape[0]
    wr = jnp.zeros((32, d), F32).at[:MOE_GROUPS].set(wg.T).at[SUBLANES:SUBLANES + N_EXPERTS].set(we.T)
    br = jnp.zeros((32, 1), F32).at[:MOE_GROUPS, 0].set(bg).at[SUBLANES:SUBLANES + N_EXPERTS, 0].set(be)
    return wr.astype(BF16), br


def kernel(x, rel_bias, norm_mix, norm_ffn, norm_final, w_in_even, w_out_even, m_igate_b, m_fgate_b,
           m_head_norm, lam_q1, lam_k1, lam_q2, lam_k2, da_subln, w_in_odd, conv_w, w_out_odd,
           router_group_w, router_group_b, router_expert_w, router_expert_b,
           exp_w_gate, exp_w_up, exp_w_down):
    bsz, seq, d = x.shape
    t = bsz * seq
    x2d = x.reshape(t, d)
    mw = M_HEADS * M_HEAD_DIM

    w_in = w_in_even[0]
    gate0 = 4 * mw
    w_main = jnp.concatenate([w_in[:, :gate0], w_in[:, gate0 + 2 * M_HEADS:]], axis=1).astype(BF16)
    w_gates = w_in[:, gate0:gate0 + 2 * M_HEADS]
    wg_col = jnp.zeros((d, LANES), F32).at[:, :2 * M_HEADS].set(w_gates).astype(BF16)
    wg_row = w_gates.T.astype(BF16)
    gate_b = jnp.concatenate([m_igate_b[0], m_fgate_b[0]]).astype(F32)
    b_col = jnp.zeros((1, LANES), F32).at[0, :2 * M_HEADS].set(gate_b)
    b_row = gate_b.reshape(2 * M_HEADS, 1)
    triu = jnp.triu(jnp.ones((ROW_TILE, ROW_TILE), F32)).astype(BF16)

    p, gc, gr = _inproj_even(x2d, norm_mix[0].reshape(1, d), w_main, wg_col, wg_row)
    p3 = p.reshape(bsz, seq, -1)
    hm = _mlstm(p3, gc.reshape(bsz, seq, LANES), gr, b_col, b_row, m_head_norm[0].reshape(1, mw))

    lam_init = 0.8 - 0.6 * math.exp(-0.3 * 0)
    lam = (jnp.exp(jnp.sum(lam_q1[0].astype(F32) * lam_k1[0].astype(F32)))
           - jnp.exp(jnp.sum(lam_q2[0].astype(F32) * lam_k2[0].astype(F32))) + lam_init).reshape(1)
    ha = _attention(lam, p3, _bias_tiles(rel_bias.astype(F32), ATT_BLOCK),
                    da_subln[0].reshape(1, DA_V_DIM).astype(F32), 1.0 - lam_init)

    wr0, br0 = _router_weights(router_group_w[0], router_group_b[0], router_expert_w[0], router_expert_b[0])
    x1, h2, meta, cnt = _outproj_even(hm.reshape(t, mw), ha.reshape(t, -1), x2d, w_out_even[0].astype(BF16),
                                      norm_ffn[0].reshape(1, d), wr0, br0, triu)
    ff = exp_w_gate.shape[-1]
    experts = (exp_w_gate.reshape(-1, d, ff), exp_w_up.reshape(-1, d, ff), exp_w_down.reshape(-1, ff, d))
    y = _moe_layer(h2, meta, cnt, 0, *experts)

    wr1, br1 = _router_weights(router_group_w[1], router_group_b[1], router_expert_w[1], router_expert_b[1])
    cw = jnp.zeros((SUBLANES, d), F32).at[:conv_w.shape[1]].set(conv_w[0])
    x3, h2, meta, cnt = _conv_mixer(x1, y, norm_mix[1].reshape(1, d), w_in_odd[0].astype(BF16), cw,
                                    w_out_odd[0].astype(BF16), norm_ffn[1].reshape(1, d), wr1, br1, triu, seq)
    y = _moe_layer(h2, meta, cnt, 1, *experts)

    return _final(x3, y, norm_final.reshape(1, d)).reshape(bsz, seq, d)
```

```python
import functools
import math

import jax
import jax.numpy as jnp
import numpy as np
from jax import lax
from jax.experimental import pallas as pl
from jax.experimental.pallas import tpu as pltpu

F32 = jnp.float32
BF16 = jnp.bfloat16

NORM_EPS = 1e-6
NEG_BIG = -1e30

M_HEADS = 4
M_HEAD_DIM = 128
DA_HEADS = 4
DA_HEAD_DIM = 64
DA_V_DIM = 128
REL_BUCKETS = 32
REL_MAX_DIST = 128
MOE_GROUPS = 4
EXPERTS_PER_GROUP = 4
N_EXPERTS = 16
N_PAIRS = 6
N_BUCKETS = MOE_GROUPS * N_PAIRS

LANES = 128
SUBLANES = 8
VMEM_LIMIT = 56 * 1024 * 1024

ROW_TILE = 512
OUTPROJ_SUBTILES = 2
INPROJ_ROWS = 1024
CONV_SUBTILES = 2
FINAL_ROWS = 1024
MLSTM_CHUNK = 256
MLSTM_BATCH = 4
ATT_BLOCK = 256
ATT_HEADS_PER_STEP = 2
MOE_TILE = 512

_NT = (((1,), (1,)), ((), ()))


def _cparams(*sem):
    return pltpu.CompilerParams(dimension_semantics=sem, vmem_limit_bytes=VMEM_LIMIT)


def _rms(x, g):
    ms = jnp.mean(x * x, axis=-1, keepdims=True)
    return x * lax.rsqrt(ms + NORM_EPS) * g


def _dot(a, b):
    return jnp.dot(a, b, preferred_element_type=F32)


def _split3(x):
    hi = x.astype(BF16)
    r1 = x - hi.astype(F32)
    mid = r1.astype(BF16)
    lo = (r1 - mid.astype(F32)).astype(BF16)
    return hi, mid, lo


def _log_sigmoid(x):
    return jnp.minimum(x, 0.0) - jnp.log(1.0 + jnp.exp(-jnp.abs(x)))


def _bias_tiles_kernel(rb_ref, out_ref):
    h = pl.program_id(0)
    blk = out_ref.shape[-1]
    r = lax.broadcasted_iota(jnp.int32, (blk, blk), 0)
    c = lax.broadcasted_iota(jnp.int32, (blk, blk), 1)
    far = rb_ref[REL_BUCKETS - 1, h]
    max_exact = REL_BUCKETS // 2
    for j in range(2):
        rel = c - r - j * blk
        n = jnp.maximum(-rel, 0)
        nf = jnp.maximum(n, 1).astype(F32)
        large = max_exact + (jnp.log(nf / max_exact) / math.log(REL_MAX_DIST / max_exact)
                             * (REL_BUCKETS - max_exact)).astype(jnp.int32)
        large = jnp.minimum(large, REL_BUCKETS - 1)
        bucket = jnp.where(n < max_exact, n, large)
        val = jnp.zeros((blk, blk), F32)
        for b in range(REL_BUCKETS):
            val = jnp.where(bucket == b, rb_ref[b, h], val)
        val = val - far
        if j == 0:
            val = jnp.where(rel <= 0, val, NEG_BIG)
        out_ref[0, j] = val


def _bias_tiles(rel_bias, blk):
    return pl.pallas_call(
        _bias_tiles_kernel,
        out_shape=jax.ShapeDtypeStruct((DA_HEADS, 2, blk, blk), F32),
        grid=(DA_HEADS,),
        in_specs=[pl.BlockSpec(memory_space=pltpu.SMEM)],
        out_specs=pl.BlockSpec((1, 2, blk, blk), lambda h: (h, 0, 0, 0)),
        compiler_params=_cparams("arbitrary"),
        name="bias_tiles",
    )(rel_bias)


def _inproj_even_kernel(x_ref, g_ref, w_ref, wgc_ref, wgr_ref, p_ref, gc_ref, gr_ref):
    h = _rms(x_ref[...], g_ref[...]).astype(BF16)
    n = w_ref.shape[1]
    for j in range(n // 512):
        sl = slice(j * 512, (j + 1) * 512)
        p_ref[:, sl] = _dot(h, w_ref[:, sl]).astype(BF16)
    gc_ref[...] = _dot(h, wgc_ref[...])
    gr_ref[...] = lax.dot_general(wgr_ref[...], h, _NT, preferred_element_type=F32)


def _inproj_even(x2d, g, w_main, wg_col, wg_row):
    t, d = x2d.shape
    n = w_main.shape[1]
    tm = INPROJ_ROWS
    return pl.pallas_call(
        _inproj_even_kernel,
        out_shape=(jax.ShapeDtypeStruct((t, n), BF16),
                   jax.ShapeDtypeStruct((t, LANES), F32),
                   jax.ShapeDtypeStruct((SUBLANES, t), F32)),
        grid=(t // tm,),
        in_specs=[pl.BlockSpec((tm, d), lambda i: (i, 0)),
                  pl.BlockSpec((1, d), lambda i: (0, 0)),
                  pl.BlockSpec((d, n), lambda i: (0, 0)),
                  pl.BlockSpec((d, LANES), lambda i: (0, 0)),
                  pl.BlockSpec((SUBLANES, d), lambda i: (0, 0))],
        out_specs=(pl.BlockSpec((tm, n), lambda i: (i, 0)),
                   pl.BlockSpec((tm, LANES), lambda i: (i, 0)),
                   pl.BlockSpec((SUBLANES, tm), lambda i: (0, i))),
        compiler_params=_cparams("arbitrary"),
        name="inproj_even",
    )(x2d, g, w_main, wg_col, wg_row)


def _mlstm_kernel(q_ref, k_ref, v_ref, og_ref, gc_ref, *rest):
    gr_refs = rest[:MLSTM_BATCH]
    bc_ref, br_ref, hn_ref, out_ref, c_ref, m_ref = rest[MLSTM_BATCH:]

    @pl.when(pl.program_id(1) == 0)
    def _():
        c_ref[...] = jnp.zeros_like(c_ref)
        m_ref[...] = jnp.zeros_like(m_ref)

    for bi, gr_ref in enumerate(gr_refs):
        _mlstm_chunk(bi, q_ref, k_ref, v_ref, og_ref, gc_ref, gr_ref, bc_ref, br_ref, hn_ref, out_ref, c_ref, m_ref)


def _mlstm_chunk(bi, q_ref, k_ref, v_ref, og_ref, gc_ref, gr_ref, bc_ref, br_ref, hn_ref, out_ref, c_ref, m_ref):
    L = q_ref.shape[1]
    dh = M_HEAD_DIM
    scale = dh ** -0.5

    gc = gc_ref[bi] + bc_ref[...]
    gr = gr_ref[...] + br_ref[...]
    lf_c = _log_sigmoid(gc)
    lf_r = _log_sigmoid(gr)

    src = lax.broadcasted_iota(jnp.int32, (L, L), 0)
    qry = lax.broadcasted_iota(jnp.int32, (L, L), 1)
    visible = src <= qry
    tril = jnp.where(src >= qry, 1.0, 0.0).astype(BF16)
    triu = jnp.where(visible, 1.0, 0.0).astype(BF16)
    b_c = sum(_dot(tril, part) for part in _split3(lf_c))
    b_r = sum(_dot(part, triu) for part in _split3(lf_r))
    u_c = gc - pltpu.roll(b_c, LANES - M_HEADS, axis=1)
    u_parts = jnp.concatenate(_split3(u_c), axis=1)
    sel_row = lax.broadcasted_iota(jnp.int32, (3 * LANES, L), 0) % LANES

    sub = lax.broadcasted_iota(jnp.int32, (dh, L), 0)
    ones_row = jnp.where(sub == 0, 1.0, 0.0).astype(BF16)

    for h in range(M_HEADS):
        hs = slice(h * dh, (h + 1) * dh)
        st = bi * M_HEADS + h
        q = q_ref[bi, :, hs]
        k = k_ref[bi, :, hs]
        kT = k.astype(F32).T
        vT = v_ref[bi, :, hs].astype(F32).T.astype(BF16)
        i_r = gr[h:h + 1, :]
        b_rh = b_r[M_HEADS + h:M_HEADS + h + 1, :]
        m_prev = m_ref[st:st + 1, 0:1]

        u_bc = _dot(u_parts, jnp.where(sel_row == h, 1.0, 0.0).astype(BF16))
        dT = jnp.where(visible, u_bc + b_rh, NEG_BIG)
        inter = b_rh + m_prev
        mt = jnp.maximum(inter, jnp.max(dT, axis=0, keepdims=True))
        wT = jnp.exp(dT - mt)
        sc = jnp.exp(inter - mt)
        sT = lax.dot_general(k, q, _NT, preferred_element_type=F32)
        qkT = (sT * (wT * scale)).astype(BF16)

        vaugT = jnp.concatenate([vT, ones_row], axis=0)
        caugT = c_ref[st]
        numT = sc * lax.dot_general(caugT.astype(BF16), q, _NT, preferred_element_type=F32) + _dot(vaugT, qkT)
        den = numT[dh:dh + 1, :]
        hT = numT[:dh, :] / jnp.maximum(jnp.abs(den), jnp.exp(-mt))

        mu = jnp.mean(hT, axis=0, keepdims=True)
        cen = hT - mu
        var = jnp.mean(cen * cen, axis=0, keepdims=True)
        hn = (cen * lax.rsqrt(var + NORM_EPS)).T * hn_ref[:, hs]
        gate = jax.nn.sigmoid(og_ref[bi, :, hs].astype(F32))
        out_ref[bi, :, hs] = (hn * gate).astype(out_ref.dtype)

        g_end = b_rh[:, L - 1:L]
        ws = g_end - b_rh + i_r
        m_new = jnp.maximum(g_end + m_prev, jnp.max(ws, axis=1, keepdims=True))
        a = jnp.exp(g_end + m_prev - m_new)
        wkT = (kT * (jnp.exp(ws - m_new) * scale)).astype(BF16)
        c_ref[st] = a * caugT + lax.dot_general(vaugT, wkT, _NT, preferred_element_type=F32)
        m_ref[st:st + 1, :] = jnp.broadcast_to(m_new, (1, LANES))


def _mlstm(p3, gc3, gr, b_col, b_row, head_norm):
    bsz, s, _ = p3.shape
    L = MLSTM_CHUNK
    w = M_HEADS * M_HEAD_DIM
    nc = s // L
    nb = MLSTM_BATCH
    blk = lambda j: pl.BlockSpec((nb, L, w), lambda b, c, j=j: (b, c, j))
    gate_rows = lambda bi: pl.BlockSpec((SUBLANES, L), lambda b, c, bi=bi: (0, (nb * b + bi) * nc + c))
    return pl.pallas_call(
        _mlstm_kernel,
        out_shape=jax.ShapeDtypeStruct((bsz, s, w), BF16),
        grid=(bsz // nb, nc),
        in_specs=[blk(0), blk(1), blk(2), blk(3),
                  pl.BlockSpec((nb, L, LANES), lambda b, c: (b, c, 0)),
                  *[gate_rows(bi) for bi in range(nb)],
                  pl.BlockSpec((1, LANES), lambda b, c: (0, 0)),
                  pl.BlockSpec((SUBLANES, 1), lambda b, c: (0, 0)),
                  pl.BlockSpec((1, w), lambda b, c: (0, 0))],
        out_specs=pl.BlockSpec((nb, L, w), lambda b, c: (b, c, 0)),
        scratch_shapes=[pltpu.VMEM((nb * M_HEADS, 2 * M_HEAD_DIM, M_HEAD_DIM), F32),
                        pltpu.VMEM((nb * M_HEADS, LANES), F32)],
        compiler_params=_cparams("arbitrary", "arbitrary"),
        name="mlstm",
    )(p3, p3, p3, p3, gc3, *([gr] * nb), b_col, b_row, head_norm)


def _attn_kernel(lam_ref, q_ref, k_ref, v_ref, bias_ref, sub_ref, out_ref, vaug_all_ref, *, out_scale, blk):
    for hh in range(ATT_HEADS_PER_STEP):
        _attn_head(hh, lam_ref, q_ref, k_ref, v_ref, bias_ref, sub_ref, out_ref, vaug_all_ref.at[hh],
                   out_scale=out_scale, blk=blk)


def _attn_head(hh, lam_ref, q_ref, k_ref, v_ref, bias_ref, sub_ref, out_ref, vaug_ref, *, out_scale, blk):
    s_len = q_ref.shape[1]
    d = DA_HEAD_DIM
    cs = slice(hh * LANES, (hh + 1) * LANES)
    lane_s = lax.broadcasted_iota(jnp.int32, (s_len, DA_V_DIM), 1)
    vaug_ref[:, :DA_V_DIM] = v_ref[0, :, cs]
    vaug_ref[:, DA_V_DIM:] = jnp.where(lane_s == 0, 1.0, 0.0).astype(BF16)
    lam = lam_ref[0]
    lane = lax.broadcasted_iota(jnp.int32, (blk, LANES), 1)
    for qi in range(s_len // blk):
        q0, q1 = qi * blk, (qi + 1) * blk
        qs = q_ref[0, q0:q1, cs] * (d ** -0.5)
        zero = jnp.zeros_like(qs)
        heads = []
        for qm in (jnp.where(lane < d, qs, zero), jnp.where(lane >= d, qs, zero)):
            score = lambda a, b: lax.dot_general(qm, k_ref[0, a:b, cs], _NT, preferred_element_type=F32)
            parts = [score(q0, q1) + bias_ref[hh, 0]]
            if qi >= 1:
                parts.append(score(q0 - blk, q0) + bias_ref[hh, 1])
            if qi >= 2:
                parts.append(score(0, q0 - blk))
            m = functools.reduce(jnp.maximum, [jnp.max(p, axis=-1, keepdims=True) for p in parts])
            pexp = [jnp.exp(p - m).astype(BF16) for p in parts]
            o = _dot(pexp[0], vaug_ref[q0:q1, :])
            if qi >= 1:
                o = o + _dot(pexp[1], vaug_ref[q0 - blk:q0, :])
            if qi >= 2:
                o = o + _dot(pexp[2], vaug_ref[0:q0 - blk, :])
            heads.append(o[:, :DA_V_DIM] / o[:, DA_V_DIM:DA_V_DIM + 1])
        o = heads[0] - lam * heads[1]
        ms = jnp.mean(o * o, axis=-1, keepdims=True)
        o = o * lax.rsqrt(ms + NORM_EPS) * sub_ref[...] * out_scale
        out_ref[0, q0:q1, cs] = o.astype(out_ref.dtype)


def _attention(lam, p3, bias_tiles, subln, out_scale):
    bsz, s, _ = p3.shape
    blk = ATT_BLOCK
    w = DA_HEADS * DA_V_DIM
    q_blk = 4 * M_HEADS
    k_blk = q_blk + DA_HEADS
    v_blk = k_blk + DA_HEADS
    nh = ATT_HEADS_PER_STEP
    assert q_blk % nh == 0 and k_blk % nh == 0 and v_blk % nh == 0 and DA_HEADS % nh == 0
    col = lambda base: pl.BlockSpec((1, s, nh * LANES), lambda b, h: (b, 0, base // nh + h))
    return pl.pallas_call(
        functools.partial(_attn_kernel, out_scale=out_scale, blk=blk),
        out_shape=jax.ShapeDtypeStruct((bsz, s, w), BF16),
        grid=(bsz, DA_HEADS // nh),
        in_specs=[pl.BlockSpec(memory_space=pltpu.SMEM),
                  col(q_blk), col(k_blk), col(v_blk),
                  pl.BlockSpec((nh, 2, blk, blk), lambda b, h: (h, 0, 0, 0)),
                  pl.BlockSpec((1, DA_V_DIM), lambda b, h: (0, 0))],
        out_specs=pl.BlockSpec((1, s, nh * DA_V_DIM), lambda b, h: (b, 0, h)),
        scratch_shapes=[pltpu.VMEM((nh, s, 2 * DA_V_DIM), BF16)],
        compiler_params=_cparams("arbitrary", "arbitrary"),
        name="diff_attention",
    )(lam, p3, p3, p3, bias_tiles, subln)


def _pack_bf16_pairs(v):
    half = v.shape[1] // 2
    hi = lax.bitcast_convert_type(v[:, :half].astype(F32), jnp.uint32)
    lo = lax.bitcast_convert_type(v[:, half:].astype(F32), jnp.uint32)
    return hi | (lo >> 16)


def _unpack_bf16_pairs(w):
    hi = lax.bitcast_convert_type(w & jnp.uint32(0xFFFF0000), F32)
    lo = lax.bitcast_convert_type(w << 16, F32)
    return hi, lo


def _init_counts(cnt_ref):
    @pl.when(pl.program_id(0) == 0)
    def _():
        cnt_ref[...] = jnp.zeros_like(cnt_ref)


def _router_epilogue(xn, tile, g_ref, wr_ref, br_ref, triu_ref, hrow_ref, route_ref, cnt_ref):
    tm, d = xn.shape
    h2 = _rms(xn, g_ref[...]).astype(BF16)
    hrow_ref[:, :d // 2] = _pack_bf16_pairs(h2)

    lg = lax.dot_general(wr_ref[...], h2, _NT, preferred_element_type=F32) + br_ref[...]
    gl = [lg[g:g + 1, :] for g in range(MOE_GROUPS)]
    gmax = functools.reduce(jnp.maximum, gl)
    gsel = jnp.where(gl[0] == gmax, 0.0, jnp.where(gl[1] == gmax, 1.0, jnp.where(gl[2] == gmax, 2.0, 3.0)))
    pg = 1.0 / functools.reduce(lambda a, b: a + b, [jnp.exp(x - gmax) for x in gl])

    ev = []
    for j in range(EXPERTS_PER_GROUP):
        acc = jnp.zeros_like(gmax)
        for g in range(MOE_GROUPS):
            r0 = SUBLANES + g * EXPERTS_PER_GROUP + j
            acc = jnp.where(gsel == float(g), lg[r0:r0 + 1, :], acc)
        ev.append(acc)
    v1 = functools.reduce(jnp.maximum, ev)
    i1 = jnp.where(ev[0] == v1, 0.0, jnp.where(ev[1] == v1, 1.0, jnp.where(ev[2] == v1, 2.0, 3.0)))
    ev2 = [jnp.where(i1 == float(j), -jnp.inf, ev[j]) for j in range(EXPERTS_PER_GROUP)]
    v2 = functools.reduce(jnp.maximum, ev2)
    hit = [(ev2[j] == v2) & (i1 != float(j)) for j in range(EXPERTS_PER_GROUP)]
    i2 = jnp.where(hit[0], 0.0, jnp.where(hit[1], 1.0, jnp.where(hit[2], 2.0, 3.0)))
    e21 = jnp.exp(v2 - v1)
    w1 = 1.0 / (1.0 + e21)
    w2 = e21 * w1
    first_low = i1 < i2
    lo = jnp.minimum(i1, i2)
    hi = jnp.maximum(i1, i2)
    c_lo = pg * jnp.where(first_low, w1, w2)
    c_hi = pg * jnp.where(first_low, w2, w1)
    pair = lo * (7.0 - lo) * 0.5 + hi - lo - 1.0
    bucket = gsel * float(N_PAIRS) + pair

    rows = lax.broadcasted_iota(jnp.int32, (32, tm), 0).astype(F32)
    onehot = rows == bucket
    cum = _dot(jnp.where(onehot, 1.0, 0.0).astype(BF16), triu_ref[...])
    carry = cnt_ref[:, 0:1]
    rank = jnp.sum(jnp.where(onehot, cum + carry, 0.0), axis=0, keepdims=True) - 1.0
    cnt_ref[...] = jnp.broadcast_to(carry + cum[:, tm - 1:tm], cnt_ref.shape)

    token = (tile * tm + 1 + lax.broadcasted_iota(jnp.int32, (1, tm), 1)).astype(F32)
    rid = lax.broadcasted_iota(jnp.int32, (LANES, tm), 0)
    rec = jnp.where(rid == 0, bucket, jnp.where(rid == 1, c_lo, jnp.where(rid == 2, c_hi,
                    jnp.where(rid == 3, rank, jnp.where(rid == 4, token, 0.0)))))
    hrow_ref[:, d // 2:] = lax.bitcast_convert_type(rec.T, jnp.uint32)
    route_ref[...] = rec[:SUBLANES, :]


_EPI_OUT_SHAPES = lambda t, d: (jax.ShapeDtypeStruct((t, d), F32),
                                jax.ShapeDtypeStruct((t, d // 2 + LANES), jnp.uint32),
                                jax.ShapeDtypeStruct((SUBLANES, t), F32),
                                jax.ShapeDtypeStruct((32, LANES), F32))


def _epi_out_specs(tm, d):
    return (pl.BlockSpec((tm, d), lambda i: (i, 0)),
            pl.BlockSpec((tm, d // 2 + LANES), lambda i: (i, 0)),
            pl.BlockSpec((SUBLANES, tm), lambda i: (0, i)),
            pl.BlockSpec((32, LANES), lambda i: (0, 0)))


def _epi_in_specs(tm, d):
    return [pl.BlockSpec((1, d), lambda i: (0, 0)),
            pl.BlockSpec((32, d), lambda i: (0, 0)),
            pl.BlockSpec((32, 1), lambda i: (0, 0)),
            pl.BlockSpec((tm, tm), lambda i: (0, 0))]


def _outproj_even_kernel(hm_ref, ha_ref, x_ref, wo_ref, g_ref, wr_ref, br_ref, triu_ref,
                         xn_ref, hrow_ref, route_ref, cnt_ref):
    _init_counts(cnt_ref)
    half = hm_ref.shape[1]
    tm = triu_ref.shape[0]
    sub = x_ref.shape[0] // tm
    for j in range(sub):
        rs = slice(j * tm, (j + 1) * tm)
        y = _dot(hm_ref[rs, :], wo_ref[:half, :]) + _dot(ha_ref[rs, :], wo_ref[half:, :])
        xn = x_ref[rs, :] + y
        xn_ref[rs, :] = xn
        _router_epilogue(xn, pl.program_id(0) * sub + j, g_ref, wr_ref, br_ref, triu_ref,
                         hrow_ref.at[rs, :], route_ref.at[:, rs], cnt_ref)


def _outproj_even(hm, ha, x2d, w_out, g, wr, br, triu):
    t, d = x2d.shape
    tm = OUTPROJ_SUBTILES * ROW_TILE
    half = hm.shape[1]
    return pl.pallas_call(
        _outproj_even_kernel,
        out_shape=_EPI_OUT_SHAPES(t, d),
        grid=(t // tm,),
        in_specs=[pl.BlockSpec((tm, half), lambda i: (i, 0)),
                  pl.BlockSpec((tm, half), lambda i: (i, 0)),
                  pl.BlockSpec((tm, d), lambda i: (i, 0)),
                  pl.BlockSpec((2 * half, d), lambda i: (0, 0))] + _epi_in_specs(ROW_TILE, d),
        out_specs=_epi_out_specs(tm, d),
        compiler_params=_cparams("arbitrary"),
        name="outproj_even",
    )(hm, ha, x2d, w_out, g, wr, br, triu)


def _conv_mixer_kernel(x_ref, y_ref, gm_ref, wi_ref, cw_ref, wo_ref, g_ref, wr_ref, br_ref, triu_ref,
                       xn_ref, h2_ref, meta_ref, cnt_ref, tail_ref, *, tiles_per_seq):
    _init_counts(cnt_ref)
    tm = triu_ref.shape[0]
    d = x_ref.shape[1]
    sub = x_ref.shape[0] // tm
    assert tiles_per_seq % sub == 0

    @pl.when((pl.program_id(0) * sub) % tiles_per_seq == 0)
    def _():
        tail_ref[...] = jnp.zeros_like(tail_ref)

    row = lax.broadcasted_iota(jnp.int32, (tm, 512), 0)
    for s in range(sub):
        rs = slice(s * tm, (s + 1) * tm)
        x = x_ref[rs, :] + jnp.concatenate(_unpack_bf16_pairs(y_ref[rs, :]), axis=1)
        h = _rms(x, gm_ref[...]).astype(BF16)
        acc = jnp.zeros((tm, d), F32)
        for j in range(d // 512):
            sl = slice(j * 512, (j + 1) * 512)
            bg = _dot(h, wi_ref[:, sl])
            cg = _dot(h, wi_ref[:, d + j * 512:d + (j + 1) * 512])
            u = _dot(h, wi_ref[:, 2 * d + j * 512:2 * d + (j + 1) * 512])
            z = cg * u
            prev1 = tail_ref[SUBLANES - 1:SUBLANES, sl]
            prev2 = tail_ref[SUBLANES - 2:SUBLANES - 1, sl]
            z1 = jnp.where(row == 0, prev1, pltpu.roll(z, 1, axis=0))
            z2 = jnp.where(row == 0, prev2, jnp.where(row == 1, prev1, pltpu.roll(z, 2, axis=0)))
            zc = cw_ref[0:1, sl] * z2 + cw_ref[1:2, sl] * z1 + cw_ref[2:3, sl] * z
            tail_ref[:, sl] = z[tm - SUBLANES:, :]
            acc = acc + _dot((bg * zc).astype(BF16), wo_ref[sl, :])
        xn = x + acc
        xn_ref[rs, :] = xn
        _router_epilogue(xn, pl.program_id(0) * sub + s, g_ref, wr_ref, br_ref, triu_ref,
                         h2_ref.at[rs, :], meta_ref.at[:, rs], cnt_ref)


def _conv_mixer(x2d, ymoe, gm, w_in, conv_w, w_out, g, wr, br, triu, seq):
    t, d = x2d.shape
    tm = CONV_SUBTILES * ROW_TILE
    const = lambda shape: pl.BlockSpec(shape, lambda i: (0, 0), pipeline_mode=pl.Buffered(1))
    return pl.pallas_call(
        functools.partial(_conv_mixer_kernel, tiles_per_seq=seq // ROW_TILE),
        out_shape=_EPI_OUT_SHAPES(t, d),
        grid=(t // tm,),
        in_specs=[pl.BlockSpec((tm, d), lambda i: (i, 0)),
                  pl.BlockSpec((tm, d // 2), lambda i: (i, 0)),
                  pl.BlockSpec((1, d), lambda i: (0, 0)),
                  const((d, 3 * d)),
                  pl.BlockSpec((SUBLANES, d), lambda i: (0, 0)),
                  const((d, d))] + _epi_in_specs(ROW_TILE, d),
        out_specs=_epi_out_specs(tm, d),
        scratch_shapes=[pltpu.VMEM((SUBLANES, d), F32)],
        compiler_params=_cparams("arbitrary"),
        name="conv_mixer",
    )(x2d, ymoe, gm, w_in, conv_w, w_out, g, wr, br, triu)


def _row_scatter_kernel(ztile_ref, zflag_ref, dst_ref, h_hbm, out_hbm, buf0, buf1, buf2, lsem, ssem, zsem):
    i = pl.program_id(0)
    n = pl.num_programs(0)
    bufs = (buf0, buf1, buf2)
    tm = buf0.shape[0]

    def load(j, s):
        return pltpu.make_async_copy(h_hbm.at[pl.ds(pl.multiple_of(j * tm, tm), tm)], bufs[s], lsem.at[s])

    wait_rows = lambda s: pltpu.make_async_copy(bufs[s], out_hbm.at[pl.ds(0, tm)], ssem.at[s]).wait()

    @pl.when(i == 0)
    def _():
        buf2[...] = jnp.zeros_like(buf2)
        fill = lambda k: pltpu.make_async_copy(
            buf2, out_hbm.at[pl.ds(pl.multiple_of(ztile_ref[k] * tm, tm), tm)], zsem)
        for k in range(ztile_ref.shape[0]):
            @pl.when(zflag_ref[k] != 0)
            def _(k=k):
                fill(k).start()
        for k in range(ztile_ref.shape[0]):
            @pl.when(zflag_ref[k] != 0)
            def _(k=k):
                fill(k).wait()
        load(0, 0).start()

    for s in range(3):
        @pl.when(i % 3 == s)
        def _(s=s):
            @pl.when(i + 1 < n)
            def _():
                load(i + 1, (s + 1) % 3).start()

            load(i, s).wait()
            for r in range(tm):
                pltpu.make_async_copy(bufs[s].at[pl.ds(r, 1)], out_hbm.at[pl.ds(dst_ref[0, 0, r], 1)],
                                      ssem.at[s]).start(priority=r % 2)

            @pl.when(i >= 1)
            def _():
                wait_rows((s + 2) % 3)

            @pl.when(i == n - 1)
            def _():
                wait_rows(s)


def _row_scatter(hrows, dest, zero_tiles, zero_flags, n_rows):
    t, wrow = hrows.shape
    tm = MOE_TILE
    return pl.pallas_call(
        _row_scatter_kernel,
        out_shape=jax.ShapeDtypeStruct((n_rows, wrow), hrows.dtype),
        grid_spec=pltpu.PrefetchScalarGridSpec(
            num_scalar_prefetch=2,
            grid=(t // tm,),
            in_specs=[pl.BlockSpec((1, 1, tm), lambda i, *_: (i, 0, 0), memory_space=pltpu.SMEM),
                      pl.BlockSpec(memory_space=pl.ANY)],
            out_specs=pl.BlockSpec(memory_space=pl.ANY),
            scratch_shapes=[pltpu.VMEM((tm, wrow), hrows.dtype),
                            pltpu.VMEM((tm, wrow), hrows.dtype),
                            pltpu.VMEM((tm, wrow), hrows.dtype),
                            pltpu.SemaphoreType.DMA((3,)),
                            pltpu.SemaphoreType.DMA((3,)),
                            pltpu.SemaphoreType.DMA]),
        compiler_params=_cparams("arbitrary"),
        name="row_scatter",
    )(zero_tiles, zero_flags, dest.reshape(t // tm, 1, tm), hrows)


def _moe_kernel(elo_ref, ehi_ref, valid_ref, fresh_ref,
                x_ref,
                wg_lo, wu_lo, wd_lo, wg_hi, wu_hi, wd_hi,
                y_hbm,
                obuf0, obuf1, idv0, idv1, ids0, ids1, wbuf_in, wbuf_out, ssem, isem):
    i = pl.program_id(0)
    tm, half = obuf0.shape
    n_tok = y_hbm.shape[0] - 2 * tm
    obufs, idvs, idss = (obuf0, obuf1), (idv0, idv1), (ids0, ids1)
    valid = valid_ref[i] != 0
    prev_valid = valid_ref[jnp.maximum(i - 1, 0)] != 0
    spare = n_tok + lax.broadcasted_iota(jnp.int32, (SUBLANES, tm), 1)

    ids_copy = lambda s: pltpu.make_async_copy(idvs[s], idss[s], isem.at[s])
    row_scatter = lambda r, s: pltpu.make_async_copy(obufs[s].at[pl.ds(r, 1)],
                                                     y_hbm.at[pl.ds(idss[s][0, r], 1)], ssem.at[s])
    wait_scatter = lambda s: pltpu.make_async_copy(obufs[s], y_hbm.at[pl.ds(0, tm)], ssem.at[s]).wait()

    @pl.when(i == 0)
    def _():
        obuf0[...] = jnp.zeros_like(obuf0)
        obuf1[...] = jnp.zeros_like(obuf1)
        idv1[...] = spare
        ids_copy(1).start()
        pltpu.make_async_copy(obuf0, y_hbm.at[pl.ds(n_tok + tm, tm)], ssem.at[0]).start()

    for k, (wg, wu, wd) in enumerate(((wg_lo, wu_lo, wd_lo), (wg_hi, wu_hi, wd_hi))):
        @pl.when((fresh_ref[i] & (1 << k)) != 0)
        def _(k=k, wg=wg, wu=wu, wd=wd):
            wbuf_in[2 * k] = wg[0].astype(BF16)
            wbuf_in[2 * k + 1] = wu[0].astype(BF16)
            wbuf_out[k] = wd[0].astype(BF16)

    for s in range(2):
        @pl.when(valid & (i % 2 == s))
        def _(s=s):
            rows = x_ref[...]
            rec = lax.bitcast_convert_type(rows[:, half:], F32)
            token = rec.T[4:5, :]
            idvs[s][...] = jnp.where(token > 0.0, token.astype(jnp.int32) - 1, spare)
            ids_copy(s).start()
            ids_copy(1 - s).wait()
            for r in range(tm):
                row_scatter(r, 1 - s).start(priority=r % 2)
            xa, xb = _unpack_bf16_pairs(rows[:, :half])
            x = jnp.concatenate([xa.astype(BF16), xb.astype(BF16)], axis=1)

            def ffn(k):
                a = _dot(x, wbuf_in[2 * k])
                a = (a * jax.nn.sigmoid(a)) * _dot(x, wbuf_in[2 * k + 1])
                return _dot(a.astype(BF16), wbuf_out[k])

            out = rec[:, 1:2] * ffn(0) + rec[:, 2:3] * ffn(1)
            wait_scatter(s)
            obufs[s][...] = _pack_bf16_pairs(out.astype(BF16))

        @pl.when(jnp.logical_not(valid) & prev_valid & (i % 2 == s))
        def _(s=s):
            wait_scatter(s)
            ids_copy(1 - s).wait()

            @pl.loop(0, tm)
            def _(r):
                row_scatter(r, 1 - s).start()

            wait_scatter(1 - s)


def _moe(tiles, xs, n_tok, w_gate, w_up, w_down):
    tile_elo, tile_ehi, tile_valid, tile_fresh = tiles
    n_tiles = tile_elo.shape[0]
    wrow = xs.shape[1]
    d = w_gate.shape[1]
    ff = w_gate.shape[2]
    tm = MOE_TILE
    lo = lambda i, elo, *_: (elo[i], 0, 0)
    hi = lambda i, elo, ehi, *_: (ehi[i], 0, 0)
    return pl.pallas_call(
        _moe_kernel,
        out_shape=jax.ShapeDtypeStruct((n_tok + 2 * tm, d // 2), jnp.uint32),
        grid_spec=pltpu.PrefetchScalarGridSpec(
            num_scalar_prefetch=4,
            grid=(n_tiles,),
            in_specs=[pl.BlockSpec((tm, wrow), lambda i, *_: (i, 0)),
                      pl.BlockSpec((1, d, ff), lo), pl.BlockSpec((1, d, ff), lo), pl.BlockSpec((1, ff, d), lo),
                      pl.BlockSpec((1, d, ff), hi), pl.BlockSpec((1, d, ff), hi), pl.BlockSpec((1, ff, d), hi)],
            out_specs=pl.BlockSpec(memory_space=pl.ANY),
            scratch_shapes=[pltpu.VMEM((tm, d // 2), jnp.uint32),
                            pltpu.VMEM((tm, d // 2), jnp.uint32),
                            pltpu.VMEM((SUBLANES, tm), jnp.int32),
                            pltpu.VMEM((SUBLANES, tm), jnp.int32),
                            pltpu.SMEM((SUBLANES, tm), jnp.int32),
                            pltpu.SMEM((SUBLANES, tm), jnp.int32),
                            pltpu.VMEM((4, d, ff), BF16),
                            pltpu.VMEM((2, ff, d), BF16),
                            pltpu.SemaphoreType.DMA((2,)),
                            pltpu.SemaphoreType.DMA((2,))]),
        compiler_params=_cparams("arbitrary"),
        name="grouped_moe",
    )(tile_elo, tile_ehi, tile_valid, tile_fresh, xs, w_gate, w_up, w_down, w_gate, w_up, w_down)


def _final_kernel(x_ref, y_ref, g_ref, out_ref):
    out_ref[...] = _rms(x_ref[...] + jnp.concatenate(_unpack_bf16_pairs(y_ref[...]), axis=1), g_ref[...])


def _final(x2d, ymoe, g):
    t, d = x2d.shape
    tm = FINAL_ROWS
    return pl.pallas_call(
        _final_kernel,
        out_shape=jax.ShapeDtypeStruct((t, d), F32),
        grid=(t // tm,),
        in_specs=[pl.BlockSpec((tm, d), lambda i: (i, 0)),
                  pl.BlockSpec((tm, d // 2), lambda i: (i, 0)),
                  pl.BlockSpec((1, d), lambda i: (0, 0))],
        out_specs=pl.BlockSpec((tm, d), lambda i: (i, 0)),
        compiler_params=_cparams("arbitrary"),
        name="final_norm",
    )(x2d, ymoe, g)


_PAIRS = [(a, b) for a in range(EXPERTS_PER_GROUP) for b in range(a + 1, EXPERTS_PER_GROUP)]
_BUCKET_ELO = np.array([g * EXPERTS_PER_GROUP + _PAIRS[p][0] for g in range(MOE_GROUPS) for p in range(N_PAIRS)], np.int32)
_BUCKET_EHI = np.array([g * EXPERTS_PER_GROUP + _PAIRS[p][1] for g in range(MOE_GROUPS) for p in range(N_PAIRS)], np.int32)


def _moe_layer(hrows, route, cnt, layer, w_gate, w_up, w_down):
    t = hrows.shape[0]
    tm = MOE_TILE
    n_tiles = t // tm + N_BUCKETS
    bucket = route[0].astype(jnp.int32)
    rank = route[3].astype(jnp.int32)
    counts = cnt[:N_BUCKETS, 0].astype(jnp.int32)
    padded = ((counts + tm - 1) // tm) * tm
    ends = jnp.cumsum(padded)
    dest = (ends - padded)[bucket] + rank
    tile_start = jnp.arange(n_tiles, dtype=jnp.int32) * tm
    tile_valid = tile_start < ends[-1]
    last_start = jnp.maximum(ends[-1] - tm, 0)
    tile_bucket = jnp.sum((ends[None, :] <= jnp.minimum(tile_start, last_start)[:, None]).astype(jnp.int32), axis=1)
    base = layer * N_EXPERTS
    tile_elo = jnp.asarray(_BUCKET_ELO)[tile_bucket] + base
    tile_ehi = jnp.asarray(_BUCKET_EHI)[tile_bucket] + base
    changed = lambda e: tile_valid & (e != jnp.concatenate([jnp.full((1,), -1, jnp.int32), e[:-1]]))
    tile_fresh = changed(tile_elo).astype(jnp.int32) + 2 * changed(tile_ehi).astype(jnp.int32)
    tiles = (tile_elo, tile_ehi, tile_valid.astype(jnp.int32), tile_fresh)
    idle = t // tm + jnp.arange(N_BUCKETS, dtype=jnp.int32)
    zero_tiles = jnp.concatenate([jnp.maximum(ends // tm - 1, 0), idle])
    zero_flags = jnp.concatenate([counts % tm != 0, idle * tm >= ends[-1]]).astype(jnp.int32)
    xs = _row_scatter(hrows, dest, zero_tiles, zero_flags, n_tiles * tm)
    return _moe(tiles, xs, t, w_gate, w_up, w_down)


def _router_weights(wg, bg, we, be):
    d = wg.shape[0]
    wr = jnp.zeros((32, d), F32).at[:MOE_GROUPS].set(wg.T).at[SUBLANES:SUBLANES + N_EXPERTS].set(we.T)
    br = jnp.zeros((32, 1), F32).at[:MOE_GROUPS, 0].set(bg).at[SUBLANES:SUBLANES + N_EXPERTS, 0].set(be)
    return wr.astype(BF16), br


def kernel(x, rel_bias, norm_mix, norm_ffn, norm_final, w_in_even, w_out_even, m_igate_b, m_fgate_b,
           m_head_norm, lam_q1, lam_k1, lam_q2, lam_k2, da_subln, w_in_odd, conv_w, w_out_odd,
           router_group_w, router_group_b, router_expert_w, router_expert_b,
           exp_w_gate, exp_w_up, exp_w_down):
    bsz, seq, d = x.shape
    t = bsz * seq
    x2d = x.reshape(t, d)
    mw = M_HEADS * M_HEAD_DIM

    w_in = w_in_even[0]
    gate0 = 4 * mw
    w_main = jnp.concatenate([w_in[:, :gate0], w_in[:, gate0 + 2 * M_HEADS:]], axis=1).astype(BF16)
    w_gates = w_in[:, gate0:gate0 + 2 * M_HEADS]
    wg_col = jnp.zeros((d, LANES), F32).at[:, :2 * M_HEADS].set(w_gates).astype(BF16)
    wg_row = w_gates.T.astype(BF16)
    gate_b = jnp.concatenate([m_igate_b[0], m_fgate_b[0]]).astype(F32)
    b_col = jnp.zeros((1, LANES), F32).at[0, :2 * M_HEADS].set(gate_b)
    b_row = gate_b.reshape(2 * M_HEADS, 1)
    triu = jnp.triu(jnp.ones((ROW_TILE, ROW_TILE), F32)).astype(BF16)

    p, gc, gr = _inproj_even(x2d, norm_mix[0].reshape(1, d), w_main, wg_col, wg_row)
    p3 = p.reshape(bsz, seq, -1)
    hm = _mlstm(p3, gc.reshape(bsz, seq, LANES), gr, b_col, b_row, m_head_norm[0].reshape(1, mw))

    lam_init = 0.8 - 0.6 * math.exp(-0.3 * 0)
    lam = (jnp.exp(jnp.sum(lam_q1[0].astype(F32) * lam_k1[0].astype(F32)))
           - jnp.exp(jnp.sum(lam_q2[0].astype(F32) * lam_k2[0].astype(F32))) + lam_init).reshape(1)
    ha = _attention(lam, p3, _bias_tiles(rel_bias.astype(F32), ATT_BLOCK),
                    da_subln[0].reshape(1, DA_V_DIM).astype(F32), 1.0 - lam_init)

    wr0, br0 = _router_weights(router_group_w[0], router_group_b[0], router_expert_w[0], router_expert_b[0])
    x1, h2, meta, cnt = _outproj_even(hm.reshape(t, mw), ha.reshape(t, -1), x2d, w_out_even[0].astype(BF16),
                                      norm_ffn[0].reshape(1, d), wr0, br0, triu)
    ff = exp_w_gate.shape[-1]
    experts = (exp_w_gate.reshape(-1, d, ff), exp_w_up.reshape(-1, d, ff), exp_w_down.reshape(-1, ff, d))
    y = _moe_layer(h2, meta, cnt, 0, *experts)

    wr1, br1 = _router_weights(router_group_w[1], router_group_b[1], router_expert_w[1], router_expert_b[1])
    cw = jnp.zeros((SUBLANES, d), F32).at[:conv_w.shape[1]].set(conv_w[0])
    x3, h2, meta, cnt = _conv_mixer(x1, y, norm_mix[1].reshape(1, d), w_in_odd[0].astype(BF16), cw,
                                    w_out_odd[0].astype(BF16), norm_ffn[1].reshape(1, d), wr1, br1, triu, seq)
    y = _moe_layer(h2, meta, cnt, 1, *experts)

    return _final(x3, y, norm_final.reshape(1, d)).reshape(bsz, seq, d)
```
